```python
import jax, jax.numpy as jnp
from jax import lax
import numpy as np

D_MODEL = 2048
BATCH = 8
SEQ = 2048
DEPTH = 1

HEAD_DIM = 128
ATTN_HEADS = 12
DILATION_GROUPS = ((128, 1), (512, 4), (2048, 16))
N_ATTN_GROUPS = len(DILATION_GROUPS)
ATTN_WIDTH = ATTN_HEADS * HEAD_DIM
ROPE_THETA = 10000.0
SSD_WIDTH = 2 * D_MODEL
SSD_HEADDIM = 64
SSD_HEADS = SSD_WIDTH // SSD_HEADDIM
SSD_GROUPS = 8
SSD_HEADS_PER_GROUP = SSD_HEADS // SSD_GROUPS
SSD_STATE = 128
SSD_CHUNK = 128
CONV_WIDTH = 5
CONV_CH = SSD_WIDTH + 2 * SSD_GROUPS * SSD_STATE
EPS = 1e-6
QKV_COLS = 3 * N_ATTN_GROUPS * ATTN_WIDTH
SPLIT_SIZES = (QKV_COLS, ATTN_WIDTH, SSD_WIDTH, CONV_CH, 2 * SSD_HEADS, D_MODEL, D_MODEL)
IN_COLS = sum(SPLIT_SIZES)

kernel_name = "hybrid_dilated_attn_bissd_block"


def rms_norm(t, g):
    tf = t.astype(jnp.float32)
    tf = tf * lax.rsqrt(jnp.mean(tf * tf, axis=-1, keepdims=True) + EPS)
    return (tf * g.astype(jnp.float32)).astype(t.dtype)


def rotary(t, pos):
    dh = t.shape[-1]
    inv = ROPE_THETA ** (-jnp.arange(0, dh, 2, dtype=jnp.float32) / dh)
    ang = pos.astype(jnp.float32)[..., None] * inv
    cos = jnp.cos(ang)[:, :, None, :]
    sin = jnp.sin(ang)[:, :, None, :]
    tf = t.astype(jnp.float32)
    t1, t2 = tf[..., : dh // 2], tf[..., dh // 2:]
    return jnp.concatenate([t1 * cos - t2 * sin, t2 * cos + t1 * sin], axis=-1).astype(t.dtype)


def dilated_window_attention(q, k, v, window, dilation):
    b, s, h, dh = q.shape
    n_side = window // (2 * dilation)
    blk = n_side
    L = s // dilation
    Lp = -(-L // blk) * blk
    nb = Lp // blk

    def to_cls(t):
        return t.reshape(b, L, dilation, h, dh)

    qc = jnp.pad(to_cls(q), ((0, 0), (0, Lp - L), (0, 0), (0, 0), (0, 0)))
    qc = qc.reshape(b, nb, blk, dilation, h, dh)

    def key_blocks(t):
        tp = jnp.pad(to_cls(t), ((0, 0), (blk, Lp - L + blk), (0, 0), (0, 0), (0, 0)))
        tp = tp.reshape(b, nb + 2, blk, dilation, h, dh)
        return jnp.concatenate([tp[:, :-2], tp[:, 1:-1], tp[:, 2:]], axis=2)

    kb, vb = key_blocks(k), key_blocks(v)
    scores = jnp.einsum('bnqrhe,bnkrhe->bnrhqk', qc, kb).astype(jnp.float32) * (dh ** -0.5)
    qi = jnp.arange(nb)[:, None] * blk + jnp.arange(blk)[None, :]
    ki = jnp.arange(nb)[:, None] * blk + jnp.arange(3 * blk)[None, :] - blk
    kk = ki[:, None, :]
    valid = (jnp.abs(qi[:, :, None] - kk) <= n_side) & (kk >= 0) & (kk < L)
    scores = jnp.where(valid[None, :, None, None], scores, -jnp.inf)
    m = jnp.max(scores, axis=-1, keepdims=True)
    p = jnp.exp(scores - m)
    den = jnp.sum(p, axis=-1)
    o = jnp.einsum('bnrhqk,bnkrhe->bnqrhe', p, vb.astype(jnp.float32))
    o = o / jnp.transpose(den, (0, 1, 4, 2, 3))[..., None]
    lse = jnp.transpose(m[..., 0] + jnp.log(den), (0, 1, 4, 2, 3))
    o = o.reshape(b, Lp, dilation, h, dh)[:, :L].reshape(b, s, h, dh)
    lse = lse.reshape(b, Lp, dilation, h)[:, :L].reshape(b, s, h)
    return o, lse


def ssd_chunked(xh, dt, A, Bm, Cm):
    b, s, G, E, P = xh.shape
    N = Bm.shape[-1]
    Q = SSD_CHUNK
    nc = s // Q
    xc = (xh * dt[..., None]).reshape(b, nc, Q, G, E, P)
    Bc = Bm.reshape(b, nc, Q, G, N)
    Cc = Cm.reshape(b, nc, Q, G, N)
    a = (dt * A).reshape(b, nc, Q, G, E).transpose(0, 1, 3, 4, 2)
    acum = jnp.cumsum(a, axis=-1)
    lower = jnp.tril(jnp.ones((Q, Q), dtype=bool))
    Lmat = jnp.exp(jnp.where(lower, acum[..., :, None] - acum[..., None, :], -jnp.inf))
    cb = jnp.einsum('bclgn,bcsgn->bcgls', Cc, Bc)
    y_diag = jnp.einsum('bcgls,bcgels,bcsgep->bclgep', cb, Lmat, xc)
    decay = jnp.exp(acum[..., -1:] - acum)
    states = jnp.einsum('bclgn,bcgel,bclgep->bcgepn', Bc, decay, xc)
    chunk_decay = jnp.exp(acum[..., -1])

    def step(carry, inp):
        st, dec = inp
        return carry * dec[..., None, None] + st, carry

    init = jnp.zeros((b, G, E, P, N), jnp.float32)
    _, prev = lax.scan(step, init, (jnp.moveaxis(states, 1, 0), jnp.moveaxis(chunk_decay, 1, 0)))
    prev = jnp.moveaxis(prev, 0, 1)
    y_off = jnp.einsum('bclgn,bcgepn,bcgel->bclgep', Cc, prev, jnp.exp(acum))
    return (y_diag + y_off).reshape(b, s, G, E, P)


def centred_depthwise_conv(t, w, bias):
    ch = t.shape[-1]
    pad = (CONV_WIDTH - 1) // 2
    out = lax.conv_general_dilated(t, w.astype(t.dtype)[:, None, :], window_strides=(1,),
                                   padding=[(pad, pad)], dimension_numbers=('NWC', 'WIO', 'NWC'),
                                   feature_group_count=ch)
    return out + bias.astype(t.dtype)


def setup_inputs(seed: int = 0) -> dict:
    key = jax.random.key(seed)
    ks = jax.random.split(key, 20)
    f32 = jnp.float32
    D = D_MODEL
    x = jax.random.normal(ks[0], (BATCH, SEQ, D), f32)
    c = jax.random.normal(ks[1], (BATCH, D), f32)
    positions = (jnp.arange(SEQ, dtype=jnp.int32)[None, :]
                 + jax.random.randint(ks[2], (BATCH, 1), 0, 4096, dtype=jnp.int32))
    norm_g = 1.0 + 0.02 * jax.random.normal(ks[3], (DEPTH, D), f32)
    w_ada = 0.5 * D ** -0.5 * jax.random.normal(ks[4], (DEPTH, D, 3 * D), f32)
    b_ada = 0.02 * jax.random.normal(ks[5], (DEPTH, 3 * D), f32)
    w_in = D ** -0.5 * jax.random.normal(ks[6], (DEPTH, D, IN_COLS), f32)
    conv_w = CONV_WIDTH ** -0.5 * jax.random.normal(ks[7], (DEPTH, CONV_WIDTH, CONV_CH), f32)
    conv_b = 0.02 * jax.random.normal(ks[8], (DEPTH, CONV_CH), f32)
    dt0 = jnp.exp(jax.random.uniform(ks[9], (DEPTH, 2, SSD_HEADS), f32, np.log(1e-3), np.log(1e-1)))
    dt_bias = dt0 + jnp.log(-jnp.expm1(-dt0))
    a_log = jnp.log(jax.random.uniform(ks[10], (DEPTH, 2, SSD_HEADS), f32, 1.0, 16.0))
    d_skip = 1.0 + 0.1 * jax.random.normal(ks[11], (DEPTH, SSD_HEADS), f32)
    ssd_norm_g = 1.0 + 0.02 * jax.random.normal(ks[12], (DEPTH, SSD_WIDTH), f32)
    w_br_attn = ATTN_WIDTH ** -0.5 * jax.random.normal(ks[13], (DEPTH, ATTN_WIDTH, D), f32)
    w_br_ssd = SSD_WIDTH ** -0.5 * jax.random.normal(ks[14], (DEPTH, SSD_WIDTH, D), f32)
    w_out = D ** -0.5 * jax.random.normal(ks[15], (DEPTH, D, D), f32)
    final_g = 1.0 + 0.02 * jax.random.normal(ks[16], (D,), f32)
    return {"x": x, "c": c, "positions": positions, "norm_g": norm_g, "w_ada": w_ada, "b_ada": b_ada,
            "w_in": w_in, "conv_w": conv_w, "conv_b": conv_b, "dt_bias": dt_bias, "a_log": a_log,
            "d_skip": d_skip, "ssd_norm_g": ssd_norm_g, "w_br_attn": w_br_attn, "w_br_ssd": w_br_ssd,
            "w_out": w_out, "final_g": final_g}


def reference(x, c, positions, norm_g, w_ada, b_ada, w_in, conv_w, conv_b, dt_bias, a_log, d_skip,
              ssd_norm_g, w_br_attn, w_br_ssd, w_out, final_g):
    b, s, D = x.shape
    dtype = x.dtype
    G, E, P, N = SSD_GROUPS, SSD_HEADS_PER_GROUP, SSD_HEADDIM, SSD_STATE
    split_at = np.cumsum(SPLIT_SIZES)[:-1].tolist()
    flip = lambda t: jnp.flip(t, axis=1)
    for i in range(DEPTH):
        ada = c @ w_ada[i] + b_ada[i]
        shift, scale, gate = jnp.split(ada, 3, axis=-1)
        h = rms_norm(x, norm_g[i]) * (1.0 + scale[:, None, :]) + shift[:, None, :]
        proj = h @ w_in[i]
        qkv, z_a, z_s, xbc, dt_raw, g_a, g_s = jnp.split(proj, split_at, axis=-1)

        qkv = qkv.reshape(b, s, 3, N_ATTN_GROUPS * ATTN_HEADS, HEAD_DIM)
        q = rotary(qkv[:, :, 0], positions).reshape(b, s, N_ATTN_GROUPS, ATTN_HEADS, HEAD_DIM)
        k = rotary(qkv[:, :, 1], positions).reshape(b, s, N_ATTN_GROUPS, ATTN_HEADS, HEAD_DIM)
        v = qkv[:, :, 2].reshape(b, s, N_ATTN_GROUPS, ATTN_HEADS, HEAD_DIM)
        outs, lses = [], []
        for gi, (win, dil) in enumerate(DILATION_GROUPS):
            o_g, l_g = dilated_window_attention(q[:, :, gi], k[:, :, gi], v[:, :, gi], win, dil)
            outs.append(o_g)
            lses.append(l_g)
        wts = jax.nn.softmax(jnp.stack(lses), axis=0)
        o = jnp.sum(wts[..., None] * jnp.stack(outs), axis=0).reshape(b, s, ATTN_WIDTH).astype(dtype)
        y_a = (o * jax.nn.silu(z_a)) @ w_br_attn[i]

        xbc = jax.nn.silu(centred_depthwise_conv(xbc, conv_w[i], conv_b[i]))
        xs, Bm, Cm = jnp.split(xbc.astype(jnp.float32), [SSD_WIDTH, SSD_WIDTH + G * N], axis=-1)
        xs = xs.reshape(b, s, G, E, P)
        Bm = Bm.reshape(b, s, G, N)
        Cm = Cm.reshape(b, s, G, N)
        dt = jax.nn.softplus(dt_raw.astype(jnp.float32).reshape(b, s, 2, SSD_HEADS) + dt_bias[i])
        dt = dt.reshape(b, s, 2, G, E)
        A = -jnp.exp(a_log[i].astype(jnp.float32)).reshape(2, G, E)
        y_f = ssd_chunked(xs, dt[:, :, 0], A[0], Bm, Cm)
        y_b = flip(ssd_chunked(flip(xs), flip(dt[:, :, 1]), A[1], flip(Bm), flip(Cm)))
        y = y_f + y_b + d_skip[i].astype(jnp.float32).reshape(G, E)[..., None] * xs
        y = y.reshape(b, s, SSD_WIDTH) * jax.nn.silu(z_s.astype(jnp.float32))
        y_s = rms_norm(y, ssd_norm_g[i]).astype(dtype) @ w_br_ssd[i]

        merged = jax.nn.sigmoid(g_a) * y_a + jax.nn.sigmoid(g_s) * y_s
        x = x + gate[:, None, :] * (merged @ w_out[i])
    return rms_norm(x, final_g)
```

```python
import functools

import jax
import jax.numpy as jnp
from jax import lax
from jax.experimental import pallas as pl
from jax.experimental.pallas import tpu as pltpu

F32 = jnp.float32
BF16 = jnp.bfloat16

D_MODEL = 2048
HEAD_DIM = 128
ATTN_HEADS = 12
DILATIONS = (1, 4, 16)
N_SIDE = 64
ATTN_WIDTH = ATTN_HEADS * HEAD_DIM
ROPE_THETA = 10000.0
SSD_WIDTH = 2 * D_MODEL
SSD_HEADDIM = 64
SSD_GROUPS = 8
SSD_HEADS = SSD_WIDTH // SSD_HEADDIM
SSD_HPG = SSD_HEADS // SSD_GROUPS
SSD_STATE = 128
SSD_CHUNK = 128
CONV_WIDTH = 5
CONV_CH = SSD_WIDTH + 2 * SSD_GROUPS * SSD_STATE
EPS = 1e-6
QKV_COLS = 3 * len(DILATIONS) * ATTN_WIDTH
COL_Z = QKV_COLS
COL_XBC = COL_Z + ATTN_WIDTH + SSD_WIDTH
COL_DT = COL_XBC + CONV_CH
COL_GATE = COL_DT + 2 * SSD_HEADS
IN_COLS = COL_GATE + 2 * D_MODEL

LANES = 128
PROJ_TN = 512
V7X_VMEM_LIMIT_BYTES = 56 * 1024 * 1024


def _params(*sem):
    return pltpu.CompilerParams(dimension_semantics=sem, vmem_limit_bytes=V7X_VMEM_LIMIT_BYTES)


def _dot(a, b):
    return jnp.dot(a, b, preferred_element_type=F32)


def _sigmoid(x):
    return 1.0 / (1.0 + jnp.exp(-x))


def _split3(v):
    hi = v.astype(BF16)
    r = v - hi.astype(F32)
    mid = r.astype(BF16)
    lo = (r - mid.astype(F32)).astype(BF16)
    return hi, mid, lo


def _exact_lmul(t01, v):
    hi, mid, lo = _split3(v)
    return _dot(t01, hi) + _dot(t01, mid) + _dot(t01, lo)


def _exact_rmul(v, e01):
    hi, mid, lo = _split3(v)
    return _dot(hi, e01) + _dot(mid, e01) + _dot(lo, e01)


def _ada_body(c_ref, w_ref, b_ref, o_ref):
    o_ref[...] = _dot(c_ref[...].astype(BF16), w_ref[...].astype(BF16)) + b_ref[...]


def _ada(c, w, bias):
    bsz, d = c.shape
    n = w.shape[1]
    tn = 768
    return pl.pallas_call(
        _ada_body,
        grid=(n // tn,),
        in_specs=[pl.BlockSpec((bsz, d), lambda j: (0, 0)),
                  pl.BlockSpec((d, tn), lambda j: (0, j)),
                  pl.BlockSpec((1, tn), lambda j: (0, j))],
        out_specs=pl.BlockSpec((bsz, tn), lambda j: (0, j)),
        out_shape=jax.ShapeDtypeStruct((bsz, n), F32),
        compiler_params=_params("arbitrary"),
        name="ada",
    )(c, w, bias)


def _rope_body(pos_ref, inv_ref, cos_ref, sin_ref):
    ang = pos_ref[0].astype(F32) * inv_ref[...]
    lane = lax.broadcasted_iota(jnp.int32, ang.shape, 1)
    sin = jnp.sin(ang)
    cos_ref[0] = jnp.cos(ang)
    sin_ref[0] = jnp.where(lane < HEAD_DIM // 2, -sin, sin)


def _rope_tables(pos3, inv2):
    n, s, _ = pos3.shape
    spec = pl.BlockSpec((1, s, HEAD_DIM), lambda i: (i, 0, 0))
    return pl.pallas_call(
        _rope_body,
        grid=(n,),
        in_specs=[pl.BlockSpec((1, s, 1), lambda i: (i, 0, 0)),
                  pl.BlockSpec((1, HEAD_DIM), lambda i: (0, 0))],
        out_specs=[spec, spec],
        out_shape=[jax.ShapeDtypeStruct((n, s, HEAD_DIM), F32)] * 2,
        compiler_params=_params("arbitrary"),
        name="rope_tables",
    )(pos3, inv2)


H_TILE = 512


def _h_body(x_ref, ada_ref, g_ref, h1_ref, h4_ref, h16_ref, hs_ref):
    x = x_ref[0]
    ms = jnp.mean(x * x, axis=-1, keepdims=True)
    xn = x * lax.rsqrt(ms + EPS) * g_ref[...]
    h = xn * (1.0 + ada_ref[0, 1:2, :]) + ada_ref[0, 0:1, :]
    h1_ref[0] = h.astype(BF16)
    for cb in range(h.shape[1] // LANES):
        cols = slice(cb * LANES, (cb + 1) * LANES)
        hs_ref[cb] = h[:, cols]
        for r in range(4):
            h4_ref[0, r, :, cols] = hs_ref[cb, pl.ds(r, H_TILE // 4, stride=4), :].astype(BF16)
        for r in range(16):
            h16_ref[0, r, :, cols] = hs_ref[cb, pl.ds(r, H_TILE // 16, stride=16), :].astype(BF16)


def _modulated_norm(x, ada3, g):
    bsz, s, d = x.shape
    t = H_TILE
    outs = pl.pallas_call(
        _h_body,
        grid=(bsz, s // t),
        in_specs=[pl.BlockSpec((1, t, d), lambda b, i: (b, i, 0)),
                  pl.BlockSpec((1, 3, d), lambda b, i: (b, 0, 0)),
                  pl.BlockSpec((1, d), lambda b, i: (0, 0))],
        out_specs=[pl.BlockSpec((1, t, d), lambda b, i: (b, i, 0)),
                   pl.BlockSpec((1, 4, t // 4, d), lambda b, i: (b, 0, i, 0)),
                   pl.BlockSpec((1, 16, t // 16, d), lambda b, i: (b, 0, i, 0))],
        out_shape=[jax.ShapeDtypeStruct((bsz, s, d), BF16),
                   jax.ShapeDtypeStruct((bsz, 4, s // 4, d), BF16),
                   jax.ShapeDtypeStruct((bsz, 16, s // 16, d), BF16)],
        scratch_shapes=[pltpu.VMEM((d // LANES, t, LANES), F32)],
        compiler_params=_params("arbitrary", "arbitrary"),
        name="modulated_norm",
    )(x, ada3, g)
    h1, h4, h16 = outs
    return h1, h4.reshape(bsz, s, d), h16.reshape(bsz, s, d)


def _proj_matmul(h_ref, w_ref, wbf_ref):
    @pl.when(pl.program_id(1) == 0)
    def _():
        wbf_ref[...] = w_ref[...].astype(BF16)

    return _dot(h_ref[0], wbf_ref[...])


def _proj_rope_body(h_ref, w_ref, cos_ref, sin_ref, o_ref, wbf_ref):
    acc = _proj_matmul(h_ref, w_ref, wbf_ref)
    j = pl.program_id(0)
    n_rot = 2 * ATTN_WIDTH // PROJ_TN

    @pl.when(j < n_rot)
    def _():
        cos = cos_ref[0]
        sin = sin_ref[0]
        for hh in range(PROJ_TN // HEAD_DIM):
            t = acc[:, hh * HEAD_DIM:(hh + 1) * HEAD_DIM]
            rot = t * cos + pltpu.roll(t, HEAD_DIM // 2, 1) * sin
            o_ref[0, :, hh * HEAD_DIM:(hh + 1) * HEAD_DIM] = rot.astype(BF16)

    @pl.when(j >= n_rot)
    def _():
        o_ref[0] = acc.astype(BF16)


def _proj_silu_body(h_ref, w_ref, o_ref, wbf_ref):
    acc = _proj_matmul(h_ref, w_ref, wbf_ref)
    o_ref[0] = (acc * _sigmoid(acc)).astype(BF16)


def _proj_sigmoid_body(h_ref, w_ref, o_ref, wbf_ref):
    acc = _proj_matmul(h_ref, w_ref, wbf_ref)
    o_ref[0] = _sigmoid(acc).astype(BF16)


CONV_PAD = 8
CONV_ROWS = 256


def _proj_conv_body(h_ref, w_ref, cw_ref, cb_ref, o_ref, wbf_ref, pad_ref):
    s = h_ref.shape[1]
    acc = _proj_matmul(h_ref, w_ref, wbf_ref)
    zeros = jnp.zeros((CONV_PAD, PROJ_TN), F32)
    pad_ref[0:CONV_PAD, :] = zeros
    pad_ref[s + CONV_PAD:s + 2 * CONV_PAD, :] = zeros
    pad_ref[CONV_PAD:s + CONV_PAD, :] = acc
    half = (CONV_WIDTH - 1) // 2
    for m in range(s // CONV_ROWS):
        base = CONV_PAD + m * CONV_ROWS - half
        out = cb_ref[...] + cw_ref[0:1, :] * pad_ref[base:base + CONV_ROWS, :]
        for k in range(1, CONV_WIDTH):
            out = out + cw_ref[k:k + 1, :] * pad_ref[base + k:base + k + CONV_ROWS, :]
        o_ref[0, m * CONV_ROWS:(m + 1) * CONV_ROWS, :] = (out * _sigmoid(out)).astype(BF16)


def _proj_dt_body(h_ref, w_ref, bias_ref, o_ref, wbf_ref):
    acc = _proj_matmul(h_ref, w_ref, wbf_ref) + bias_ref[...]
    sp = jnp.maximum(acc, 0.0) + jnp.log1p(jnp.exp(-jnp.abs(acc)))
    n = 2 * SSD_HEADS
    o_ref[0, :, 0:n] = sp
    for g in range(1, SSD_GROUPS):
        o_ref[0, :, g * n:(g + 1) * n] = pltpu.roll(sp, n - g * SSD_HPG, 1)


def _proj_call(body, h, w_in, col0, n_tiles, tn, out_cols, out_dtype, extra=(), extra_specs=(),
               scratch=(), out_tn=None, name="proj"):
    bsz, s, d = h.shape
    out_tn = tn if out_tn is None else out_tn
    blk0 = col0 // tn
    assert blk0 * tn == col0
    return pl.pallas_call(
        body,
        grid=(n_tiles, bsz),
        in_specs=[pl.BlockSpec((1, s, d), lambda j, b: (b, 0, 0)),
                  pl.BlockSpec((d, tn), lambda j, b: (0, blk0 + j)),
                  *extra_specs],
        out_specs=pl.BlockSpec((1, s, out_tn), lambda j, b: (b, 0, j)),
        out_shape=jax.ShapeDtypeStruct((bsz, s, out_cols), out_dtype),
        scratch_shapes=[pltpu.VMEM((d, tn), BF16), *scratch],
        compiler_params=_params("arbitrary", "arbitrary"),
        name=name,
    )(h, w_in, *extra)


def _proj_qkv(h, w_in, cos, sin, group):
    bsz, s, d = h.shape
    tn = PROJ_TN
    tps = ATTN_WIDTH // tn
    n_groups = len(DILATIONS)

    def wcol(j, b):
        return (0, (j // tps) * (n_groups * tps) + group * tps + j % tps)

    tab = pl.BlockSpec((1, s, HEAD_DIM), lambda j, b: (group * bsz + b, 0, 0))
    return pl.pallas_call(
        _proj_rope_body,
        grid=(3 * tps, bsz),
        in_specs=[pl.BlockSpec((1, s, d), lambda j, b: (b, 0, 0)),
                  pl.BlockSpec((d, tn), wcol), tab, tab],
        out_specs=pl.BlockSpec((1, s, tn), lambda j, b: (b, 0, j)),
        out_shape=jax.ShapeDtypeStruct((bsz, s, 3 * ATTN_WIDTH), BF16),
        scratch_shapes=[pltpu.VMEM((d, tn), BF16)],
        compiler_params=_params("arbitrary", "arbitrary"),
        name=f"proj_qkv{group}",
    )(h, w_in, cos, sin)


ATT_BLK = 128


def _attn_body(q1, k1, v1, q2, k2, v2, q3, k3, v3, z_ref, o_ref, on_ref, ls_ref):
    s = o_ref.shape[1]
    scale = HEAD_DIM ** -0.5
    qi = lax.broadcasted_iota(jnp.int32, (ATT_BLK, 2 * ATT_BLK), 0)
    ki = lax.broadcasted_iota(jnp.int32, (ATT_BLK, 2 * ATT_BLK), 1)
    rel = qi - ki

    def block(q_ref, k_ref, v_ref, q0, k0, off, nk):
        q = q_ref[0, pl.ds(q0, ATT_BLK), :]
        kw = k_ref[0, pl.ds(k0, nk), :]
        vw = v_ref[0, pl.ds(k0, nk), :]
        sc = lax.dot_general(q, kw, (((1,), (1,)), ((), ())), preferred_element_type=F32) * scale
        dist = rel[:, :nk] + off
        sc = jnp.where(jnp.abs(dist) <= N_SIDE, sc, -jnp.inf)
        m = jnp.max(sc, axis=-1, keepdims=True)
        p = jnp.exp(sc - m)
        den = jnp.sum(p, axis=-1, keepdims=True)
        o = _dot(p.astype(BF16), vw) / den
        lse = m + jnp.log(den)
        return o, jnp.broadcast_to(lse, (ATT_BLK, HEAD_DIM))

    groups = ((q1, k1, v1), (q2, k2, v2), (q3, k3, v3))
    for g, (q_ref, k_ref, v_ref) in enumerate(groups):
        dil = DILATIONS[g]
        sub = s // dil
        nblk = sub // ATT_BLK
        for r in range(dil):
            if nblk == 1:
                o, lse = block(q_ref, k_ref, v_ref, r * sub, r * sub, 0, ATT_BLK)
                on_ref[g, pl.ds(r, ATT_BLK, stride=dil), :] = o
                ls_ref[g, pl.ds(r, ATT_BLK, stride=dil), :] = lse
            else:
                def step(i, carry, q_ref=q_ref, k_ref=k_ref, v_ref=v_ref, r=r, sub=sub, dil=dil):
                    loc = i * ATT_BLK
                    kloc = jnp.clip(loc - N_SIDE, 0, sub - 2 * ATT_BLK)
                    q0 = pl.multiple_of(r * sub + loc, ATT_BLK)
                    k0 = pl.multiple_of(r * sub + kloc, N_SIDE)
                    o, lse = block(q_ref, k_ref, v_ref, q0, k0, loc - kloc, 2 * ATT_BLK)
                    if dil == 1:
                        rows = pl.ds(q0, ATT_BLK)
                    else:
                        rows = pl.ds(loc * dil + r, ATT_BLK, stride=dil)
                    on_ref[g, rows, :] = o
                    ls_ref[g, rows, :] = lse
                    return carry

                lax.fori_loop(0, nblk, step, 0)

    l0, l1, l2 = ls_ref[0], ls_ref[1], ls_ref[2]
    mx = jnp.maximum(jnp.maximum(l0, l1), l2)
    e0, e1, e2 = jnp.exp(l0 - mx), jnp.exp(l1 - mx), jnp.exp(l2 - mx)
    o = (e0 * on_ref[0] + e1 * on_ref[1] + e2 * on_ref[2]) / (e0 + e1 + e2)
    o_ref[0] = (o * z_ref[0].astype(F32)).astype(BF16)


def _attention(qkv, zs):
    bsz, s, _ = qkv[0].shape
    specs = []
    args = []
    for g in range(len(DILATIONS)):
        for sec in range(3):
            specs.append(pl.BlockSpec((1, s, HEAD_DIM),
                                      lambda b, h, sec=sec: (b, 0, sec * ATTN_HEADS + h)))
            args.append(qkv[g])
    specs.append(pl.BlockSpec((1, s, HEAD_DIM), lambda b, h: (b, 0, h)))
    args.append(zs)
    return pl.pallas_call(
        _attn_body,
        grid=(bsz, ATTN_HEADS),
        in_specs=specs,
        out_specs=pl.BlockSpec((1, s, HEAD_DIM), lambda b, h: (b, 0, h)),
        out_shape=jax.ShapeDtypeStruct((bsz, s, ATTN_WIDTH), BF16),
        scratch_shapes=[pltpu.VMEM((3, s, HEAD_DIM), F32), pltpu.VMEM((3, s, HEAD_DIM), F32)],
        compiler_params=_params("arbitrary", "arbitrary"),
        name="dilated_attention",
    )(*args)


def _ssd_body(x_ref, b_ref, c_ref, dt_ref, alog_ref, dskip_ref, z_ref, y_ref, prevb_ref, sf_ref, sb_ref):
    q = SSD_CHUNK
    s = x_ref.shape[1]
    nc = s // q
    gw = SSD_HPG * SSD_HEADDIM
    bwd = SSD_HEADS

    a_row = -jnp.exp(alog_ref[0])
    row = lax.broadcasted_iota(jnp.int32, (q, q), 0)
    col = lax.broadcasted_iota(jnp.int32, (q, q), 1)
    tri = (col <= row).astype(BF16)
    triu = (col >= row).astype(BF16)
    ec = lax.broadcasted_iota(jnp.int32, (LANES, gw), 0)
    eh = lax.shift_right_logical(lax.broadcasted_iota(jnp.int32, (LANES, gw), 1),
                                 SSD_HEADDIM.bit_length() - 1)
    e_f = (ec == eh).astype(BF16)
    e_b = (ec == eh + bwd).astype(BF16)
    lane = lax.broadcasted_iota(jnp.int32, (q, LANES), 1)
    tn_dims = (((0,), (0,)), ((), ()))

    def expand_row(v, e01):
        return _exact_rmul(jnp.broadcast_to(v, (8, LANES)), e01)[0:1]

    sb_ref[...] = jnp.zeros_like(sb_ref)

    def bwd_step(cc, carry):
        c = nc - 1 - cc
        rows = pl.ds(pl.multiple_of(c * q, q), q)
        dt = dt_ref[0, rows, :]
        rb = _exact_lmul(triu, dt * a_row)
        rb0 = rb[0:1, :]
        wexp = _exact_rmul(dt * jnp.exp(rb0 - rb), e_b)
        xw = (x_ref[0, rows, :].astype(F32) * wexp).astype(BF16)
        st = lax.dot_general(b_ref[0, rows, :], xw, tn_dims, preferred_element_type=F32)
        prevb_ref[c] = sb_ref[...]
        sb_ref[...] = sb_ref[...] * expand_row(jnp.exp(rb0), e_b) + st
        return carry

    lax.fori_loop(0, nc, bwd_step, 0)

    sf_ref[...] = jnp.zeros_like(sf_ref)

    def fwd_step(c, carry):
        rows = pl.ds(pl.multiple_of(c * q, q), q)
        dt = dt_ref[0, rows, :]
        a = dt * a_row
        cf = _exact_lmul(tri, a)
        rb = _exact_lmul(triu, a)
        cfl = cf[q - 1:q, :]
        bc = b_ref[0, rows, :]
        cc = c_ref[0, rows, :]
        xb = x_ref[0, rows, :]
        x = xb.astype(F32)
        cb = lax.dot_general(cc, bc, (((1,), (1,)), ((), ())), preferred_element_type=F32)
        cft, rbt, dtt = cf.T, rb.T, dt.T
        ms = []
        for e in range(SSD_HPG):
            dl = cf[:, e:e + 1] - cft[e:e + 1, :]
            wf = jnp.exp(jnp.where(row >= col, dl, -jnp.inf)) * dtt[e:e + 1, :]
            db = rb[:, bwd + e:bwd + e + 1] - rbt[bwd + e:bwd + e + 1, :]
            wb = jnp.exp(jnp.where(row <= col, db, -jnp.inf)) * dtt[bwd + e:bwd + e + 1, :]
            ms.append((cb * (wf + wb)).astype(BF16))
        ys = []
        for p in range(SSD_HPG // 2):
            lhs = jnp.concatenate([ms[2 * p], ms[2 * p + 1]], axis=1)
            xp = xb[:, p * LANES:(p + 1) * LANES]
            zero = jnp.zeros_like(xp)
            rhs = jnp.concatenate([jnp.where(lane < SSD_HEADDIM, xp, zero),
                                   jnp.where(lane >= SSD_HEADDIM, xp, zero)], axis=0)
            ys.append(_dot(lhs, rhs))
        y = jnp.concatenate(ys, axis=1)
        y = y + _dot(cc, sf_ref[...].astype(BF16)) * _exact_rmul(jnp.exp(cf), e_f)
        y = y + _dot(cc, prevb_ref[c].astype(BF16)) * _exact_rmul(jnp.exp(rb), e_b)
        wexp = _exact_rmul(dt * jnp.exp(cfl - cf), e_f)
        st = lax.dot_general(bc, (x * wexp).astype(BF16), tn_dims, preferred_element_type=F32)
        sf_ref[...] = sf_ref[...] * expand_row(jnp.exp(cfl), e_f) + st
        y = (y + dskip_ref[...] * x) * z_ref[0, rows, :].astype(F32)
        y_ref[0, rows, :] = y.astype(BF16)
        return carry

    lax.fori_loop(0, nc, fwd_step, 0)


def _ssd(xbc, dtx, alog_g, dskip_x, zs):
    bsz, s, _ = xbc.shape
    gw = SSD_HPG * SSD_HEADDIM
    nb = SSD_WIDTH // SSD_STATE
    zoff = ATTN_WIDTH // gw
    return pl.pallas_call(
        _ssd_body,
        grid=(bsz, SSD_GROUPS),
        in_specs=[pl.BlockSpec((1, s, gw), lambda b, g: (b, 0, g)),
                  pl.BlockSpec((1, s, SSD_STATE), lambda b, g: (b, 0, nb + g)),
                  pl.BlockSpec((1, s, SSD_STATE), lambda b, g: (b, 0, nb + SSD_GROUPS + g)),
                  pl.BlockSpec((1, s, LANES), lambda b, g: (b, 0, g)),
                  pl.BlockSpec((1, 1, LANES), lambda b, g: (g, 0, 0)),
                  pl.BlockSpec((1, gw), lambda b, g: (0, g)),
                  pl.BlockSpec((1, s, gw), lambda b, g: (b, 0, zoff + g))],
        out_specs=pl.BlockSpec((1, s, gw), lambda b, g: (b, 0, g)),
        out_shape=jax.ShapeDtypeStruct((bsz, s, SSD_WIDTH), BF16),
        scratch_shapes=[pltpu.VMEM((s // SSD_CHUNK, SSD_STATE, gw), F32),
                        pltpu.VMEM((SSD_STATE, gw), F32),
                        pltpu.VMEM((SSD_STATE, gw), F32)],
        compiler_params=_params("arbitrary", "arbitrary"),
        name="ssd",
    )(xbc, xbc, xbc, dtx, alog_g, dskip_x, zs)


TAIL_TM = 512
TAIL_A_TM = 256


def _tail_a_body(ya_ref, y_ref, g_ref, ng_ref, wa_ref, ws_ref, o_ref):
    d = o_ref.shape[1]
    ya = _dot(ya_ref[...], wa_ref[...])
    y = y_ref[...].astype(F32)
    yn = y * lax.rsqrt(jnp.mean(y * y, axis=-1, keepdims=True) + EPS) * ng_ref[...]
    ys = _dot(yn.astype(BF16), ws_ref[...])
    off = 2 * SSD_HEADS
    ga = g_ref[:, off:off + d].astype(F32)
    gs = g_ref[:, off + d:off + 2 * d].astype(F32)
    o_ref[...] = (ga * ya + gs * ys).astype(BF16)


def _tail_a(ya_in, y, gates, norm_g, wa, ws):
    n, d = ya_in.shape[0], wa.shape[1]
    tm = TAIL_A_TM
    resident = functools.partial(pl.BlockSpec, pipeline_mode=pl.Buffered(1))
    return pl.pallas_call(
        _tail_a_body,
        grid=(n // tm,),
        in_specs=[pl.BlockSpec((tm, ya_in.shape[1]), lambda i: (i, 0)),
                  pl.BlockSpec((tm, y.shape[1]), lambda i: (i, 0)),
                  pl.BlockSpec((tm, gates.shape[1]), lambda i: (i, 0)),
                  pl.BlockSpec((1, y.shape[1]), lambda i: (0, 0)),
                  resident(wa.shape, lambda i: (0, 0)),
                  resident(ws.shape, lambda i: (0, 0))],
        out_specs=pl.BlockSpec((tm, d), lambda i: (i, 0)),
        out_shape=jax.ShapeDtypeStruct((n, d), BF16),
        compiler_params=_params("arbitrary"),
        name="tail_a",
    )(ya_in, y, gates, norm_g, wa, ws)


def _tail_b_body(m_ref, x_ref, ada_ref, w_ref, fg_ref, o_ref):
    t = _dot(m_ref[...], w_ref[...])
    xn = x_ref[...] + ada_ref[0, 2:3, :] * t
    o_ref[...] = xn * lax.rsqrt(jnp.mean(xn * xn, axis=-1, keepdims=True) + EPS) * fg_ref[...]


def _tail_b(merged, x2, ada3, w_out, final_g, seq):
    n, d = x2.shape
    tm = TAIL_TM
    per = seq // tm
    resident = functools.partial(pl.BlockSpec, pipeline_mode=pl.Buffered(1))
    return pl.pallas_call(
        _tail_b_body,
        grid=(n // tm,),
        in_specs=[pl.BlockSpec((tm, d), lambda i: (i, 0)),
                  pl.BlockSpec((tm, d), lambda i: (i, 0)),
                  pl.BlockSpec((1, 3, d), lambda i: (i // per, 0, 0)),
                  resident(w_out.shape, lambda i: (0, 0)),
                  pl.BlockSpec((1, d), lambda i: (0, 0))],
        out_specs=pl.BlockSpec((tm, d), lambda i: (i, 0)),
        out_shape=jax.ShapeDtypeStruct((n, d), F32),
        compiler_params=_params("arbitrary"),
        name="tail_b",
    )(merged, x2, ada3, w_out, final_g)


def _layer(x, ada3, pos_tabs, norm_g, w_in, conv_w, conv_b, dt_bias, a_log, d_skip, ssd_norm_g,
           w_br_attn, w_br_ssd, w_out, out_g):
    bsz, s, d = x.shape
    cos, sin = pos_tabs
    hs = _modulated_norm(x, ada3, norm_g[None])
    qkv = [_proj_qkv(hs[g], w_in, cos, sin, g) for g in range(len(DILATIONS))]
    h = hs[0]
    tn = PROJ_TN
    zs = _proj_call(_proj_silu_body, h, w_in, COL_Z, (ATTN_WIDTH + SSD_WIDTH) // tn, tn,
                    ATTN_WIDTH + SSD_WIDTH, BF16, name="proj_z")
    xbc = _proj_call(
        _proj_conv_body, h, w_in, COL_XBC, CONV_CH // tn, tn, CONV_CH, BF16,
        extra=(conv_w, conv_b[None]),
        extra_specs=(pl.BlockSpec((CONV_WIDTH, tn), lambda j, b: (0, j)),
                     pl.BlockSpec((1, tn), lambda j, b: (0, j))),
        scratch=(pltpu.VMEM((s + 2 * CONV_PAD, tn), F32),), name="proj_xbc")
    n_dt = 2 * SSD_HEADS
    dtx = _proj_call(
        _proj_dt_body, h, w_in, COL_DT, 1, n_dt, SSD_GROUPS * n_dt, F32,
        extra=(dt_bias.reshape(1, n_dt),),
        extra_specs=(pl.BlockSpec((1, n_dt), lambda j, b: (0, 0)),),
        out_tn=SSD_GROUPS * n_dt, name="proj_dt")
    n_gate_tiles = -(-(IN_COLS - COL_DT) // tn)
    gates = _proj_call(_proj_sigmoid_body, h, w_in, COL_DT, n_gate_tiles, tn, n_gate_tiles * tn,
                       BF16, name="proj_gates")

    ya_in = _attention(qkv, zs)

    a_flat = a_log.reshape(1, n_dt)
    alog_g = jnp.stack([jnp.roll(a_flat, -g * SSD_HPG, axis=1) for g in range(SSD_GROUPS)])
    dskip_x = jnp.repeat(d_skip, SSD_HEADDIM)[None]
    y = _ssd(xbc, dtx, alog_g, dskip_x, zs)

    n = bsz * s
    merged = _tail_a(ya_in.reshape(n, ATTN_WIDTH), y.reshape(n, SSD_WIDTH),
                     gates.reshape(n, gates.shape[-1]), ssd_norm_g[None],
                     w_br_attn.astype(BF16), w_br_ssd.astype(BF16))
    out = _tail_b(merged, x.reshape(n, d), ada3, w_out.astype(BF16), out_g, s)
    return out.reshape(bsz, s, d)


def kernel(x, c, positions, norm_g, w_ada, b_ada, w_in, conv_w, conv_b, dt_bias, a_log, d_skip,
           ssd_norm_g, w_br_attn, w_br_ssd, w_out, final_g):
    bsz, s, d = x.shape
    depth = w_in.shape[0]
    inv = ROPE_THETA ** (-jnp.arange(0, HEAD_DIM, 2, dtype=F32) / HEAD_DIM)
    inv2 = jnp.concatenate([inv, inv])[None]
    pos_orders = [positions.reshape(bsz, s // dil, dil).transpose(0, 2, 1).reshape(bsz, s)
                  for dil in DILATIONS]
    pos3 = jnp.concatenate(pos_orders, axis=0)[..., None]
    pos_tabs = _rope_tables(pos3, inv2)
    for i in range(depth):
        ada3 = _ada(c, w_ada[i], b_ada[i][None]).reshape(bsz, 3, d)
        assert depth == 1
        x = _layer(x, ada3, pos_tabs, norm_g[i], w_in[i], conv_w[i], conv_b[i], dt_bias[i], a_log[i],
                   d_skip[i], ssd_norm_g[i], w_br_attn[i], w_br_ssd[i], w_out[i], final_g[None])
    return x
```

```python
import functools

import numpy as np
import jax
import jax.numpy as jnp
from jax import lax
from jax.experimental import pallas as pl
from jax.experimental.pallas import tpu as pltpu

F32 = jnp.float32
BF16 = jnp.bfloat16

D_MODEL = 2048
HEAD_DIM = 128
ATTN_HEADS = 12
DILATIONS = (1, 4, 16)
N_SIDE = 64
ATTN_WIDTH = ATTN_HEADS * HEAD_DIM
ROPE_THETA = 10000.0
SSD_WIDTH = 2 * D_MODEL
SSD_HEADDIM = 64
SSD_GROUPS = 8
SSD_HEADS = SSD_WIDTH // SSD_HEADDIM
SSD_HPG = SSD_HEADS // SSD_GROUPS
SSD_STATE = 128
SSD_CHUNK = 128
CONV_WIDTH = 5
CONV_CH = SSD_WIDTH + 2 * SSD_GROUPS * SSD_STATE
EPS = 1e-6
QKV_COLS = 3 * len(DILATIONS) * ATTN_WIDTH
COL_Z = QKV_COLS
COL_XBC = COL_Z + ATTN_WIDTH + SSD_WIDTH
COL_DT = COL_XBC + CONV_CH
COL_GATE = COL_DT + 2 * SSD_HEADS
IN_COLS = COL_GATE + 2 * D_MODEL

LANES = 128
PROJ_TN = 512
V7X_VMEM_LIMIT_BYTES = 56 * 1024 * 1024


def _params(*sem):
    return pltpu.CompilerParams(dimension_semantics=sem, vmem_limit_bytes=V7X_VMEM_LIMIT_BYTES)


def _dot(a, b):
    return jnp.dot(a, b, preferred_element_type=F32)


def _sigmoid(x):
    return 1.0 / (1.0 + jnp.exp(-x))


def _split3(v):
    hi = v.astype(BF16)
    r = v - hi.astype(F32)
    mid = r.astype(BF16)
    lo = (r - mid.astype(F32)).astype(BF16)
    return hi, mid, lo


def _exact_lmul(t01, v):
    hi, mid, lo = _split3(v)
    return _dot(t01, hi) + _dot(t01, mid) + _dot(t01, lo)


def _exact_rmul(v, e01):
    hi, mid, lo = _split3(v)
    return _dot(hi, e01) + _dot(mid, e01) + _dot(lo, e01)


def _ada_body(c_ref, w_ref, b_ref, o_ref):
    o_ref[...] = _dot(c_ref[...].astype(BF16), w_ref[...].astype(BF16)) + b_ref[...]


def _ada(c, w, bias):
    bsz, d = c.shape
    n = w.shape[1]
    tn = 768
    return pl.pallas_call(
        _ada_body,
        grid=(n // tn,),
        in_specs=[pl.BlockSpec((bsz, d), lambda j: (0, 0)),
                  pl.BlockSpec((d, tn), lambda j: (0, j)),
                  pl.BlockSpec((1, tn), lambda j: (0, j))],
        out_specs=pl.BlockSpec((bsz, tn), lambda j: (0, j)),
        out_shape=jax.ShapeDtypeStruct((bsz, n), F32),
        compiler_params=_params("arbitrary"),
        name="ada",
    )(c, w, bias)


def _rope_body(pos_ref, inv_ref, cos_ref, sin_ref):
    ang = pos_ref[0].astype(F32) * inv_ref[...]
    lane = lax.broadcasted_iota(jnp.int32, ang.shape, 1)
    sin = jnp.sin(ang)
    cos_ref[0] = jnp.cos(ang)
    sin_ref[0] = jnp.where(lane < HEAD_DIM // 2, -sin, sin)


def _rope_tables(pos3, inv2):
    n, s, _ = pos3.shape
    spec = pl.BlockSpec((1, s, HEAD_DIM), lambda i: (i, 0, 0))
    return pl.pallas_call(
        _rope_body,
        grid=(n,),
        in_specs=[pl.BlockSpec((1, s, 1), lambda i: (i, 0, 0)),
                  pl.BlockSpec((1, HEAD_DIM), lambda i: (0, 0))],
        out_specs=[spec, spec],
        out_shape=[jax.ShapeDtypeStruct((n, s, HEAD_DIM), F32)] * 2,
        compiler_params=_params("arbitrary"),
        name="rope_tables",
    )(pos3, inv2)


H_TILE = 512


def _h_body(x_ref, ada_ref, g_ref, h1_ref, h4_ref, h16_ref, hs_ref):
    x = x_ref[0]
    ms = jnp.mean(x * x, axis=-1, keepdims=True)
    xn = x * lax.rsqrt(ms + EPS) * g_ref[...]
    h = xn * (1.0 + ada_ref[0, 1:2, :]) + ada_ref[0, 0:1, :]
    h1_ref[0] = h.astype(BF16)
    for cb in range(h.shape[1] // LANES):
        cols = slice(cb * LANES, (cb + 1) * LANES)
        hs_ref[cb] = h[:, cols]
        for r in range(4):
            h4_ref[0, r, :, cols] = hs_ref[cb, pl.ds(r, H_TILE // 4, stride=4), :].astype(BF16)
        for r in range(16):
            h16_ref[0, r, :, cols] = hs_ref[cb, pl.ds(r, H_TILE // 16, stride=16), :].astype(BF16)


def _modulated_norm(x, ada3, g):
    bsz, s, d = x.shape
    t = H_TILE
    outs = pl.pallas_call(
        _h_body,
        grid=(bsz, s // t),
        in_specs=[pl.BlockSpec((1, t, d), lambda b, i: (b, i, 0)),
                  pl.BlockSpec((1, 3, d), lambda b, i: (b, 0, 0)),
                  pl.BlockSpec((1, d), lambda b, i: (0, 0))],
        out_specs=[pl.BlockSpec((1, t, d), lambda b, i: (b, i, 0)),
                   pl.BlockSpec((1, 4, t // 4, d), lambda b, i: (b, 0, i, 0)),
                   pl.BlockSpec((1, 16, t // 16, d), lambda b, i: (b, 0, i, 0))],
        out_shape=[jax.ShapeDtypeStruct((bsz, s, d), BF16),
                   jax.ShapeDtypeStruct((bsz, 4, s // 4, d), BF16),
                   jax.ShapeDtypeStruct((bsz, 16, s // 16, d), BF16)],
        scratch_shapes=[pltpu.VMEM((d // LANES, t, LANES), F32)],
        compiler_params=_params("arbitrary", "arbitrary"),
        name="modulated_norm",
    )(x, ada3, g)
    h1, h4, h16 = outs
    return h1, h4.reshape(bsz, s, d), h16.reshape(bsz, s, d)


PROJ_ROWS = 512


def _cast_weight_tile(w_ref, wbf_ref):
    @pl.when(pl.program_id(1) == 0)
    def _():
        wbf_ref[...] = w_ref[...].astype(BF16)


def _proj_chunks(h_ref, wbf_ref, epilogue):
    for m in range(h_ref.shape[1] // PROJ_ROWS):
        rows = slice(m * PROJ_ROWS, (m + 1) * PROJ_ROWS)
        epilogue(rows, _dot(h_ref[0, rows, :], wbf_ref[...]))


def _proj_rope_body(h_ref, w_ref, cos_ref, sin_ref, o_ref, wbf_ref):
    _cast_weight_tile(w_ref, wbf_ref)
    j = pl.program_id(0)
    n_rot = 2 * ATTN_WIDTH // PROJ_TN

    def rotate(rows, acc):
        cos = cos_ref[0, rows, :]
        sin = sin_ref[0, rows, :]
        for hh in range(PROJ_TN // HEAD_DIM):
            t = acc[:, hh * HEAD_DIM:(hh + 1) * HEAD_DIM]
            rot = t * cos + pltpu.roll(t, HEAD_DIM // 2, 1) * sin
            o_ref[0, rows, hh * HEAD_DIM:(hh + 1) * HEAD_DIM] = rot.astype(BF16)

    def plain(rows, acc):
        o_ref[0, rows, :] = acc.astype(BF16)

    @pl.when(j < n_rot)
    def _():
        _proj_chunks(h_ref, wbf_ref, rotate)

    @pl.when(j >= n_rot)
    def _():
        _proj_chunks(h_ref, wbf_ref, plain)


def _proj_silu_body(h_ref, w_ref, o_ref, wbf_ref):
    _cast_weight_tile(w_ref, wbf_ref)

    def epilogue(rows, acc):
        o_ref[0, rows, :] = (acc * _sigmoid(acc)).astype(BF16)

    _proj_chunks(h_ref, wbf_ref, epilogue)


def _proj_sigmoid_body(h_ref, w_ref, o_ref, wbf_ref):
    _cast_weight_tile(w_ref, wbf_ref)

    def epilogue(rows, acc):
        o_ref[0, rows, :] = _sigmoid(acc).astype(BF16)

    _proj_chunks(h_ref, wbf_ref, epilogue)


CONV_PAD = 8
CONV_ROWS = 256


def _proj_conv_body(h_ref, w_ref, cw_ref, cb_ref, o_ref, wbf_ref, pad_ref):
    s = h_ref.shape[1]
    _cast_weight_tile(w_ref, wbf_ref)
    zeros = jnp.zeros((CONV_PAD, PROJ_TN), F32)
    pad_ref[0:CONV_PAD, :] = zeros
    pad_ref[s + CONV_PAD:s + 2 * CONV_PAD, :] = zeros
    half = (CONV_WIDTH - 1) // 2

    def conv_rows(m):
        base = CONV_PAD + m * CONV_ROWS - half
        out = cb_ref[...] + cw_ref[0:1, :] * pad_ref[base:base + CONV_ROWS, :]
        for k in range(1, CONV_WIDTH):
            out = out + cw_ref[k:k + 1, :] * pad_ref[base + k:base + k + CONV_ROWS, :]
        o_ref[0, m * CONV_ROWS:(m + 1) * CONV_ROWS, :] = (out * _sigmoid(out)).astype(BF16)

    per = PROJ_ROWS // CONV_ROWS

    def epilogue(rows, acc):
        pad_ref[CONV_PAD + rows.start:CONV_PAD + rows.stop, :] = acc
        done = rows.start // PROJ_ROWS
        for m in range((done - 1) * per, done * per):
            if m >= 0:
                conv_rows(m)

    _proj_chunks(h_ref, wbf_ref, epilogue)
    last = s // PROJ_ROWS - 1
    for m in range(last * per, (last + 1) * per):
        conv_rows(m)


def _rep_masks(shape):
    lane = lax.broadcasted_iota(jnp.int32, shape, len(shape) - 1) & (SSD_HEADS - 1)
    return lane < SSD_HPG, lane < 2 * SSD_HPG


def _packed_split(v):
    hi, mid, lo = _split3(v)
    m0, m1 = _rep_masks(v.shape)
    return jnp.where(m0, hi, jnp.where(m1, mid, lo))


def _replicate_heads(rolled, g, m0, m1):
    return jnp.where(m0, rolled[g], jnp.where(m1, rolled[g - 1], rolled[g - 2]))


def _proj_dt_body(h_ref, w_ref, bias_ref, o_ref, ot_ref, wbf_ref):
    s = h_ref.shape[1]
    _cast_weight_tile(w_ref, wbf_ref)
    acc = _dot(h_ref[0], wbf_ref[...]) + bias_ref[...]
    sp = jnp.maximum(acc, 0.0) + jnp.log1p(jnp.exp(-jnp.abs(acc)))
    n = 2 * SSD_HEADS
    m0, m1 = _rep_masks(sp.shape)
    rolled = {k: (sp if k == 0 else pltpu.roll(sp, (-k * SSD_HPG) % n, 1))
              for k in range(-2, SSD_GROUPS)}
    for g in range(SSD_GROUPS):
        o_ref[0, :, g * n:(g + 1) * n] = _replicate_heads(rolled, g, m0, m1)
    for c in range(s // SSD_CHUNK):
        t = sp[c * SSD_CHUNK:(c + 1) * SSD_CHUNK, :].T
        for g in range(SSD_GROUPS):
            for d in range(2):
                r0 = d * SSD_HEADS + g * SSD_HPG
                ot_ref[0, g, d, c * SSD_HPG:(c + 1) * SSD_HPG, :] = t[r0:r0 + SSD_HPG, :]


def _proj_dt(h, w_in, dt_bias):
    bsz, s, d = h.shape
    n = 2 * SSD_HEADS
    nc = s // SSD_CHUNK
    blk0 = COL_DT // n
    return pl.pallas_call(
        _proj_dt_body,
        grid=(1, bsz),
        in_specs=[pl.BlockSpec((1, s, d), lambda j, b: (b, 0, 0)),
                  pl.BlockSpec((d, n), lambda j, b: (0, blk0)),
                  pl.BlockSpec((1, n), lambda j, b: (0, 0))],
        out_specs=[pl.BlockSpec((1, s, SSD_GROUPS * n), lambda j, b: (b, 0, 0)),
                   pl.BlockSpec((1, SSD_GROUPS, 2, nc * SSD_HPG, SSD_CHUNK), lambda j, b: (b, 0, 0, 0, 0))],
        out_shape=[jax.ShapeDtypeStruct((bsz, s, SSD_GROUPS * n), F32),
                   jax.ShapeDtypeStruct((bsz, SSD_GROUPS, 2, nc * SSD_HPG, SSD_CHUNK), F32)],
        scratch_shapes=[pltpu.VMEM((d, n), BF16)],
        compiler_params=_params("arbitrary", "arbitrary"),
        name="proj_dt",
    )(h, w_in, dt_bias.reshape(1, n))


def _proj_call(body, h, w_in, col0, n_tiles, tn, out_cols, out_dtype, extra=(), extra_specs=(),
               scratch=(), out_tn=None, name="proj"):
    bsz, s, d = h.shape
    out_tn = tn if out_tn is None else out_tn
    blk0 = col0 // tn
    assert blk0 * tn == col0
    return pl.pallas_call(
        body,
        grid=(n_tiles, bsz),
        in_specs=[pl.BlockSpec((1, s, d), lambda j, b: (b, 0, 0)),
                  pl.BlockSpec((d, tn), lambda j, b: (0, blk0 + j)),
                  *extra_specs],
        out_specs=pl.BlockSpec((1, s, out_tn), lambda j, b: (b, 0, j)),
        out_shape=jax.ShapeDtypeStruct((bsz, s, out_cols), out_dtype),
        scratch_shapes=[pltpu.VMEM((d, tn), BF16), *scratch],
        compiler_params=_params("arbitrary", "arbitrary"),
        name=name,
    )(h, w_in, *extra)


def _proj_qkv(h, w_in, cos, sin, group):
    bsz, s, d = h.shape
    tn = PROJ_TN
    tps = ATTN_WIDTH // tn
    n_groups = len(DILATIONS)

    def wcol(j, b):
        return (0, (j // tps) * (n_groups * tps) + group * tps + j % tps)

    tab = pl.BlockSpec((1, s, HEAD_DIM), lambda j, b: (group * bsz + b, 0, 0))
    return pl.pallas_call(
        _proj_rope_body,
        grid=(3 * tps, bsz),
        in_specs=[pl.BlockSpec((1, s, d), lambda j, b: (b, 0, 0)),
                  pl.BlockSpec((d, tn), wcol), tab, tab],
        out_specs=pl.BlockSpec((1, s, tn), lambda j, b: (b, 0, j)),
        out_shape=jax.ShapeDtypeStruct((bsz, s, 3 * ATTN_WIDTH), BF16),
        scratch_shapes=[pltpu.VMEM((d, tn), BF16)],
        compiler_params=_params("arbitrary", "arbitrary"),
        name=f"proj_qkv{group}",
    )(h, w_in, cos, sin)


ATT_BLK = 128


ATT_UNROLL = 4


def _attn_body(q1, k1, v1, q2, k2, v2, q3, k3, v3, z_ref, o_ref, on_ref, ls_ref, va_ref, bias_ref):
    s = o_ref.shape[1]
    scale = HEAD_DIM ** -0.5
    qi = lax.broadcasted_iota(jnp.int32, (ATT_BLK, 2 * ATT_BLK), 0)
    ki = lax.broadcasted_iota(jnp.int32, (ATT_BLK, 2 * ATT_BLK), 1)
    for t in range(3):
        bias_ref[t] = jnp.where(jnp.abs(qi - ki + t * N_SIDE) <= N_SIDE, 0.0, -jnp.inf)
    va_ref[:, :, HEAD_DIM:] = jnp.ones((len(DILATIONS), s, HEAD_DIM), BF16)

    def block(g, q_ref, k_ref, q0, k0, bias, nk):
        q = q_ref[0, pl.ds(q0, ATT_BLK), :]
        kw = k_ref[0, pl.ds(k0, nk), :]
        sc = lax.dot_general(q, kw, (((1,), (1,)), ((), ())), preferred_element_type=F32) * scale + bias
        m = jnp.max(sc, axis=-1, keepdims=True)
        p = jnp.exp(sc - m)
        pv = _dot(p.astype(BF16), va_ref[g, pl.ds(k0, nk), :])
        den = pv[:, HEAD_DIM:]
        return pv[:, :HEAD_DIM] / den, m + jnp.log(den)

    groups = ((q1, k1, v1), (q2, k2, v2), (q3, k3, v3))
    for g, (q_ref, k_ref, v_ref) in enumerate(groups):
        dil = DILATIONS[g]
        sub = s // dil
        nblk = sub // ATT_BLK
        va_ref[g, :, :HEAD_DIM] = v_ref[0]
        for r in range(dil):
            if nblk == 1:
                o, lse = block(g, q_ref, k_ref, r * sub, r * sub, bias_ref[0, :, :ATT_BLK], ATT_BLK)
                on_ref[g, pl.ds(r, ATT_BLK, stride=dil), :] = o
                ls_ref[g, pl.ds(r, ATT_BLK, stride=dil), :] = lse
            else:
                def one(i, q_ref=q_ref, k_ref=k_ref, r=r, sub=sub, dil=dil, g=g):
                    loc = i * ATT_BLK
                    kloc = jnp.clip(loc - N_SIDE, 0, sub - 2 * ATT_BLK)
                    q0 = pl.multiple_of(r * sub + loc, ATT_BLK)
                    k0 = pl.multiple_of(r * sub + kloc, N_SIDE)
                    bias = bias_ref[lax.shift_right_logical(loc - kloc, N_SIDE.bit_length() - 1)]
                    o, lse = block(g, q_ref, k_ref, q0, k0, bias, 2 * ATT_BLK)
                    if dil == 1:
                        rows = pl.ds(q0, ATT_BLK)
                    else:
                        rows = pl.ds(loc * dil + r, ATT_BLK, stride=dil)
                    on_ref[g, rows, :] = o
                    ls_ref[g, rows, :] = lse

                def step(io, carry, one=one):
                    for j in range(ATT_UNROLL):
                        one(io * ATT_UNROLL + j)
                    return carry

                lax.fori_loop(0, nblk // ATT_UNROLL, step, 0)

    l0, l1, l2 = ls_ref[0], ls_ref[1], ls_ref[2]
    mx = jnp.maximum(jnp.maximum(l0, l1), l2)
    e0, e1, e2 = jnp.exp(l0 - mx), jnp.exp(l1 - mx), jnp.exp(l2 - mx)
    o = (e0 * on_ref[0] + e1 * on_ref[1] + e2 * on_ref[2]) / (e0 + e1 + e2)
    o_ref[0] = (o * z_ref[0].astype(F32)).astype(BF16)


def _attention(qkv, zs):
    bsz, s, _ = qkv[0].shape
    specs = []
    args = []
    for g in range(len(DILATIONS)):
        for sec in range(3):
            specs.append(pl.BlockSpec((1, s, HEAD_DIM),
                                      lambda b, h, sec=sec: (b, 0, sec * ATTN_HEADS + h)))
            args.append(qkv[g])
    specs.append(pl.BlockSpec((1, s, HEAD_DIM), lambda b, h: (b, 0, h)))
    args.append(zs)
    return pl.pallas_call(
        _attn_body,
        grid=(bsz, ATTN_HEADS),
        in_specs=specs,
        out_specs=pl.BlockSpec((1, s, HEAD_DIM), lambda b, h: (b, 0, h)),
        out_shape=jax.ShapeDtypeStruct((bsz, s, ATTN_WIDTH), BF16),
        scratch_shapes=[pltpu.VMEM((3, s, HEAD_DIM), F32), pltpu.VMEM((3, s, HEAD_DIM), F32),
                        pltpu.VMEM((len(DILATIONS), s, 2 * HEAD_DIM), BF16),
                        pltpu.VMEM((3, ATT_BLK, 2 * ATT_BLK), F32)],
        compiler_params=_params("arbitrary", "arbitrary"),
        name="dilated_attention",
    )(*args)


SSD_UNROLL = 2


def _ssd_constants():
    q, gw = SSD_CHUNK, SSD_HPG * SSD_HEADDIM
    i = np.arange(q)
    low = i[None, :] <= i[:, None]
    upp = i[None, :] >= i[:, None]
    tri = np.stack([low, upp])
    tri3 = np.stack([np.tile(upp, (3, 1)), np.tile(low, (3, 1))])
    c = np.arange(LANES)[:, None]
    in_f = c < 3 * SSD_HPG
    in_b = (c >= SSD_HEADS) & (c < SSD_HEADS + 3 * SSD_HPG)
    head = np.arange(gw)[None, :] // SSD_HEADDIM
    efb = np.concatenate([in_f & (c % SSD_HPG == head), in_b & (c % SSD_HPG == head)], axis=1)
    k = np.arange(2 * SSD_HPG * LANES)[None, :] // LANES
    ecol = np.where(k < SSD_HPG, in_f & (c % SSD_HPG == k), in_b & (c % SSD_HPG == k - SSD_HPG))
    return tuple(jnp.asarray(m, BF16) for m in (tri, tri3, efb, ecol))


def _ssd_body(x_ref, b_ref, c_ref, dt_ref, dtt_ref, alog_ref, alogt_ref, dskip_ref, z_ref,
              tri_ref, tri3_ref, efb_ref, ecol_ref, y_ref,
              rowf_ref, pcum_ref, eexp_ref, cdx_ref, stb_ref, prev_ref):
    q = SSD_CHUNK
    s = x_ref.shape[1]
    nc = s // q
    gw = SSD_HPG * SSD_HEADDIM
    fwd_lane = lax.broadcasted_iota(jnp.int32, (q, LANES), 1) < SSD_HEADS
    lane = lax.broadcasted_iota(jnp.int32, (q, LANES), 1)

    a_row = -jnp.exp(alog_ref[0])
    dt_all = dt_ref[0]
    a_all = dt_all * a_row
    a_wide = jnp.concatenate([a_all[c * q:(c + 1) * q] for c in range(nc)], axis=1)
    fwd_wide = (lax.broadcasted_iota(jnp.int32, a_wide.shape, 1) & (LANES - 1)) < SSD_HEADS
    cum_wide = jnp.where(fwd_wide, _exact_lmul(tri_ref[0], a_wide), _exact_lmul(tri_ref[1], a_wide))

    a_t = dtt_ref[0, 0] * -jnp.exp(alogt_ref[0])
    for d in range(2):
        parts = jnp.concatenate(_split3(a_t[d]), axis=1)
        rowf_ref[d] = _dot(parts, tri3_ref[d])
        rowf_ref[2 + d] = dtt_ref[0, 0, d]

    bt = b_ref[0].T
    state_f = jnp.zeros((SSD_STATE, gw), F32)
    for c in range(nc):
        cum = cum_wide[:, c * LANES:(c + 1) * LANES]
        dt = dt_all[c * q:(c + 1) * q]
        ref = jnp.where(fwd_lane[0:1], cum[q - 1:q], cum[0:1])
        pcum_ref[c] = _packed_split(cum)
        eexp_ref[c] = _dot(_packed_split(jnp.exp(cum)), efb_ref[...])
        wexp = _dot(_packed_split(dt * jnp.exp(ref - cum)), efb_ref[...])
        cdx = _dot(_packed_split(jnp.broadcast_to(jnp.exp(ref), (8, LANES))), efb_ref[...])[0:1]
        cdx_ref[c] = cdx
        xb = x_ref[0, c * q:(c + 1) * q, :].astype(F32)
        xw = (jnp.concatenate([xb, xb], axis=1) * wexp).astype(BF16)
        st = _dot(bt[:, c * q:(c + 1) * q], xw)
        stb_ref[c] = st[:, gw:]
        prev_ref[c, :, 0:gw] = state_f.astype(BF16)
        state_f = state_f * cdx[:, 0:gw] + st[:, 0:gw]
    state_b = jnp.zeros((SSD_STATE, gw), F32)
    for c in range(nc - 1, -1, -1):
        prev_ref[c, :, gw:] = state_b.astype(BF16)
        state_b = state_b * cdx_ref[c][:, gw:] + stb_ref[c]

    row = lax.broadcasted_iota(jnp.int32, (q, q), 0)
    col = lax.broadcasted_iota(jnp.int32, (q, q), 1)
    lower = row >= col
    upper = row <= col

    def chunk_out(c):
        rows = pl.ds(pl.multiple_of(c * q, q), q)
        r8 = pl.ds(pl.multiple_of(c * SSD_HPG, SSD_HPG), SSD_HPG)
        bc = b_ref[0, rows, :]
        cc = c_ref[0, rows, :]
        xb = x_ref[0, rows, :]
        cb = lax.dot_general(cc, bc, (((1,), (1,)), ((), ())), preferred_element_type=F32)
        colb = _dot(pcum_ref[c], ecol_ref[...])
        cft, rbt = rowf_ref[0, r8, :], rowf_ref[1, r8, :]
        dtf, dtb = rowf_ref[2, r8, :], rowf_ref[3, r8, :]
        ms = []
        for e in range(SSD_HPG):
            dl = colb[:, e * LANES:(e + 1) * LANES] - cft[e:e + 1, :]
            wf = jnp.exp(jnp.where(lower, dl, -jnp.inf)) * dtf[e:e + 1, :]
            k = SSD_HPG + e
            db = colb[:, k * LANES:(k + 1) * LANES] - rbt[e:e + 1, :]
            wb = jnp.exp(jnp.where(upper, db, -jnp.inf)) * dtb[e:e + 1, :]
            ms.append((cb * (wf + wb)).astype(BF16))
        ys = []
        for p in range(SSD_HPG // 2):
            lhs = jnp.concatenate([ms[2 * p], ms[2 * p + 1]], axis=1)
            xp = xb[:, p * LANES:(p + 1) * LANES]
            zero = jnp.zeros_like(xp)
            rhs = jnp.concatenate([jnp.where(lane < SSD_HEADDIM, xp, zero),
                                   jnp.where(lane >= SSD_HEADDIM, xp, zero)], axis=0)
            ys.append(_dot(lhs, rhs))
        y = jnp.concatenate(ys, axis=1)
        yoff = _dot(cc, prev_ref[c]) * eexp_ref[c]
        y = y + yoff[:, 0:gw] + yoff[:, gw:]
        y = (y + dskip_ref[...] * xb.astype(F32)) * z_ref[0, rows, :].astype(F32)
        y_ref[0, rows, :] = y.astype(BF16)

    def chunk_step(io, carry):
        for j in range(SSD_UNROLL):
            chunk_out(io * SSD_UNROLL + j)
        return carry

    lax.fori_loop(0, nc // SSD_UNROLL, chunk_step, 0)


def _ssd(xbc, dtx, dtt, alog_g, alogt_g, dskip_x, zs):
    bsz, s, _ = xbc.shape
    q = SSD_CHUNK
    nc = s // q
    gw = SSD_HPG * SSD_HEADDIM
    nb = SSD_WIDTH // SSD_STATE
    zoff = ATTN_WIDTH // gw
    consts = _ssd_constants()
    const_specs = [pl.BlockSpec(m.shape, lambda b, g, nd=m.ndim: (0,) * nd) for m in consts]
    return pl.pallas_call(
        _ssd_body,
        grid=(bsz, SSD_GROUPS),
        in_specs=[pl.BlockSpec((1, s, gw), lambda b, g: (b, 0, g)),
                  pl.BlockSpec((1, s, SSD_STATE), lambda b, g: (b, 0, nb + g)),
                  pl.BlockSpec((1, s, SSD_STATE), lambda b, g: (b, 0, nb + SSD_GROUPS + g)),
                  pl.BlockSpec((1, s, LANES), lambda b, g: (b, 0, g)),
                  pl.BlockSpec((1, 1, 2, nc * SSD_HPG, q), lambda b, g: (b, g, 0, 0, 0)),
                  pl.BlockSpec((1, 1, LANES), lambda b, g: (g, 0, 0)),
                  pl.BlockSpec((1, 2, nc * SSD_HPG, q), lambda b, g: (g, 0, 0, 0)),
                  pl.BlockSpec((1, gw), lambda b, g: (0, g)),
                  pl.BlockSpec((1, s, gw), lambda b, g: (b, 0, zoff + g)),
                  *const_specs],
        out_specs=pl.BlockSpec((1, s, gw), lambda b, g: (b, 0, g)),
        out_shape=jax.ShapeDtypeStruct((bsz, s, SSD_WIDTH), BF16),
        scratch_shapes=[pltpu.VMEM((4, nc * SSD_HPG, q), F32),
                        pltpu.VMEM((nc, q, LANES), BF16),
                        pltpu.VMEM((nc, q, 2 * gw), F32),
                        pltpu.VMEM((nc, 1, 2 * gw), F32),
                        pltpu.VMEM((nc, SSD_STATE, gw), F32),
                        pltpu.VMEM((nc, SSD_STATE, 2 * gw), BF16)],
        compiler_params=_params("arbitrary", "arbitrary"),
        name="ssd",
    )(xbc, xbc, xbc, dtx, dtt, alog_g, alogt_g, dskip_x, zs, *consts)


TAIL_TM = 512
TAIL_A_TM = 256


def _tail_a_body(ya_ref, y_ref, g_ref, ng_ref, wa_ref, ws_ref, o_ref):
    d = o_ref.shape[1]
    ya = _dot(ya_ref[...], wa_ref[...])
    y = y_ref[...].astype(F32)
    yn = y * lax.rsqrt(jnp.mean(y * y, axis=-1, keepdims=True) + EPS) * ng_ref[...]
    ys = _dot(yn.astype(BF16), ws_ref[...])
    off = 2 * SSD_HEADS
    ga = g_ref[:, off:off + d].astype(F32)
    gs = g_ref[:, off + d:off + 2 * d].astype(F32)
    o_ref[...] = (ga * ya + gs * ys).astype(BF16)


def _tail_a(ya_in, y, gates, norm_g, wa, ws):
    n, d = ya_in.shape[0], wa.shape[1]
    tm = TAIL_A_TM
    resident = functools.partial(pl.BlockSpec, pipeline_mode=pl.Buffered(1))
    return pl.pallas_call(
        _tail_a_body,
        grid=(n // tm,),
        in_specs=[pl.BlockSpec((tm, ya_in.shape[1]), lambda i: (i, 0)),
                  pl.BlockSpec((tm, y.shape[1]), lambda i: (i, 0)),
                  pl.BlockSpec((tm, gates.shape[1]), lambda i: (i, 0)),
                  pl.BlockSpec((1, y.shape[1]), lambda i: (0, 0)),
                  resident(wa.shape, lambda i: (0, 0)),
                  resident(ws.shape, lambda i: (0, 0))],
        out_specs=pl.BlockSpec((tm, d), lambda i: (i, 0)),
        out_shape=jax.ShapeDtypeStruct((n, d), BF16),
        compiler_params=_params("arbitrary"),
        name="tail_a",
    )(ya_in, y, gates, norm_g, wa, ws)


def _tail_b_body(m_ref, x_ref, ada_ref, w_ref, fg_ref, o_ref):
    t = _dot(m_ref[...], w_ref[...])
    xn = x_ref[...] + ada_ref[0, 2:3, :] * t
    o_ref[...] = xn * lax.rsqrt(jnp.mean(xn * xn, axis=-1, keepdims=True) + EPS) * fg_ref[...]


def _tail_b(merged, x2, ada3, w_out, final_g, seq):
    n, d = x2.shape
    tm = TAIL_TM
    per = seq // tm
    resident = functools.partial(pl.BlockSpec, pipeline_mode=pl.Buffered(1))
    return pl.pallas_call(
        _tail_b_body,
        grid=(n // tm,),
        in_specs=[pl.BlockSpec((tm, d), lambda i: (i, 0)),
                  pl.BlockSpec((tm, d), lambda i: (i, 0)),
                  pl.BlockSpec((1, 3, d), lambda i: (i // per, 0, 0)),
                  resident(w_out.shape, lambda i: (0, 0)),
                  pl.BlockSpec((1, d), lambda i: (0, 0))],
        out_specs=pl.BlockSpec((tm, d), lambda i: (i, 0)),
        out_shape=jax.ShapeDtypeStruct((n, d), F32),
        compiler_params=_params("arbitrary"),
        name="tail_b",
    )(merged, x2, ada3, w_out, final_g)


def _layer(x, ada3, pos_tabs, norm_g, w_in, conv_w, conv_b, dt_bias, a_log, d_skip, ssd_norm_g,
           w_br_attn, w_br_ssd, w_out, out_g):
    bsz, s, d = x.shape
    cos, sin = pos_tabs
    hs = _modulated_norm(x, ada3, norm_g[None])
    qkv = [_proj_qkv(hs[g], w_in, cos, sin, g) for g in range(len(DILATIONS))]
    h = hs[0]
    tn = PROJ_TN
    zs = _proj_call(_proj_silu_body, h, w_in, COL_Z, (ATTN_WIDTH + SSD_WIDTH) // tn, tn,
                    ATTN_WIDTH + SSD_WIDTH, BF16, name="proj_z")
    xbc = _proj_call(
        _proj_conv_body, h, w_in, COL_XBC, CONV_CH // tn, tn, CONV_CH, BF16,
        extra=(conv_w, conv_b[None]),
        extra_specs=(pl.BlockSpec((CONV_WIDTH, tn), lambda j, b: (0, j)),
                     pl.BlockSpec((1, tn), lambda j, b: (0, j))),
        scratch=(pltpu.VMEM((s + 2 * CONV_PAD, tn), F32),), name="proj_xbc")
    n_dt = 2 * SSD_HEADS
    dtx, dtt = _proj_dt(h, w_in, dt_bias)
    n_gate_tiles = -(-(IN_COLS - COL_DT) // tn)
    gates = _proj_call(_proj_sigmoid_body, h, w_in, COL_DT, n_gate_tiles, tn, n_gate_tiles * tn,
                       BF16, name="proj_gates")

    ya_in = _attention(qkv, zs)

    a_flat = a_log.reshape(1, n_dt)
    rolled = {k: jnp.roll(a_flat, -k * SSD_HPG, axis=1) for k in range(-2, SSD_GROUPS)}
    m0, m1 = _rep_masks(a_flat.shape)
    alog_g = jnp.stack([_replicate_heads(rolled, g, m0, m1) for g in range(SSD_GROUPS)])
    nc = s // SSD_CHUNK
    alogt_g = a_log.reshape(2, SSD_GROUPS, SSD_HPG).transpose(1, 0, 2)
    alogt_g = jnp.broadcast_to(alogt_g[:, :, None, :, None], (SSD_GROUPS, 2, nc, SSD_HPG, SSD_CHUNK))
    alogt_g = alogt_g.reshape(SSD_GROUPS, 2, nc * SSD_HPG, SSD_CHUNK)
    dskip_x = jnp.repeat(d_skip, SSD_HEADDIM)[None]
    y = _ssd(xbc, dtx, dtt, alog_g, alogt_g, dskip_x, zs)

    n = bsz * s
    merged = _tail_a(ya_in.reshape(n, ATTN_WIDTH), y.reshape(n, SSD_WIDTH),
                     gates.reshape(n, gates.shape[-1]), ssd_norm_g[None],
                     w_br_attn.astype(BF16), w_br_ssd.astype(BF16))
    out = _tail_b(merged, x.reshape(n, d), ada3, w_out.astype(BF16), out_g, s)
    return out.reshape(bsz, s, d)


def kernel(x, c, positions, norm_g, w_ada, b_ada, w_in, conv_w, conv_b, dt_bias, a_log, d_skip,
           ssd_norm_g, w_br_attn, w_br_ssd, w_out, final_g):
    bsz, s, d = x.shape
    depth = w_in.shape[0]
    inv = ROPE_THETA ** (-jnp.arange(0, HEAD_DIM, 2, dtype=F32) / HEAD_DIM)
    inv2 = jnp.concatenate([inv, inv])[None]
    pos_orders = [positions.reshape(bsz, s // dil, dil).transpose(0, 2, 1).reshape(bsz, s)
                  for dil in DILATIONS]
    pos3 = jnp.concatenate(pos_orders, axis=0)[..., None]
    pos_tabs = _rope_tables(pos3, inv2)
    for i in range(depth):
        ada3 = _ada(c, w_ada[i], b_ada[i][None]).reshape(bsz, 3, d)
        assert depth == 1
        x = _layer(x, ada3, pos_tabs, norm_g[i], w_in[i], conv_w[i], conv_b[i], dt_bias[i], a_log[i],
                   d_skip[i], ssd_norm_g[i], w_br_attn[i], w_br_ssd[i], w_out[i], final_g[None])
    return x
```

```python
import functools

import numpy as np
import jax
import jax.numpy as jnp
from jax import lax
from jax.experimental import pallas as pl
from jax.experimental.pallas import tpu as pltpu

F32 = jnp.float32
BF16 = jnp.bfloat16

D_MODEL = 2048
HEAD_DIM = 128
ATTN_HEADS = 12
DILATIONS = (1, 4, 16)
N_SIDE = 64
ATTN_WIDTH = ATTN_HEADS * HEAD_DIM
ROPE_THETA = 10000.0
SSD_WIDTH = 2 * D_MODEL
SSD_HEADDIM = 64
SSD_GROUPS = 8
SSD_HEADS = SSD_WIDTH // SSD_HEADDIM
SSD_HPG = SSD_HEADS // SSD_GROUPS
SSD_STATE = 128
SSD_CHUNK = 128
CONV_WIDTH = 5
CONV_CH = SSD_WIDTH + 2 * SSD_GROUPS * SSD_STATE
EPS = 1e-6
QKV_COLS = 3 * len(DILATIONS) * ATTN_WIDTH
COL_Z = QKV_COLS
COL_XBC = COL_Z + ATTN_WIDTH + SSD_WIDTH
COL_DT = COL_XBC + CONV_CH
COL_GATE = COL_DT + 2 * SSD_HEADS
IN_COLS = COL_GATE + 2 * D_MODEL

LANES = 128
PROJ_TN = 512
V7X_VMEM_LIMIT_BYTES = 56 * 1024 * 1024


def _params(*sem):
    return pltpu.CompilerParams(dimension_semantics=sem, vmem_limit_bytes=V7X_VMEM_LIMIT_BYTES)


def _dot(a, b):
    return jnp.dot(a, b, preferred_element_type=F32)


def _sigmoid(x):
    return 1.0 / (1.0 + jnp.exp(-x))


def _split3(v):
    hi = v.astype(BF16)
    r = v - hi.astype(F32)
    mid = r.astype(BF16)
    lo = (r - mid.astype(F32)).astype(BF16)
    return hi, mid, lo


def _exact_lmul(t01, v):
    hi, mid, lo = _split3(v)
    return _dot(t01, hi) + _dot(t01, mid) + _dot(t01, lo)


def _exact_rmul(v, e01):
    hi, mid, lo = _split3(v)
    return _dot(hi, e01) + _dot(mid, e01) + _dot(lo, e01)


def _ada_body(c_ref, w_ref, b_ref, o_ref):
    o_ref[...] = _dot(c_ref[...].astype(BF16), w_ref[...].astype(BF16)) + b_ref[...]


def _ada(c, w, bias):
    bsz, d = c.shape
    n = w.shape[1]
    tn = 768
    return pl.pallas_call(
        _ada_body,
        grid=(n // tn,),
        in_specs=[pl.BlockSpec((bsz, d), lambda j: (0, 0)),
                  pl.BlockSpec((d, tn), lambda j: (0, j)),
                  pl.BlockSpec((1, tn), lambda j: (0, j))],
        out_specs=pl.BlockSpec((bsz, tn), lambda j: (0, j)),
        out_shape=jax.ShapeDtypeStruct((bsz, n), F32),
        compiler_params=_params("arbitrary"),
        name="ada",
    )(c, w, bias)


def _rope_body(pos_ref, inv_ref, cos_ref, sin_ref):
    s = pos_ref.shape[1]
    ang = pos_ref[0].astype(F32) * inv_ref[...]
    lane = lax.broadcasted_iota(jnp.int32, ang.shape, 1)
    sin = jnp.sin(ang)
    cos_ref[0, 0] = jnp.cos(ang)
    sin_ref[0, 0] = jnp.where(lane < HEAD_DIM // 2, -sin, sin)
    for tab in (cos_ref, sin_ref):
        for g, dil in enumerate(DILATIONS[1:], start=1):
            sub = s // dil
            for r in range(dil):
                tab[g, 0, r * sub:(r + 1) * sub, :] = tab[0, 0, pl.ds(r, sub, stride=dil), :]


def _rope_tables(pos, inv2):
    bsz, s, _ = pos.shape
    ng = len(DILATIONS)
    spec = pl.BlockSpec((ng, 1, s, HEAD_DIM), lambda b: (0, b, 0, 0))
    return pl.pallas_call(
        _rope_body,
        grid=(bsz,),
        in_specs=[pl.BlockSpec((1, s, 1), lambda b: (b, 0, 0)),
                  pl.BlockSpec((1, HEAD_DIM), lambda b: (0, 0))],
        out_specs=[spec, spec],
        out_shape=[jax.ShapeDtypeStruct((ng, bsz, s, HEAD_DIM), F32)] * 2,
        compiler_params=_params("arbitrary"),
        name="rope_tables",
    )(pos, inv2)


H_TILE = 512


def _h_body(x_ref, ada_ref, g_ref, h1_ref, h4_ref, h16_ref, hs_ref):
    x = x_ref[0]
    ms = jnp.mean(x * x, axis=-1, keepdims=True)
    xn = x * lax.rsqrt(ms + EPS) * g_ref[...]
    h = xn * (1.0 + ada_ref[0, 1:2, :]) + ada_ref[0, 0:1, :]
    h1_ref[0] = h.astype(BF16)
    for cb in range(h.shape[1] // LANES):
        cols = slice(cb * LANES, (cb + 1) * LANES)
        hs_ref[cb] = h[:, cols]
        for r in range(4):
            h4_ref[0, r, :, cols] = hs_ref[cb, pl.ds(r, H_TILE // 4, stride=4), :].astype(BF16)
        for r in range(16):
            h16_ref[0, r, :, cols] = hs_ref[cb, pl.ds(r, H_TILE // 16, stride=16), :].astype(BF16)


def _modulated_norm(x, ada3, g):
    bsz, s, d = x.shape
    t = H_TILE
    outs = pl.pallas_call(
        _h_body,
        grid=(bsz, s // t),
        in_specs=[pl.BlockSpec((1, t, d), lambda b, i: (b, i, 0)),
                  pl.BlockSpec((1, 3, d), lambda b, i: (b, 0, 0)),
                  pl.BlockSpec((1, d), lambda b, i: (0, 0))],
        out_specs=[pl.BlockSpec((1, t, d), lambda b, i: (b, i, 0)),
                   pl.BlockSpec((1, 4, t // 4, d), lambda b, i: (b, 0, i, 0)),
                   pl.BlockSpec((1, 16, t // 16, d), lambda b, i: (b, 0, i, 0))],
        out_shape=[jax.ShapeDtypeStruct((bsz, s, d), BF16),
                   jax.ShapeDtypeStruct((bsz, 4, s // 4, d), BF16),
                   jax.ShapeDtypeStruct((bsz, 16, s // 16, d), BF16)],
        scratch_shapes=[pltpu.VMEM((d // LANES, t, LANES), F32)],
        compiler_params=_params("arbitrary", "arbitrary"),
        name="modulated_norm",
    )(x, ada3, g)
    h1, h4, h16 = outs
    return h1, h4.reshape(bsz, s, d), h16.reshape(bsz, s, d)


PROJ_ROWS = 512


def _cast_weight_tile(w_ref, wbf_ref):
    @pl.when(pl.program_id(1) == 0)
    def _():
        wbf_ref[...] = w_ref[...].astype(BF16)


def _proj_chunks(h_ref, wbf_ref, epilogue):
    for m in range(h_ref.shape[1] // PROJ_ROWS):
        rows = slice(m * PROJ_ROWS, (m + 1) * PROJ_ROWS)
        epilogue(rows, _dot(h_ref[0, rows, :], wbf_ref[...]))


def _proj_rope_body(h_ref, w_ref, cos_ref, sin_ref, o_ref, wbf_ref):
    _cast_weight_tile(w_ref, wbf_ref)
    j = pl.program_id(0)
    n_rot = 2 * ATTN_WIDTH // PROJ_TN

    def rotate(rows, acc):
        cos = cos_ref[0, rows, :]
        sin = sin_ref[0, rows, :]
        for hh in range(PROJ_TN // HEAD_DIM):
            t = acc[:, hh * HEAD_DIM:(hh + 1) * HEAD_DIM]
            rot = t * cos + pltpu.roll(t, HEAD_DIM // 2, 1) * sin
            o_ref[0, rows, hh * HEAD_DIM:(hh + 1) * HEAD_DIM] = rot.astype(BF16)

    def plain(rows, acc):
        o_ref[0, rows, :] = acc.astype(BF16)

    @pl.when(j < n_rot)
    def _():
        _proj_chunks(h_ref, wbf_ref, rotate)

    @pl.when(j >= n_rot)
    def _():
        _proj_chunks(h_ref, wbf_ref, plain)


def _proj_silu_body(h_ref, w_ref, o_ref, wbf_ref):
    _cast_weight_tile(w_ref, wbf_ref)

    def epilogue(rows, acc):
        o_ref[0, rows, :] = (acc * _sigmoid(acc)).astype(BF16)

    _proj_chunks(h_ref, wbf_ref, epilogue)


def _proj_sigmoid_body(h_ref, w_ref, o_ref, wbf_ref):
    _cast_weight_tile(w_ref, wbf_ref)

    def epilogue(rows, acc):
        o_ref[0, rows, :] = _sigmoid(acc).astype(BF16)

    _proj_chunks(h_ref, wbf_ref, epilogue)


CONV_PAD = 8
SUBLANES = 8
CONV_STRIDE = PROJ_ROWS // SUBLANES + 1
CONV_TAIL = 3 * SUBLANES
CONV_CHUNKS = 4


def _proj_conv_body(h_ref, w_ref, cw_ref, cb_ref, o_ref, wbf_ref, *scratch):
    pads, stages = scratch[:CONV_CHUNKS], scratch[CONV_CHUNKS:]
    n_cb = PROJ_TN // LANES
    half = (CONV_WIDTH - 1) // 2
    end = CONV_PAD + PROJ_ROWS
    _cast_weight_tile(w_ref, wbf_ref)
    pads[0][:, 0:CONV_PAD, :] = jnp.zeros((n_cb, CONV_PAD, LANES), F32)
    pads[-1][:, end:end + CONV_TAIL, :] = jnp.zeros((n_cb, CONV_TAIL, LANES), F32)

    def conv_chunk(m):
        base = m * PROJ_ROWS
        for cb in range(n_cb):
            cols = slice(cb * LANES, (cb + 1) * LANES)
            taps = [cw_ref[k:k + 1, cols] for k in range(CONV_WIDTH)]
            bias = cb_ref[:, cols]

            def tile(i):
                return pads[m][cb, pl.ds(CONV_PAD + i, SUBLANES, stride=CONV_STRIDE), :]

            win = [tile(i) for i in range(-half, half)]
            for i in range(CONV_STRIDE):
                win.append(tile(i + half))
                out = bias + taps[0] * win[0]
                for k in range(1, CONV_WIDTH):
                    out = out + taps[k] * win[k]
                stages[m][cb, pl.ds(i, SUBLANES, stride=CONV_STRIDE), :] = out * _sigmoid(out)
                win.pop(0)
            o_ref[0, base:base + PROJ_ROWS, cols] = stages[m][cb, 0:PROJ_ROWS, :].astype(BF16)

    def epilogue(rows, acc):
        m = rows.start // PROJ_ROWS
        for cb in range(n_cb):
            a = acc[:, cb * LANES:(cb + 1) * LANES]
            pads[m][cb, CONV_PAD:end, :] = a
            if m > 0:
                pads[m - 1][cb, end:end + CONV_TAIL, :] = a[0:CONV_TAIL]
            if m + 1 < CONV_CHUNKS:
                pads[m + 1][cb, 0:CONV_PAD, :] = a[PROJ_ROWS - CONV_PAD:PROJ_ROWS]
        if m > 0:
            conv_chunk(m - 1)

    assert h_ref.shape[1] == CONV_CHUNKS * PROJ_ROWS
    _proj_chunks(h_ref, wbf_ref, epilogue)
    conv_chunk(CONV_CHUNKS - 1)


def _rep_masks(shape):
    lane = lax.broadcasted_iota(jnp.int32, shape, len(shape) - 1) & (SSD_HEADS - 1)
    return lane < SSD_HPG, lane < 2 * SSD_HPG


def _packed_split(v):
    hi, mid, lo = _split3(v)
    m0, m1 = _rep_masks(v.shape)
    return jnp.where(m0, hi, jnp.where(m1, mid, lo))


def _replicate_heads(rolled, g, m0, m1):
    return jnp.where(m0, rolled[g], jnp.where(m1, rolled[g - 1], rolled[g - 2]))


def _proj_dt_body(h_ref, w_ref, bias_ref, o_ref, ot_ref, wbf_ref):
    s = h_ref.shape[1]
    _cast_weight_tile(w_ref, wbf_ref)
    acc = _dot(h_ref[0], wbf_ref[...]) + bias_ref[...]
    sp = jnp.maximum(acc, 0.0) + jnp.log1p(jnp.exp(-jnp.abs(acc)))
    n = 2 * SSD_HEADS
    m0, m1 = _rep_masks(sp.shape)
    rolled = {k: (sp if k == 0 else pltpu.roll(sp, (-k * SSD_HPG) % n, 1))
              for k in range(-2, SSD_GROUPS)}
    for g in range(SSD_GROUPS):
        o_ref[0, :, g * n:(g + 1) * n] = _replicate_heads(rolled, g, m0, m1)
    for c in range(s // SSD_CHUNK):
        t = sp[c * SSD_CHUNK:(c + 1) * SSD_CHUNK, :].T
        for g in range(SSD_GROUPS):
            for d in range(2):
                r0 = d * SSD_HEADS + g * SSD_HPG
                ot_ref[0, g, d, c * SSD_HPG:(c + 1) * SSD_HPG, :] = t[r0:r0 + SSD_HPG, :]


def _proj_dt(h, w_in, dt_bias):
    bsz, s, d = h.shape
    n = 2 * SSD_HEADS
    nc = s // SSD_CHUNK
    blk0 = COL_DT // n
    return pl.pallas_call(
        _proj_dt_body,
        grid=(1, bsz),
        in_specs=[pl.BlockSpec((1, s, d), lambda j, b: (b, 0, 0)),
                  pl.BlockSpec((d, n), lambda j, b: (0, blk0)),
                  pl.BlockSpec((1, n), lambda j, b: (0, 0))],
        out_specs=[pl.BlockSpec((1, s, SSD_GROUPS * n), lambda j, b: (b, 0, 0)),
                   pl.BlockSpec((1, SSD_GROUPS, 2, nc * SSD_HPG, SSD_CHUNK), lambda j, b: (b, 0, 0, 0, 0))],
        out_shape=[jax.ShapeDtypeStruct((bsz, s, SSD_GROUPS * n), F32),
                   jax.ShapeDtypeStruct((bsz, SSD_GROUPS, 2, nc * SSD_HPG, SSD_CHUNK), F32)],
        scratch_shapes=[pltpu.VMEM((d, n), BF16)],
        compiler_params=_params("arbitrary", "arbitrary"),
        name="proj_dt",
    )(h, w_in, dt_bias.reshape(1, n))


def _proj_call(body, h, w_in, col0, n_tiles, tn, out_cols, out_dtype, extra=(), extra_specs=(),
               scratch=(), out_tn=None, name="proj"):
    bsz, s, d = h.shape
    out_tn = tn if out_tn is None else out_tn
    blk0 = col0 // tn
    assert blk0 * tn == col0
    return pl.pallas_call(
        body,
        grid=(n_tiles, bsz),
        in_specs=[pl.BlockSpec((1, s, d), lambda j, b: (b, 0, 0)),
                  pl.BlockSpec((d, tn), lambda j, b: (0, blk0 + j)),
                  *extra_specs],
        out_specs=pl.BlockSpec((1, s, out_tn), lambda j, b: (b, 0, j)),
        out_shape=jax.ShapeDtypeStruct((bsz, s, out_cols), out_dtype),
        scratch_shapes=[pltpu.VMEM((d, tn), BF16), *scratch],
        compiler_params=_params("arbitrary", "arbitrary"),
        name=name,
    )(h, w_in, *extra)


def _proj_qkv(h, w_in, cos, sin, group):
    bsz, s, d = h.shape
    tn = PROJ_TN
    tps = ATTN_WIDTH // tn
    n_groups = len(DILATIONS)

    def wcol(j, b):
        return (0, (j // tps) * (n_groups * tps) + group * tps + j % tps)

    tab = pl.BlockSpec((1, s, HEAD_DIM), lambda j, b: (group * bsz + b, 0, 0))
    return pl.pallas_call(
        _proj_rope_body,
        grid=(3 * tps, bsz),
        in_specs=[pl.BlockSpec((1, s, d), lambda j, b: (b, 0, 0)),
                  pl.BlockSpec((d, tn), wcol), tab, tab],
        out_specs=pl.BlockSpec((1, s, tn), lambda j, b: (b, 0, j)),
        out_shape=jax.ShapeDtypeStruct((bsz, s, 3 * ATTN_WIDTH), BF16),
        scratch_shapes=[pltpu.VMEM((d, tn), BF16)],
        compiler_params=_params("arbitrary", "arbitrary"),
        name=f"proj_qkv{group}",
    )(h, w_in, cos, sin)


ATT_BLK = 128


ATT_UNROLL = 8


def _attn_body(q1, k1, v1, q2, k2, v2, q3, k3, v3, z_ref, o_ref, on_ref, ls_ref, va_ref, bias_ref):
    s = o_ref.shape[1]
    scale = HEAD_DIM ** -0.5
    qi = lax.broadcasted_iota(jnp.int32, (ATT_BLK, 2 * ATT_BLK), 0)
    ki = lax.broadcasted_iota(jnp.int32, (ATT_BLK, 2 * ATT_BLK), 1)
    for t in range(3):
        bias_ref[t] = jnp.where(jnp.abs(qi - ki + t * N_SIDE) <= N_SIDE, 0.0, -jnp.inf)
    va_ref[:, :, HEAD_DIM:] = jnp.ones((len(DILATIONS), s, HEAD_DIM), BF16)

    def block(g, q_ref, k_ref, q0, k0, bias, nk):
        q = q_ref[0, pl.ds(q0, ATT_BLK), :]
        kw = k_ref[0, pl.ds(k0, nk), :]
        sc = lax.dot_general(q, kw, (((1,), (1,)), ((), ())), preferred_element_type=F32) * scale + bias
        m = jnp.max(sc, axis=-1, keepdims=True)
        p = jnp.exp(sc - m)
        pv = _dot(p.astype(BF16), va_ref[g, pl.ds(k0, nk), :])
        den = pv[:, HEAD_DIM:]
        return pv[:, :HEAD_DIM] / den, m + jnp.log(den)

    groups = ((q1, k1, v1), (q2, k2, v2), (q3, k3, v3))
    for g, (q_ref, k_ref, v_ref) in enumerate(groups):
        dil = DILATIONS[g]
        sub = s // dil
        nblk = sub // ATT_BLK
        va_ref[g, :, :HEAD_DIM] = v_ref[0]
        for r in range(dil):
            if nblk == 1:
                o, lse = block(g, q_ref, k_ref, r * sub, r * sub, bias_ref[0, :, :ATT_BLK], ATT_BLK)
                on_ref[g, pl.ds(r, ATT_BLK, stride=dil), :] = o
                ls_ref[g, pl.ds(r, ATT_BLK, stride=dil), :] = lse
            else:
                def one(i, q_ref=q_ref, k_ref=k_ref, r=r, sub=sub, dil=dil, g=g):
                    loc = i * ATT_BLK
                    kloc = jnp.clip(loc - N_SIDE, 0, sub - 2 * ATT_BLK)
                    q0 = pl.multiple_of(r * sub + loc, ATT_BLK)
                    k0 = pl.multiple_of(r * sub + kloc, N_SIDE)
                    bias = bias_ref[lax.shift_right_logical(loc - kloc, N_SIDE.bit_length() - 1)]
                    o, lse = block(g, q_ref, k_ref, q0, k0, bias, 2 * ATT_BLK)
                    if dil == 1:
                        rows = pl.ds(q0, ATT_BLK)
                    else:
                        rows = pl.ds(loc * dil + r, ATT_BLK, stride=dil)
                    on_ref[g, rows, :] = o
                    ls_ref[g, rows, :] = lse

                un = min(ATT_UNROLL, nblk)

                def step(io, carry, one=one, un=un):
                    for j in range(un):
                        one(io * un + j)
                    return carry

                lax.fori_loop(0, nblk // un, step, 0)

    l0, l1, l2 = ls_ref[0], ls_ref[1], ls_ref[2]
    mx = jnp.maximum(jnp.maximum(l0, l1), l2)
    e0, e1, e2 = jnp.exp(l0 - mx), jnp.exp(l1 - mx), jnp.exp(l2 - mx)
    o = (e0 * on_ref[0] + e1 * on_ref[1] + e2 * on_ref[2]) / (e0 + e1 + e2)
    o_ref[0] = (o * z_ref[0].astype(F32)).astype(BF16)


def _attention(qkv, zs):
    bsz, s, _ = qkv[0].shape
    specs = []
    args = []
    for g in range(len(DILATIONS)):
        for sec in range(3):
            specs.append(pl.BlockSpec((1, s, HEAD_DIM),
                                      lambda b, h, sec=sec: (b, 0, sec * ATTN_HEADS + h)))
            args.append(qkv[g])
    specs.append(pl.BlockSpec((1, s, HEAD_DIM), lambda b, h: (b, 0, h)))
    args.append(zs)
    return pl.pallas_call(
        _attn_body,
        grid=(bsz, ATTN_HEADS),
        in_specs=specs,
        out_specs=pl.BlockSpec((1, s, HEAD_DIM), lambda b, h: (b, 0, h)),
        out_shape=jax.ShapeDtypeStruct((bsz, s, ATTN_WIDTH), BF16),
        scratch_shapes=[pltpu.VMEM((3, s, HEAD_DIM), F32), pltpu.VMEM((3, s, HEAD_DIM), F32),
                        pltpu.VMEM((len(DILATIONS), s, 2 * HEAD_DIM), BF16),
                        pltpu.VMEM((3, ATT_BLK, 2 * ATT_BLK), F32)],
        compiler_params=_params("arbitrary", "arbitrary"),
        name="dilated_attention",
    )(*args)


SSD_UNROLL = 4


def _ssd_constants():
    q, gw = SSD_CHUNK, SSD_HPG * SSD_HEADDIM
    i = np.arange(q)
    low = i[None, :] <= i[:, None]
    upp = i[None, :] >= i[:, None]
    tri = np.stack([low, upp])
    tri3 = np.stack([np.tile(upp, (3, 1)), np.tile(low, (3, 1))])
    c = np.arange(LANES)[:, None]
    in_f = c < 3 * SSD_HPG
    in_b = (c >= SSD_HEADS) & (c < SSD_HEADS + 3 * SSD_HPG)
    head = np.arange(gw)[None, :] // SSD_HEADDIM
    efb = np.concatenate([in_f & (c % SSD_HPG == head), in_b & (c % SSD_HPG == head)], axis=1)
    k = np.arange(2 * SSD_HPG * LANES)[None, :] // LANES
    ecol = np.where(k < SSD_HPG, in_f & (c % SSD_HPG == k), in_b & (c % SSD_HPG == k - SSD_HPG))
    return tuple(jnp.asarray(m, BF16) for m in (tri, tri3, efb, ecol))


def _ssd_body(x_ref, b_ref, c_ref, dt_ref, dtt_ref, alog_ref, alogt_ref, dskip_ref, z_ref,
              tri_ref, tri3_ref, efb_ref, ecol_ref, y_ref,
              rowf_ref, pcum_ref, eexp_ref, cdx_ref, stb_ref, prev_ref):
    q = SSD_CHUNK
    s = x_ref.shape[1]
    nc = s // q
    gw = SSD_HPG * SSD_HEADDIM
    fwd_lane = lax.broadcasted_iota(jnp.int32, (q, LANES), 1) < SSD_HEADS
    lane = lax.broadcasted_iota(jnp.int32, (q, LANES), 1)

    a_row = -jnp.exp(alog_ref[0])
    dt_all = dt_ref[0]
    a_all = dt_all * a_row
    a_wide = jnp.concatenate([a_all[c * q:(c + 1) * q] for c in range(nc)], axis=1)
    fwd_wide = (lax.broadcasted_iota(jnp.int32, a_wide.shape, 1) & (LANES - 1)) < SSD_HEADS
    cum_wide = jnp.where(fwd_wide, _exact_lmul(tri_ref[0], a_wide), _exact_lmul(tri_ref[1], a_wide))

    a_t = dtt_ref[0, 0] * -jnp.exp(alogt_ref[0])
    for d in range(2):
        parts = jnp.concatenate(_split3(a_t[d]), axis=1)
        rowf_ref[d] = _dot(parts, tri3_ref[d])
        rowf_ref[2 + d] = dtt_ref[0, 0, d]

    bt = b_ref[0].T
    state_f = jnp.zeros((SSD_STATE, gw), F32)
    for c in range(nc):
        cum = cum_wide[:, c * LANES:(c + 1) * LANES]
        dt = dt_all[c * q:(c + 1) * q]
        ref = jnp.where(fwd_lane[0:1], cum[q - 1:q], cum[0:1])
        pcum_ref[c] = _packed_split(cum)
        eexp_ref[c] = _dot(_packed_split(jnp.exp(cum)), efb_ref[...])
        wexp = _dot(_packed_split(dt * jnp.exp(ref - cum)), efb_ref[...])
        cdx = _dot(_packed_split(jnp.broadcast_to(jnp.exp(ref), (8, LANES))), efb_ref[...])[0:1]
        cdx_ref[c] = cdx
        xb = x_ref[0, c * q:(c + 1) * q, :].astype(F32)
        xw = (jnp.concatenate([xb, xb], axis=1) * wexp).astype(BF16)
        st = _dot(bt[:, c * q:(c + 1) * q], xw)
        stb_ref[c] = st[:, gw:]
        prev_ref[c, :, 0:gw] = state_f.astype(BF16)
        state_f = state_f * cdx[:, 0:gw] + st[:, 0:gw]
    state_b = jnp.zeros((SSD_STATE, gw), F32)
    for c in range(nc - 1, -1, -1):
        prev_ref[c, :, gw:] = state_b.astype(BF16)
        state_b = state_b * cdx_ref[c][:, gw:] + stb_ref[c]

    row = lax.broadcasted_iota(jnp.int32, (q, q), 0)
    col = lax.broadcasted_iota(jnp.int32, (q, q), 1)
    lower = row >= col
    upper = row <= col

    def chunk_out(c):
        rows = pl.ds(pl.multiple_of(c * q, q), q)
        r8 = pl.ds(pl.multiple_of(c * SSD_HPG, SSD_HPG), SSD_HPG)
        bc = b_ref[0, rows, :]
        cc = c_ref[0, rows, :]
        xb = x_ref[0, rows, :]
        cb = lax.dot_general(cc, bc, (((1,), (1,)), ((), ())), preferred_element_type=F32)
        colb = _dot(pcum_ref[c], ecol_ref[...])
        cft, rbt = rowf_ref[0, r8, :], rowf_ref[1, r8, :]
        dtf, dtb = rowf_ref[2, r8, :], rowf_ref[3, r8, :]
        ms = []
        for e in range(SSD_HPG):
            dl = colb[:, e * LANES:(e + 1) * LANES] - cft[e:e + 1, :]
            wf = jnp.exp(jnp.where(lower, dl, -jnp.inf)) * dtf[e:e + 1, :]
            k = SSD_HPG + e
            db = colb[:, k * LANES:(k + 1) * LANES] - rbt[e:e + 1, :]
            wb = jnp.exp(jnp.where(upper, db, -jnp.inf)) * dtb[e:e + 1, :]
            ms.append((cb * (wf + wb)).astype(BF16))
        ys = []
        for p in range(SSD_HPG // 2):
            lhs = jnp.concatenate([ms[2 * p], ms[2 * p + 1]], axis=1)
            xp = xb[:, p * LANES:(p + 1) * LANES]
            zero = jnp.zeros_like(xp)
            rhs = jnp.concatenate([jnp.where(lane < SSD_HEADDIM, xp, zero),
                                   jnp.where(lane >= SSD_HEADDIM, xp, zero)], axis=0)
            ys.append(_dot(lhs, rhs))
        y = jnp.concatenate(ys, axis=1)
        yoff = _dot(cc, prev_ref[c]) * eexp_ref[c]
        y = y + yoff[:, 0:gw] + yoff[:, gw:]
        y = (y + dskip_ref[...] * xb.astype(F32)) * z_ref[0, rows, :].astype(F32)
        y_ref[0, rows, :] = y.astype(BF16)

    def chunk_step(io, carry):
        for j in range(SSD_UNROLL):
            chunk_out(io * SSD_UNROLL + j)
        return carry

    lax.fori_loop(0, nc // SSD_UNROLL, chunk_step, 0)


def _ssd(xbc, dtx, dtt, alog_g, alogt_g, dskip_x, zs):
    bsz, s, _ = xbc.shape
    q = SSD_CHUNK
    nc = s // q
    gw = SSD_HPG * SSD_HEADDIM
    nb = SSD_WIDTH // SSD_STATE
    zoff = ATTN_WIDTH // gw
    consts = _ssd_constants()
    const_specs = [pl.BlockSpec(m.shape, lambda b, g, nd=m.ndim: (0,) * nd) for m in consts]
    return pl.pallas_call(
        _ssd_body,
        grid=(bsz, SSD_GROUPS),
        in_specs=[pl.BlockSpec((1, s, gw), lambda b, g: (b, 0, g)),
                  pl.BlockSpec((1, s, SSD_STATE), lambda b, g: (b, 0, nb + g)),
                  pl.BlockSpec((1, s, SSD_STATE), lambda b, g: (b, 0, nb + SSD_GROUPS + g)),
                  pl.BlockSpec((1, s, LANES), lambda b, g: (b, 0, g)),
                  pl.BlockSpec((1, 1, 2, nc * SSD_HPG, q), lambda b, g: (b, g, 0, 0, 0)),
                  pl.BlockSpec((1, 1, LANES), lambda b, g: (g, 0, 0)),
                  pl.BlockSpec((1, 2, nc * SSD_HPG, q), lambda b, g: (g, 0, 0, 0)),
                  pl.BlockSpec((1, gw), lambda b, g: (0, g)),
                  pl.BlockSpec((1, s, gw), lambda b, g: (b, 0, zoff + g)),
                  *const_specs],
        out_specs=pl.BlockSpec((1, s, gw), lambda b, g: (b, 0, g)),
        out_shape=jax.ShapeDtypeStruct((bsz, s, SSD_WIDTH), BF16),
        scratch_shapes=[pltpu.VMEM((4, nc * SSD_HPG, q), F32),
                        pltpu.VMEM((nc, q, LANES), BF16),
                        pltpu.VMEM((nc, q, 2 * gw), F32),
                        pltpu.VMEM((nc, 1, 2 * gw), F32),
                        pltpu.VMEM((nc, SSD_STATE, gw), F32),
                        pltpu.VMEM((nc, SSD_STATE, 2 * gw), BF16)],
        compiler_params=_params("arbitrary", "arbitrary"),
        name="ssd",
    )(xbc, xbc, xbc, dtx, dtt, alog_g, alogt_g, dskip_x, zs, *consts)


TAIL_TM = 512
TAIL_A_TM = 256


def _tail_a_body(ya_ref, y_ref, g_ref, ng_ref, wa_ref, ws_ref, o_ref):
    d = o_ref.shape[1]
    ya = _dot(ya_ref[...], wa_ref[...])
    y = y_ref[...].astype(F32)
    yn = y * lax.rsqrt(jnp.mean(y * y, axis=-1, keepdims=True) + EPS) * ng_ref[...]
    ys = _dot(yn.astype(BF16), ws_ref[...])
    ga = g_ref[:, 0:d].astype(F32)
    gs = g_ref[:, d:2 * d].astype(F32)
    o_ref[...] = (ga * ya + gs * ys).astype(BF16)


def _tail_a(ya_in, y, gates, norm_g, wa, ws):
    n, d = ya_in.shape[0], wa.shape[1]
    tm = TAIL_A_TM
    resident = functools.partial(pl.BlockSpec, pipeline_mode=pl.Buffered(1))
    return pl.pallas_call(
        _tail_a_body,
        grid=(n // tm,),
        in_specs=[pl.BlockSpec((tm, ya_in.shape[1]), lambda i: (i, 0)),
                  pl.BlockSpec((tm, y.shape[1]), lambda i: (i, 0)),
                  pl.BlockSpec((tm, gates.shape[1]), lambda i: (i, 0)),
                  pl.BlockSpec((1, y.shape[1]), lambda i: (0, 0)),
                  resident(wa.shape, lambda i: (0, 0)),
                  resident(ws.shape, lambda i: (0, 0))],
        out_specs=pl.BlockSpec((tm, d), lambda i: (i, 0)),
        out_shape=jax.ShapeDtypeStruct((n, d), BF16),
        compiler_params=_params("arbitrary"),
        name="tail_a",
    )(ya_in, y, gates, norm_g, wa, ws)


def _tail_b_body(m_ref, x_ref, ada_ref, w_ref, fg_ref, o_ref):
    t = _dot(m_ref[...], w_ref[...])
    xn = x_ref[...] + ada_ref[0, 2:3, :] * t
    o_ref[...] = xn * lax.rsqrt(jnp.mean(xn * xn, axis=-1, keepdims=True) + EPS) * fg_ref[...]


def _tail_b(merged, x2, ada3, w_out, final_g, seq):
    n, d = x2.shape
    tm = TAIL_TM
    per = seq // tm
    resident = functools.partial(pl.BlockSpec, pipeline_mode=pl.Buffered(1))
    return pl.pallas_call(
        _tail_b_body,
        grid=(n // tm,),
        in_specs=[pl.BlockSpec((tm, d), lambda i: (i, 0)),
                  pl.BlockSpec((tm, d), lambda i: (i, 0)),
                  pl.BlockSpec((1, 3, d), lambda i: (i // per, 0, 0)),
                  resident(w_out.shape, lambda i: (0, 0)),
                  pl.BlockSpec((1, d), lambda i: (0, 0))],
        out_specs=pl.BlockSpec((tm, d), lambda i: (i, 0)),
        out_shape=jax.ShapeDtypeStruct((n, d), F32),
        compiler_params=_params("arbitrary"),
        name="tail_b",
    )(merged, x2, ada3, w_out, final_g)


def _layer(x, ada3, pos_tabs, norm_g, w_in, conv_w, conv_b, dt_bias, a_log, d_skip, ssd_norm_g,
           w_br_attn, w_br_ssd, w_out, out_g):
    bsz, s, d = x.shape
    cos, sin = pos_tabs
    hs = _modulated_norm(x, ada3, norm_g[None])
    qkv = [_proj_qkv(hs[g], w_in, cos, sin, g) for g in range(len(DILATIONS))]
    h = hs[0]
    tn = PROJ_TN
    zs = _proj_call(_proj_silu_body, h, w_in, COL_Z, (ATTN_WIDTH + SSD_WIDTH) // tn, tn,
                    ATTN_WIDTH + SSD_WIDTH, BF16, name="proj_z")
    xbc = _proj_call(
        _proj_conv_body, h, w_in, COL_XBC, CONV_CH // tn, tn, CONV_CH, BF16,
        extra=(conv_w, conv_b[None]),
        extra_specs=(pl.BlockSpec((CONV_WIDTH, tn), lambda j, b: (0, j)),
                     pl.BlockSpec((1, tn), lambda j, b: (0, j))),
        scratch=(*[pltpu.VMEM((tn // LANES, CONV_PAD + PROJ_ROWS + CONV_TAIL, LANES), F32)] * CONV_CHUNKS,
                 *[pltpu.VMEM((tn // LANES, SUBLANES * CONV_STRIDE, LANES), F32)] * CONV_CHUNKS),
        name="proj_xbc")
    n_dt = 2 * SSD_HEADS
    dtx, dtt = _proj_dt(h, w_in, dt_bias)
    gates = _proj_call(_proj_sigmoid_body, h, w_in[:, COL_GATE:], 0, 2 * d // tn, tn, 2 * d,
                       BF16, name="proj_gates")

    ya_in = _attention(qkv, zs)

    a_flat = a_log.reshape(1, n_dt)
    rolled = {k: jnp.roll(a_flat, -k * SSD_HPG, axis=1) for k in range(-2, SSD_GROUPS)}
    m0, m1 = _rep_masks(a_flat.shape)
    alog_g = jnp.stack([_replicate_heads(rolled, g, m0, m1) for g in range(SSD_GROUPS)])
    nc = s // SSD_CHUNK
    alogt_g = a_log.reshape(2, SSD_GROUPS, SSD_HPG).transpose(1, 0, 2)
    alogt_g = jnp.broadcast_to(alogt_g[:, :, None, :, None], (SSD_GROUPS, 2, nc, SSD_HPG, SSD_CHUNK))
    alogt_g = alogt_g.reshape(SSD_GROUPS, 2, nc * SSD_HPG, SSD_CHUNK)
    dskip_x = jnp.repeat(d_skip, SSD_HEADDIM)[None]
    y = _ssd(xbc, dtx, dtt, alog_g, alogt_g, dskip_x, zs)

    n = bsz * s
    merged = _tail_a(ya_in.reshape(n, ATTN_WIDTH), y.reshape(n, SSD_WIDTH),
                     gates.reshape(n, gates.shape[-1]), ssd_norm_g[None],
                     w_br_attn.astype(BF16), w_br_ssd.astype(BF16))
    out = _tail_b(merged, x.reshape(n, d), ada3, w_out.astype(BF16), out_g, s)
    return out.reshape(bsz, s, d)


def kernel(x, c, positions, norm_g, w_ada, b_ada, w_in, conv_w, conv_b, dt_bias, a_log, d_skip,
           ssd_norm_g, w_br_attn, w_br_ssd, w_out, final_g):
    bsz, s, d = x.shape
    depth = w_in.shape[0]
    inv = ROPE_THETA ** (-jnp.arange(0, HEAD_DIM, 2, dtype=F32) / HEAD_DIM)
    inv2 = jnp.concatenate([inv, inv])[None]
    cos, sin = _rope_tables(positions[..., None], inv2)
    pos_tabs = (cos.reshape(-1, s, HEAD_DIM), sin.reshape(-1, s, HEAD_DIM))
    for i in range(depth):
        ada3 = _ada(c, w_ada[i], b_ada[i][None]).reshape(bsz, 3, d)
        assert depth == 1
        x = _layer(x, ada3, pos_tabs, norm_g[i], w_in[i], conv_w[i], conv_b[i], dt_bias[i], a_log[i],
                   d_skip[i], ssd_norm_g[i], w_br_attn[i], w_br_ssd[i], w_out[i], final_g[None])
    return x
```

```python
import functools

import numpy as np
import jax
import jax.numpy as jnp
from jax import lax
from jax.experimental import pallas as pl
from jax.experimental.pallas import tpu as pltpu

F32 = jnp.float32
BF16 = jnp.bfloat16

D_MODEL = 2048
HEAD_DIM = 128
ATTN_HEADS = 12
DILATIONS = (1, 4, 16)
N_SIDE = 64
ATTN_WIDTH = ATTN_HEADS * HEAD_DIM
ROPE_THETA = 10000.0
SSD_WIDTH = 2 * D_MODEL
SSD_HEADDIM = 64
SSD_GROUPS = 8
SSD_HEADS = SSD_WIDTH // SSD_HEADDIM
SSD_HPG = SSD_HEADS // SSD_GROUPS
SSD_STATE = 128
SSD_CHUNK = 128
CONV_WIDTH = 5
CONV_CH = SSD_WIDTH + 2 * SSD_GROUPS * SSD_STATE
EPS = 1e-6
QKV_COLS = 3 * len(DILATIONS) * ATTN_WIDTH
COL_Z = QKV_COLS
COL_XBC = COL_Z + ATTN_WIDTH + SSD_WIDTH
COL_DT = COL_XBC + CONV_CH
COL_GATE = COL_DT + 2 * SSD_HEADS
IN_COLS = COL_GATE + 2 * D_MODEL

LANES = 128
PROJ_TN = 512
V7X_VMEM_LIMIT_BYTES = 56 * 1024 * 1024


def _params(*sem):
    return pltpu.CompilerParams(dimension_semantics=sem, vmem_limit_bytes=V7X_VMEM_LIMIT_BYTES)


def _dot(a, b):
    return jnp.dot(a, b, preferred_element_type=F32)


def _sigmoid(x):
    return 1.0 / (1.0 + jnp.exp(-x))


def _split3(v):
    hi = v.astype(BF16)
    r = v - hi.astype(F32)
    mid = r.astype(BF16)
    lo = (r - mid.astype(F32)).astype(BF16)
    return hi, mid, lo


def _exact_lmul(t01, v):
    hi, mid, lo = _split3(v)
    return _dot(t01, hi) + _dot(t01, mid) + _dot(t01, lo)


def _exact_rmul(v, e01):
    hi, mid, lo = _split3(v)
    return _dot(hi, e01) + _dot(mid, e01) + _dot(lo, e01)


def _ada_body(c_ref, w_ref, b_ref, o_ref):
    o_ref[...] = _dot(c_ref[...].astype(BF16), w_ref[...].astype(BF16)) + b_ref[...]


def _ada(c, w, bias):
    bsz, d = c.shape
    n = w.shape[1]
    tn = 768
    return pl.pallas_call(
        _ada_body,
        grid=(n // tn,),
        in_specs=[pl.BlockSpec((bsz, d), lambda j: (0, 0)),
                  pl.BlockSpec((d, tn), lambda j: (0, j)),
                  pl.BlockSpec((1, tn), lambda j: (0, j))],
        out_specs=pl.BlockSpec((bsz, tn), lambda j: (0, j)),
        out_shape=jax.ShapeDtypeStruct((bsz, n), F32),
        compiler_params=_params("arbitrary"),
        name="ada",
    )(c, w, bias)


def _rope_body(pos_ref, inv_ref, cos_ref, sin_ref):
    s = pos_ref.shape[1]
    ang = pos_ref[0].astype(F32) * inv_ref[...]
    lane = lax.broadcasted_iota(jnp.int32, ang.shape, 1)
    sin = jnp.sin(ang)
    cos_ref[0, 0] = jnp.cos(ang)
    sin_ref[0, 0] = jnp.where(lane < HEAD_DIM // 2, -sin, sin)
    for tab in (cos_ref, sin_ref):
        for g, dil in enumerate(DILATIONS[1:], start=1):
            sub = s // dil
            for r in range(dil):
                tab[g, 0, r * sub:(r + 1) * sub, :] = tab[0, 0, pl.ds(r, sub, stride=dil), :]


def _rope_tables(pos, inv2):
    bsz, s, _ = pos.shape
    ng = len(DILATIONS)
    spec = pl.BlockSpec((ng, 1, s, HEAD_DIM), lambda b: (0, b, 0, 0))
    return pl.pallas_call(
        _rope_body,
        grid=(bsz,),
        in_specs=[pl.BlockSpec((1, s, 1), lambda b: (b, 0, 0)),
                  pl.BlockSpec((1, HEAD_DIM), lambda b: (0, 0))],
        out_specs=[spec, spec],
        out_shape=[jax.ShapeDtypeStruct((ng, bsz, s, HEAD_DIM), F32)] * 2,
        compiler_params=_params("arbitrary"),
        name="rope_tables",
    )(pos, inv2)


H_TILE = 512


def _h_body(x_ref, ada_ref, g_ref, h1_ref, h4_ref, h16_ref, hs_ref):
    x = x_ref[0]
    ms = jnp.mean(x * x, axis=-1, keepdims=True)
    xn = x * lax.rsqrt(ms + EPS) * g_ref[...]
    h = xn * (1.0 + ada_ref[0, 1:2, :]) + ada_ref[0, 0:1, :]
    h1_ref[0] = h.astype(BF16)
    for cb in range(h.shape[1] // LANES):
        cols = slice(cb * LANES, (cb + 1) * LANES)
        hs_ref[cb] = h[:, cols]
        for r in range(4):
            h4_ref[0, r, :, cols] = hs_ref[cb, pl.ds(r, H_TILE // 4, stride=4), :].astype(BF16)
        for r in range(16):
            h16_ref[0, r, :, cols] = hs_ref[cb, pl.ds(r, H_TILE // 16, stride=16), :].astype(BF16)


def _modulated_norm(x, ada3, g):
    bsz, s, d = x.shape
    t = H_TILE
    outs = pl.pallas_call(
        _h_body,
        grid=(bsz, s // t),
        in_specs=[pl.BlockSpec((1, t, d), lambda b, i: (b, i, 0)),
                  pl.BlockSpec((1, 3, d), lambda b, i: (b, 0, 0)),
                  pl.BlockSpec((1, d), lambda b, i: (0, 0))],
        out_specs=[pl.BlockSpec((1, t, d), lambda b, i: (b, i, 0)),
                   pl.BlockSpec((1, 4, t // 4, d), lambda b, i: (b, 0, i, 0)),
                   pl.BlockSpec((1, 16, t // 16, d), lambda b, i: (b, 0, i, 0))],
        out_shape=[jax.ShapeDtypeStruct((bsz, s, d), BF16),
                   jax.ShapeDtypeStruct((bsz, 4, s // 4, d), BF16),
                   jax.ShapeDtypeStruct((bsz, 16, s // 16, d), BF16)],
        scratch_shapes=[pltpu.VMEM((d // LANES, t, LANES), F32)],
        compiler_params=_params("arbitrary", "arbitrary"),
        name="modulated_norm",
    )(x, ada3, g)
    h1, h4, h16 = outs
    return h1, h4.reshape(bsz, s, d), h16.reshape(bsz, s, d)


PROJ_ROWS = 512


def _cast_weight_tile(w_ref, wbf_ref):
    @pl.when(pl.program_id(1) == 0)
    def _():
        wbf_ref[...] = w_ref[...].astype(BF16)


PROJ_SPLIT = (PROJ_ROWS,) * 4


def _proj_chunks(h_ref, wbf_ref, epilogue, sizes=PROJ_SPLIT):
    assert sum(sizes) == h_ref.shape[1]
    start = 0
    for size in sizes:
        rows = slice(start, start + size)
        epilogue(rows, _dot(h_ref[0, rows, :], wbf_ref[...]))
        start += size


def _proj_rope_body(h_ref, w_ref, cos_ref, sin_ref, o_ref, wbf_ref):
    _cast_weight_tile(w_ref, wbf_ref)
    j = pl.program_id(0)
    n_rot = 2 * ATTN_WIDTH // PROJ_TN

    def rotate(rows, acc):
        cos = cos_ref[0, rows, :]
        sin = sin_ref[0, rows, :]
        for hh in range(PROJ_TN // HEAD_DIM):
            t = acc[:, hh * HEAD_DIM:(hh + 1) * HEAD_DIM]
            rot = t * cos + pltpu.roll(t, HEAD_DIM // 2, 1) * sin
            o_ref[0, rows, hh * HEAD_DIM:(hh + 1) * HEAD_DIM] = rot.astype(BF16)

    def plain(rows, acc):
        o_ref[0, rows, :] = acc.astype(BF16)

    @pl.when(j < n_rot)
    def _():
        _proj_chunks(h_ref, wbf_ref, rotate)

    @pl.when(j >= n_rot)
    def _():
        _proj_chunks(h_ref, wbf_ref, plain)


def _proj_silu_body(h_ref, w_ref, o_ref, wbf_ref):
    _cast_weight_tile(w_ref, wbf_ref)

    def epilogue(rows, acc):
        o_ref[0, rows, :] = (acc * _sigmoid(acc)).astype(BF16)

    _proj_chunks(h_ref, wbf_ref, epilogue)


def _proj_sigmoid_body(h_ref, w_ref, o_ref, wbf_ref):
    _cast_weight_tile(w_ref, wbf_ref)

    def epilogue(rows, acc):
        o_ref[0, rows, :] = _sigmoid(acc).astype(BF16)

    _proj_chunks(h_ref, wbf_ref, epilogue)


CONV_PAD = 8
SUBLANES = 8
CONV_STRIDE = PROJ_ROWS // SUBLANES + 1
CONV_TAIL = 3 * SUBLANES
CONV_CHUNKS = 4


def _proj_conv_body(h_ref, w_ref, cw_ref, cb_ref, o_ref, wbf_ref, *scratch):
    pads, stages = scratch[:CONV_CHUNKS], scratch[CONV_CHUNKS:]
    n_cb = PROJ_TN // LANES
    half = (CONV_WIDTH - 1) // 2
    end = CONV_PAD + PROJ_ROWS
    _cast_weight_tile(w_ref, wbf_ref)
    pads[0][:, 0:CONV_PAD, :] = jnp.zeros((n_cb, CONV_PAD, LANES), F32)
    pads[-1][:, end:end + CONV_TAIL, :] = jnp.zeros((n_cb, CONV_TAIL, LANES), F32)

    def conv_chunk(m):
        base = m * PROJ_ROWS
        for cb in range(n_cb):
            cols = slice(cb * LANES, (cb + 1) * LANES)
            taps = [cw_ref[k:k + 1, cols] for k in range(CONV_WIDTH)]
            bias = cb_ref[:, cols]

            def tile(i):
                return pads[m][cb, pl.ds(CONV_PAD + i, SUBLANES, stride=CONV_STRIDE), :]

            win = [tile(i) for i in range(-half, half)]
            for i in range(CONV_STRIDE):
                win.append(tile(i + half))
                out = bias + taps[0] * win[0]
                for k in range(1, CONV_WIDTH):
                    out = out + taps[k] * win[k]
                stages[m][cb, pl.ds(i, SUBLANES, stride=CONV_STRIDE), :] = out * _sigmoid(out)
                win.pop(0)
            o_ref[0, base:base + PROJ_ROWS, cols] = stages[m][cb, 0:PROJ_ROWS, :].astype(BF16)

    def epilogue(rows, acc):
        m = rows.start // PROJ_ROWS
        for cb in range(n_cb):
            a = acc[:, cb * LANES:(cb + 1) * LANES]
            pads[m][cb, CONV_PAD:end, :] = a
            if m > 0:
                pads[m - 1][cb, end:end + CONV_TAIL, :] = a[0:CONV_TAIL]
            if m + 1 < CONV_CHUNKS:
                pads[m + 1][cb, 0:CONV_PAD, :] = a[PROJ_ROWS - CONV_PAD:PROJ_ROWS]
        if m > 0:
            conv_chunk(m - 1)

    _proj_chunks(h_ref, wbf_ref, epilogue, sizes=(PROJ_ROWS,) * CONV_CHUNKS)
    conv_chunk(CONV_CHUNKS - 1)


def _rep_masks(shape):
    lane = lax.broadcasted_iota(jnp.int32, shape, len(shape) - 1) & (SSD_HEADS - 1)
    return lane < SSD_HPG, lane < 2 * SSD_HPG


def _packed_split(v):
    hi, mid, lo = _split3(v)
    m0, m1 = _rep_masks(v.shape)
    return jnp.where(m0, hi, jnp.where(m1, mid, lo))


def _replicate_heads(rolled, g, m0, m1):
    return jnp.where(m0, rolled[g], jnp.where(m1, rolled[g - 1], rolled[g - 2]))


def _proj_dt_body(h_ref, w_ref, bias_ref, o_ref, ot_ref, wbf_ref):
    s = h_ref.shape[1]
    _cast_weight_tile(w_ref, wbf_ref)
    acc = _dot(h_ref[0], wbf_ref[...]) + bias_ref[...]
    sp = jnp.maximum(acc, 0.0) + jnp.log1p(jnp.exp(-jnp.abs(acc)))
    n = 2 * SSD_HEADS
    m0, m1 = _rep_masks(sp.shape)
    rolled = {k: (sp if k == 0 else pltpu.roll(sp, (-k * SSD_HPG) % n, 1))
              for k in range(-2, SSD_GROUPS)}
    for g in range(SSD_GROUPS):
        o_ref[0, :, g * n:(g + 1) * n] = _replicate_heads(rolled, g, m0, m1)
    for c in range(s // SSD_CHUNK):
        t = sp[c * SSD_CHUNK:(c + 1) * SSD_CHUNK, :].T
        for g in range(SSD_GROUPS):
            for d in range(2):
                r0 = d * SSD_HEADS + g * SSD_HPG
                ot_ref[0, g, d, c * SSD_HPG:(c + 1) * SSD_HPG, :] = t[r0:r0 + SSD_HPG, :]


def _proj_dt(h, w_in, dt_bias):
    bsz, s, d = h.shape
    n = 2 * SSD_HEADS
    nc = s // SSD_CHUNK
    blk0 = COL_DT // n
    return pl.pallas_call(
        _proj_dt_body,
        grid=(1, bsz),
        in_specs=[pl.BlockSpec((1, s, d), lambda j, b: (b, 0, 0)),
                  pl.BlockSpec((d, n), lambda j, b: (0, blk0)),
                  pl.BlockSpec((1, n), lambda j, b: (0, 0))],
        out_specs=[pl.BlockSpec((1, s, SSD_GROUPS * n), lambda j, b: (b, 0, 0)),
                   pl.BlockSpec((1, SSD_GROUPS, 2, nc * SSD_HPG, SSD_CHUNK), lambda j, b: (b, 0, 0, 0, 0))],
        out_shape=[jax.ShapeDtypeStruct((bsz, s, SSD_GROUPS * n), F32),
                   jax.ShapeDtypeStruct((bsz, SSD_GROUPS, 2, nc * SSD_HPG, SSD_CHUNK), F32)],
        scratch_shapes=[pltpu.VMEM((d, n), BF16)],
        compiler_params=_params("arbitrary", "arbitrary"),
        name="proj_dt",
    )(h, w_in, dt_bias.reshape(1, n))


def _proj_call(body, h, w_in, col0, n_tiles, tn, out_cols, out_dtype, extra=(), extra_specs=(),
               scratch=(), out_tn=None, name="proj"):
    bsz, s, d = h.shape
    out_tn = tn if out_tn is None else out_tn
    blk0 = col0 // tn
    assert blk0 * tn == col0
    return pl.pallas_call(
        body,
        grid=(n_tiles, bsz),
        in_specs=[pl.BlockSpec((1, s, d), lambda j, b: (b, 0, 0)),
                  pl.BlockSpec((d, tn), lambda j, b: (0, blk0 + j)),
                  *extra_specs],
        out_specs=pl.BlockSpec((1, s, out_tn), lambda j, b: (b, 0, j)),
        out_shape=jax.ShapeDtypeStruct((bsz, s, out_cols), out_dtype),
        scratch_shapes=[pltpu.VMEM((d, tn), BF16), *scratch],
        compiler_params=_params("arbitrary", "arbitrary"),
        name=name,
    )(h, w_in, *extra)


def _proj_qkv(h, w_in, cos, sin, group):
    bsz, s, d = h.shape
    tn = PROJ_TN
    tps = ATTN_WIDTH // tn
    n_groups = len(DILATIONS)

    def wcol(j, b):
        return (0, (j // tps) * (n_groups * tps) + group * tps + j % tps)

    tab = pl.BlockSpec((1, s, HEAD_DIM), lambda j, b: (group * bsz + b, 0, 0))
    return pl.pallas_call(
        _proj_rope_body,
        grid=(3 * tps, bsz),
        in_specs=[pl.BlockSpec((1, s, d), lambda j, b: (b, 0, 0)),
                  pl.BlockSpec((d, tn), wcol), tab, tab],
        out_specs=pl.BlockSpec((1, s, tn), lambda j, b: (b, 0, j)),
        out_shape=jax.ShapeDtypeStruct((bsz, s, 3 * ATTN_WIDTH), BF16),
        scratch_shapes=[pltpu.VMEM((d, tn), BF16)],
        compiler_params=_params("arbitrary", "arbitrary"),
        name=f"proj_qkv{group}",
    )(h, w_in, cos, sin)


ATT_BLK = 128


ATT_UNROLL = 16


def _attn_body(q1, k1, v1, q2, k2, v2, q3, k3, v3, z_ref, o_ref, on_ref, ls_ref, va_ref, bias_ref):
    s = o_ref.shape[1]
    scale = HEAD_DIM ** -0.5
    qi = lax.broadcasted_iota(jnp.int32, (ATT_BLK, 2 * ATT_BLK), 0)
    ki = lax.broadcasted_iota(jnp.int32, (ATT_BLK, 2 * ATT_BLK), 1)
    for t in range(3):
        bias_ref[t] = jnp.where(jnp.abs(qi - ki + t * N_SIDE) <= N_SIDE, 0.0, -jnp.inf)
    va_ref[:, :, HEAD_DIM:] = jnp.ones((len(DILATIONS), s, HEAD_DIM), BF16)

    def block(g, q_ref, k_ref, q0, k0, bias, nk):
        q = q_ref[0, pl.ds(q0, ATT_BLK), :]
        kw = k_ref[0, pl.ds(k0, nk), :]
        sc = lax.dot_general(q, kw, (((1,), (1,)), ((), ())), preferred_element_type=F32) * scale + bias
        m = jnp.max(sc, axis=-1, keepdims=True)
        p = jnp.exp(sc - m)
        pv = _dot(p.astype(BF16), va_ref[g, pl.ds(k0, nk), :])
        den = pv[:, HEAD_DIM:]
        return pv[:, :HEAD_DIM] / den, m + jnp.log(den)

    groups = ((q1, k1, v1), (q2, k2, v2), (q3, k3, v3))
    for g, (q_ref, k_ref, v_ref) in enumerate(groups):
        dil = DILATIONS[g]
        sub = s // dil
        nblk = sub // ATT_BLK
        va_ref[g, :, :HEAD_DIM] = v_ref[0]
        for r in range(dil):
            if nblk == 1:
                o, lse = block(g, q_ref, k_ref, r * sub, r * sub, bias_ref[0, :, :ATT_BLK], ATT_BLK)
                on_ref[g, pl.ds(r, ATT_BLK, stride=dil), :] = o
                ls_ref[g, pl.ds(r, ATT_BLK, stride=dil), :] = lse
            else:
                def one(i, q_ref=q_ref, k_ref=k_ref, r=r, sub=sub, dil=dil, g=g):
                    loc = i * ATT_BLK
                    kloc = jnp.clip(loc - N_SIDE, 0, sub - 2 * ATT_BLK)
                    q0 = pl.multiple_of(r * sub + loc, ATT_BLK)
                    k0 = pl.multiple_of(r * sub + kloc, N_SIDE)
                    bias = bias_ref[lax.shift_right_logical(loc - kloc, N_SIDE.bit_length() - 1)]
                    o, lse = block(g, q_ref, k_ref, q0, k0, bias, 2 * ATT_BLK)
                    if dil == 1:
                        rows = pl.ds(q0, ATT_BLK)
                    else:
                        rows = pl.ds(loc * dil + r, ATT_BLK, stride=dil)
                    on_ref[g, rows, :] = o
                    ls_ref[g, rows, :] = lse

                un = min(ATT_UNROLL, nblk)

                def step(io, carry, one=one, un=un):
                    for j in range(un):
                        one(io * un + j)
                    return carry

                lax.fori_loop(0, nblk // un, step, 0)

    l0, l1, l2 = ls_ref[0], ls_ref[1], ls_ref[2]
    mx = jnp.maximum(jnp.maximum(l0, l1), l2)
    e0, e1, e2 = jnp.exp(l0 - mx), jnp.exp(l1 - mx), jnp.exp(l2 - mx)
    o = (e0 * on_ref[0] + e1 * on_ref[1] + e2 * on_ref[2]) / (e0 + e1 + e2)
    o_ref[0] = (o * z_ref[0].astype(F32)).astype(BF16)


def _attention(qkv, zs):
    bsz, s, _ = qkv[0].shape
    specs = []
    args = []
    for g in range(len(DILATIONS)):
        for sec in range(3):
            specs.append(pl.BlockSpec((1, s, HEAD_DIM),
                                      lambda b, h, sec=sec: (b, 0, sec * ATTN_HEADS + h)))
            args.append(qkv[g])
    specs.append(pl.BlockSpec((1, s, HEAD_DIM), lambda b, h: (b, 0, h)))
    args.append(zs)
    return pl.pallas_call(
        _attn_body,
        grid=(bsz, ATTN_HEADS),
        in_specs=specs,
        out_specs=pl.BlockSpec((1, s, HEAD_DIM), lambda b, h: (b, 0, h)),
        out_shape=jax.ShapeDtypeStruct((bsz, s, ATTN_WIDTH), BF16),
        scratch_shapes=[pltpu.VMEM((3, s, HEAD_DIM), F32), pltpu.VMEM((3, s, HEAD_DIM), F32),
                        pltpu.VMEM((len(DILATIONS), s, 2 * HEAD_DIM), BF16),
                        pltpu.VMEM((3, ATT_BLK, 2 * ATT_BLK), F32)],
        compiler_params=_params("arbitrary", "arbitrary"),
        name="dilated_attention",
    )(*args)


SSD_UNROLL = 4
SSD_BIG = 1e30


def _ssd_constants():
    q, gw = SSD_CHUNK, SSD_HPG * SSD_HEADDIM
    i = np.arange(q)
    low = i[None, :] <= i[:, None]
    upp = i[None, :] >= i[:, None]
    tri = np.stack([low, upp])
    tri3 = np.stack([np.tile(upp, (3, 1)), np.tile(low, (3, 1))])
    c = np.arange(LANES)[:, None]
    in_f = c < 3 * SSD_HPG
    in_b = (c >= SSD_HEADS) & (c < SSD_HEADS + 3 * SSD_HPG)
    head = np.arange(gw)[None, :] // SSD_HEADDIM
    efb = np.concatenate([in_f & (c % SSD_HPG == head), in_b & (c % SSD_HPG == head)], axis=1)
    k = np.arange(2 * SSD_HPG * LANES)[None, :] // LANES
    ecol = np.where(k < SSD_HPG, in_f & (c % SSD_HPG == k), in_b & (c % SSD_HPG == k - SSD_HPG))
    src = np.arange(2 * SSD_HPG * LANES)[None, :] % LANES
    dst = i[:, None]
    keep = np.where(k < SSD_HPG, dst >= src, dst <= src)
    rhs0 = np.concatenate([ecol.astype(np.float32), np.where(keep, 0.0, -SSD_BIG)], axis=0)
    return tuple(jnp.asarray(m, BF16) for m in (tri, tri3, efb, rhs0))


SSD_DYN_ROW0 = 2 * SSD_HPG
SSD_ONE_LANE0 = 3 * SSD_HPG


def _ssd_body(x_ref, b_ref, c_ref, dt_ref, dtt_ref, alog_ref, alogt_ref, dskip_ref, z_ref,
              tri_ref, tri3_ref, efb_ref, rhs0_ref, y_ref,
              pcum_ref, eexp_ref, cdx_ref, stb_ref, prev_ref, rdyn_ref, rhs_ref):
    q = SSD_CHUNK
    s = x_ref.shape[1]
    nc = s // q
    gw = SSD_HPG * SSD_HEADDIM
    fwd_lane = lax.broadcasted_iota(jnp.int32, (q, LANES), 1) < SSD_HEADS
    lane = lax.broadcasted_iota(jnp.int32, (q, LANES), 1)

    a_row = -jnp.exp(alog_ref[0])
    dt_all = dt_ref[0]
    a_all = dt_all * a_row
    a_wide = jnp.concatenate([a_all[c * q:(c + 1) * q] for c in range(nc)], axis=1)
    fwd_wide = (lax.broadcasted_iota(jnp.int32, a_wide.shape, 1) & (LANES - 1)) < SSD_HEADS
    cum_wide = jnp.where(fwd_wide, _exact_lmul(tri_ref[0], a_wide), _exact_lmul(tri_ref[1], a_wide))

    dt_t = dtt_ref[0, 0]
    a_t = dt_t * -jnp.exp(alogt_ref[0])
    log_dt = jnp.where(dt_t > 0.0, jnp.log(dt_t), -SSD_BIG)
    neg_parts = []
    for d in range(2):
        parts = jnp.concatenate(_split3(a_t[d]), axis=1)
        src_term = _dot(parts, tri3_ref[d]) - log_dt[d]
        neg_parts.append([p.astype(F32) for p in _split3(-src_term)])
    sub = lax.broadcasted_iota(jnp.int32, (2 * SUBLANES, LANES), 0)
    for c in range(nc):
        blocks = []
        for k in range(2 * SSD_HPG):
            d, e = divmod(k, SSD_HPG)
            blk = jnp.where(sub == e, 1.0, 0.0) if d == 0 else jnp.zeros(sub.shape, F32)
            for p in range(3):
                r = c * SSD_HPG + e
                blk = jnp.where(sub == SUBLANES + p, neg_parts[d][p][r:r + 1, :], blk)
            blocks.append(blk)
        rdyn_ref[c] = jnp.concatenate(blocks, axis=1).astype(BF16)
    for j in range(SSD_UNROLL):
        rhs_ref[j] = rhs0_ref[...]

    bt = b_ref[0].T
    state_f = jnp.zeros((SSD_STATE, gw), F32)
    for c in range(nc):
        cum = cum_wide[:, c * LANES:(c + 1) * LANES]
        dt = dt_all[c * q:(c + 1) * q]
        ref = jnp.where(fwd_lane[0:1], cum[q - 1:q], cum[0:1])
        one_lane = (lane >= SSD_ONE_LANE0) & (lane < SSD_ONE_LANE0 + 3)
        pcum_ref[c] = jnp.where(one_lane, 1.0, _packed_split(cum).astype(F32)).astype(BF16)
        eexp_ref[c] = _dot(_packed_split(jnp.exp(cum)), efb_ref[...])
        wexp = _dot(_packed_split(dt * jnp.exp(ref - cum)), efb_ref[...])
        cdx = _dot(_packed_split(jnp.broadcast_to(jnp.exp(ref), (8, LANES))), efb_ref[...])[0:1]
        cdx_ref[c] = cdx
        xb = x_ref[0, c * q:(c + 1) * q, :].astype(F32)
        xw = (jnp.concatenate([xb, xb], axis=1) * wexp).astype(BF16)
        st = _dot(bt[:, c * q:(c + 1) * q], xw)
        stb_ref[c] = st[:, gw:]
        prev_ref[c, :, 0:gw] = state_f.astype(BF16)
        state_f = state_f * cdx[:, 0:gw] + st[:, 0:gw]
    state_b = jnp.zeros((SSD_STATE, gw), F32)
    for c in range(nc - 1, -1, -1):
        prev_ref[c, :, gw:] = state_b.astype(BF16)
        state_b = state_b * cdx_ref[c][:, gw:] + stb_ref[c]

    row = lax.broadcasted_iota(jnp.int32, (q, q), 0)
    col = lax.broadcasted_iota(jnp.int32, (q, q), 1)
    eye = jnp.where(row == col, 1.0, 0.0).astype(BF16)

    def chunk_out(c, slot):
        rows = pl.ds(pl.multiple_of(c * q, q), q)
        bc = b_ref[0, rows, :]
        cc = c_ref[0, rows, :]
        xb = x_ref[0, rows, :]
        cb = lax.dot_general(cc, bc, (((1,), (1,)), ((), ())), preferred_element_type=F32)
        rhs_ref[slot, SSD_DYN_ROW0:SSD_DYN_ROW0 + 2 * SUBLANES, :] = rdyn_ref[c]
        expo = _dot(jnp.concatenate([pcum_ref[c], eye], axis=1), rhs_ref[slot])
        ms = []
        for e in range(SSD_HPG):
            k = SSD_HPG + e
            w = jnp.exp(expo[:, e * LANES:(e + 1) * LANES]) + jnp.exp(expo[:, k * LANES:(k + 1) * LANES])
            ms.append((cb * w).astype(BF16))
        ys = []
        for p in range(SSD_HPG // 2):
            lhs = jnp.concatenate([ms[2 * p], ms[2 * p + 1]], axis=1)
            xp = xb[:, p * LANES:(p + 1) * LANES]
            zero = jnp.zeros_like(xp)
            rhs = jnp.concatenate([jnp.where(lane < SSD_HEADDIM, xp, zero),
                                   jnp.where(lane >= SSD_HEADDIM, xp, zero)], axis=0)
            ys.append(_dot(lhs, rhs))
        y = jnp.concatenate(ys, axis=1)
        yoff = _dot(cc, prev_ref[c]) * eexp_ref[c]
        y = y + yoff[:, 0:gw] + yoff[:, gw:]
        y = (y + dskip_ref[...] * xb.astype(F32)) * z_ref[0, rows, :].astype(F32)
        y_ref[0, rows, :] = y.astype(BF16)

    def chunk_step(io, carry):
        for j in range(SSD_UNROLL):
            chunk_out(io * SSD_UNROLL + j, j)
        return carry

    lax.fori_loop(0, nc // SSD_UNROLL, chunk_step, 0)


def _ssd(xbc, dtx, dtt, alog_g, alogt_g, dskip_x, zs):
    bsz, s, _ = xbc.shape
    q = SSD_CHUNK
    nc = s // q
    gw = SSD_HPG * SSD_HEADDIM
    nb = SSD_WIDTH // SSD_STATE
    zoff = ATTN_WIDTH // gw
    consts = _ssd_constants()
    const_specs = [pl.BlockSpec(m.shape, lambda b, g, nd=m.ndim: (0,) * nd) for m in consts]
    return pl.pallas_call(
        _ssd_body,
        grid=(bsz, SSD_GROUPS),
        in_specs=[pl.BlockSpec((1, s, gw), lambda b, g: (b, 0, g)),
                  pl.BlockSpec((1, s, SSD_STATE), lambda b, g: (b, 0, nb + g)),
                  pl.BlockSpec((1, s, SSD_STATE), lambda b, g: (b, 0, nb + SSD_GROUPS + g)),
                  pl.BlockSpec((1, s, LANES), lambda b, g: (b, 0, g)),
                  pl.BlockSpec((1, 1, 2, nc * SSD_HPG, q), lambda b, g: (b, g, 0, 0, 0)),
                  pl.BlockSpec((1, 1, LANES), lambda b, g: (g, 0, 0)),
                  pl.BlockSpec((1, 2, nc * SSD_HPG, q), lambda b, g: (g, 0, 0, 0)),
                  pl.BlockSpec((1, gw), lambda b, g: (0, g)),
                  pl.BlockSpec((1, s, gw), lambda b, g: (b, 0, zoff + g)),
                  *const_specs],
        out_specs=pl.BlockSpec((1, s, gw), lambda b, g: (b, 0, g)),
        out_shape=jax.ShapeDtypeStruct((bsz, s, SSD_WIDTH), BF16),
        scratch_shapes=[pltpu.VMEM((nc, q, LANES), BF16),
                        pltpu.VMEM((nc, q, 2 * gw), F32),
                        pltpu.VMEM((nc, 1, 2 * gw), F32),
                        pltpu.VMEM((nc, SSD_STATE, gw), F32),
                        pltpu.VMEM((nc, SSD_STATE, 2 * gw), BF16),
                        pltpu.VMEM((nc, 2 * SUBLANES, 2 * SSD_HPG * LANES), BF16),
                        pltpu.VMEM((SSD_UNROLL, 2 * q, 2 * SSD_HPG * LANES), BF16)],
        compiler_params=_params("arbitrary", "arbitrary"),
        name="ssd",
    )(xbc, xbc, xbc, dtx, dtt, alog_g, alogt_g, dskip_x, zs, *consts)


TAIL_TM = 512
TAIL_A_TM = 256


def _tail_a_body(ya_ref, y_ref, g_ref, ng_ref, wa_ref, ws_ref, o_ref):
    d = o_ref.shape[1]
    ya = _dot(ya_ref[...], wa_ref[...])
    y = y_ref[...].astype(F32)
    yn = y * lax.rsqrt(jnp.mean(y * y, axis=-1, keepdims=True) + EPS) * ng_ref[...]
    ys = _dot(yn.astype(BF16), ws_ref[...])
    ga = g_ref[:, 0:d].astype(F32)
    gs = g_ref[:, d:2 * d].astype(F32)
    o_ref[...] = (ga * ya + gs * ys).astype(BF16)


def _tail_a(ya_in, y, gates, norm_g, wa, ws):
    n, d = ya_in.shape[0], wa.shape[1]
    tm = TAIL_A_TM
    resident = functools.partial(pl.BlockSpec, pipeline_mode=pl.Buffered(1))
    return pl.pallas_call(
        _tail_a_body,
        grid=(n // tm,),
        in_specs=[pl.BlockSpec((tm, ya_in.shape[1]), lambda i: (i, 0)),
                  pl.BlockSpec((tm, y.shape[1]), lambda i: (i, 0)),
                  pl.BlockSpec((tm, gates.shape[1]), lambda i: (i, 0)),
                  pl.BlockSpec((1, y.shape[1]), lambda i: (0, 0)),
                  resident(wa.shape, lambda i: (0, 0)),
                  resident(ws.shape, lambda i: (0, 0))],
        out_specs=pl.BlockSpec((tm, d), lambda i: (i, 0)),
        out_shape=jax.ShapeDtypeStruct((n, d), BF16),
        compiler_params=_params("arbitrary"),
        name="tail_a",
    )(ya_in, y, gates, norm_g, wa, ws)


def _tail_b_body(m_ref, x_ref, ada_ref, w_ref, fg_ref, o_ref):
    t = _dot(m_ref[...], w_ref[...])
    xn = x_ref[...] + ada_ref[0, 2:3, :] * t
    o_ref[...] = xn * lax.rsqrt(jnp.mean(xn * xn, axis=-1, keepdims=True) + EPS) * fg_ref[...]


def _tail_b(merged, x2, ada3, w_out, final_g, seq):
    n, d = x2.shape
    tm = TAIL_TM
    per = seq // tm
    resident = functools.partial(pl.BlockSpec, pipeline_mode=pl.Buffered(1))
    return pl.pallas_call(
        _tail_b_body,
        grid=(n // tm,),
        in_specs=[pl.BlockSpec((tm, d), lambda i: (i, 0)),
                  pl.BlockSpec((tm, d), lambda i: (i, 0)),
                  pl.BlockSpec((1, 3, d), lambda i: (i // per, 0, 0)),
                  resident(w_out.shape, lambda i: (0, 0)),
                  pl.BlockSpec((1, d), lambda i: (0, 0))],
        out_specs=pl.BlockSpec((tm, d), lambda i: (i, 0)),
        out_shape=jax.ShapeDtypeStruct((n, d), F32),
        compiler_params=_params("arbitrary"),
        name="tail_b",
    )(merged, x2, ada3, w_out, final_g)


def _layer(x, ada3, pos_tabs, norm_g, w_in, conv_w, conv_b, dt_bias, a_log, d_skip, ssd_norm_g,
           w_br_attn, w_br_ssd, w_out, out_g):
    bsz, s, d = x.shape
    cos, sin = pos_tabs
    hs = _modulated_norm(x, ada3, norm_g[None])
    qkv = [_proj_qkv(hs[g], w_in, cos, sin, g) for g in range(len(DILATIONS))]
    h = hs[0]
    tn = PROJ_TN
    zs = _proj_call(_proj_silu_body, h, w_in, COL_Z, (ATTN_WIDTH + SSD_WIDTH) // tn, tn,
                    ATTN_WIDTH + SSD_WIDTH, BF16, name="proj_z")
    xbc = _proj_call(
        _proj_conv_body, h, w_in, COL_XBC, CONV_CH // tn, tn, CONV_CH, BF16,
        extra=(conv_w, conv_b[None]),
        extra_specs=(pl.BlockSpec((CONV_WIDTH, tn), lambda j, b: (0, j)),
                     pl.BlockSpec((1, tn), lambda j, b: (0, j))),
        scratch=(*[pltpu.VMEM((tn // LANES, CONV_PAD + PROJ_ROWS + CONV_TAIL, LANES), F32)] * CONV_CHUNKS,
                 *[pltpu.VMEM((tn // LANES, SUBLANES * CONV_STRIDE, LANES), F32)] * CONV_CHUNKS),
        name="proj_xbc")
    n_dt = 2 * SSD_HEADS
    dtx, dtt = _proj_dt(h, w_in, dt_bias)
    gates = _proj_call(_proj_sigmoid_body, h, w_in[:, COL_GATE:], 0, 2 * d // tn, tn, 2 * d,
                       BF16, name="proj_gates")

    ya_in = _attention(qkv, zs)

    a_flat = a_log.reshape(1, n_dt)
    rolled = {k: jnp.roll(a_flat, -k * SSD_HPG, axis=1) for k in range(-2, SSD_GROUPS)}
    m0, m1 = _rep_masks(a_flat.shape)
    alog_g = jnp.stack([_replicate_heads(rolled, g, m0, m1) for g in range(SSD_GROUPS)])
    nc = s // SSD_CHUNK
    alogt_g = a_log.reshape(2, SSD_GROUPS, SSD_HPG).transpose(1, 0, 2)
    alogt_g = jnp.broadcast_to(alogt_g[:, :, None, :, None], (SSD_GROUPS, 2, nc, SSD_HPG, SSD_CHUNK))
    alogt_g = alogt_g.reshape(SSD_GROUPS, 2, nc * SSD_HPG, SSD_CHUNK)
    dskip_x = jnp.repeat(d_skip, SSD_HEADDIM)[None]
    y = _ssd(xbc, dtx, dtt, alog_g, alogt_g, dskip_x, zs)

    n = bsz * s
    merged = _tail_a(ya_in.reshape(n, ATTN_WIDTH), y.reshape(n, SSD_WIDTH),
                     gates.reshape(n, gates.shape[-1]), ssd_norm_g[None],
                     w_br_attn.astype(BF16), w_br_ssd.astype(BF16))
    out = _tail_b(merged, x.reshape(n, d), ada3, w_out.astype(BF16), out_g, s)
    return out.reshape(bsz, s, d)


def kernel(x, c, positions, norm_g, w_ada, b_ada, w_in, conv_w, conv_b, dt_bias, a_log, d_skip,
           ssd_norm_g, w_br_attn, w_br_ssd, w_out, final_g):
    bsz, s, d = x.shape
    depth = w_in.shape[0]
    inv = ROPE_THETA ** (-jnp.arange(0, HEAD_DIM, 2, dtype=F32) / HEAD_DIM)
    inv2 = jnp.concatenate([inv, inv])[None]
    cos, sin = _rope_tables(positions[..., None], inv2)
    pos_tabs = (cos.reshape(-1, s, HEAD_DIM), sin.reshape(-1, s, HEAD_DIM))
    for i in range(depth):
        ada3 = _ada(c, w_ada[i], b_ada[i][None]).reshape(bsz, 3, d)
        assert depth == 1
        x = _layer(x, ada3, pos_tabs, norm_g[i], w_in[i], conv_w[i], conv_b[i], dt_bias[i], a_log[i],
                   d_skip[i], ssd_norm_g[i], w_br_attn[i], w_br_ssd[i], w_out[i], final_g[None])
    return x
```

```python
import functools

import numpy as np
import jax
import jax.numpy as jnp
from jax import lax
from jax.experimental import pallas as pl
from jax.experimental.pallas import tpu as pltpu

F32 = jnp.float32
BF16 = jnp.bfloat16

D_MODEL = 2048
HEAD_DIM = 128
ATTN_HEADS = 12
DILATIONS = (1, 4, 16)
N_SIDE = 64
ATTN_WIDTH = ATTN_HEADS * HEAD_DIM
ROPE_THETA = 10000.0
SSD_WIDTH = 2 * D_MODEL
SSD_HEADDIM = 64
SSD_GROUPS = 8
SSD_HEADS = SSD_WIDTH // SSD_HEADDIM
SSD_HPG = SSD_HEADS // SSD_GROUPS
SSD_STATE = 128
SSD_CHUNK = 128
CONV_WIDTH = 5
CONV_CH = SSD_WIDTH + 2 * SSD_GROUPS * SSD_STATE
EPS = 1e-6
QKV_COLS = 3 * len(DILATIONS) * ATTN_WIDTH
COL_Z = QKV_COLS
COL_XBC = COL_Z + ATTN_WIDTH + SSD_WIDTH
COL_DT = COL_XBC + CONV_CH
COL_GATE = COL_DT + 2 * SSD_HEADS
IN_COLS = COL_GATE + 2 * D_MODEL

LANES = 128
PROJ_TN = 512
V7X_VMEM_LIMIT_BYTES = 56 * 1024 * 1024


def _params(*sem):
    return pltpu.CompilerParams(dimension_semantics=sem, vmem_limit_bytes=V7X_VMEM_LIMIT_BYTES)


def _dot(a, b):
    return jnp.dot(a, b, preferred_element_type=F32)


def _sigmoid(x):
    return 1.0 / (1.0 + jnp.exp(-x))


def _split3(v):
    hi = v.astype(BF16)
    r = v - hi.astype(F32)
    mid = r.astype(BF16)
    lo = (r - mid.astype(F32)).astype(BF16)
    return hi, mid, lo


def _exact_lmul(t01, v):
    hi, mid, lo = _split3(v)
    return _dot(t01, hi) + _dot(t01, mid) + _dot(t01, lo)


def _exact_rmul(v, e01):
    hi, mid, lo = _split3(v)
    return _dot(hi, e01) + _dot(mid, e01) + _dot(lo, e01)


def _ada_body(c_ref, w_ref, b_ref, o_ref):
    o_ref[...] = _dot(c_ref[...].astype(BF16), w_ref[...].astype(BF16)) + b_ref[...]


def _ada(c, w, bias):
    bsz, d = c.shape
    n = w.shape[1]
    tn = 768
    return pl.pallas_call(
        _ada_body,
        grid=(n // tn,),
        in_specs=[pl.BlockSpec((bsz, d), lambda j: (0, 0)),
                  pl.BlockSpec((d, tn), lambda j: (0, j)),
                  pl.BlockSpec((1, tn), lambda j: (0, j))],
        out_specs=pl.BlockSpec((bsz, tn), lambda j: (0, j)),
        out_shape=jax.ShapeDtypeStruct((bsz, n), F32),
        compiler_params=_params("arbitrary"),
        name="ada",
    )(c, w, bias)


def _rope_body(pos_ref, inv_ref, cos_ref, sin_ref):
    s = pos_ref.shape[1]
    ang = pos_ref[0].astype(F32) * inv_ref[...]
    lane = lax.broadcasted_iota(jnp.int32, ang.shape, 1)
    sin = jnp.sin(ang)
    cos_ref[0, 0] = jnp.cos(ang)
    sin_ref[0, 0] = jnp.where(lane < HEAD_DIM // 2, -sin, sin)
    for tab in (cos_ref, sin_ref):
        for g, dil in enumerate(DILATIONS[1:], start=1):
            sub = s // dil
            for r in range(dil):
                tab[g, 0, r * sub:(r + 1) * sub, :] = tab[0, 0, pl.ds(r, sub, stride=dil), :]


def _rope_tables(pos, inv2):
    bsz, s, _ = pos.shape
    ng = len(DILATIONS)
    spec = pl.BlockSpec((ng, 1, s, HEAD_DIM), lambda b: (0, b, 0, 0))
    return pl.pallas_call(
        _rope_body,
        grid=(bsz,),
        in_specs=[pl.BlockSpec((1, s, 1), lambda b: (b, 0, 0)),
                  pl.BlockSpec((1, HEAD_DIM), lambda b: (0, 0))],
        out_specs=[spec, spec],
        out_shape=[jax.ShapeDtypeStruct((ng, bsz, s, HEAD_DIM), F32)] * 2,
        compiler_params=_params("arbitrary"),
        name="rope_tables",
    )(pos, inv2)


H_TILE = 512


def _h_body(x_ref, ada_ref, g_ref, h1_ref, h4_ref, h16_ref, hs_ref):
    x = x_ref[0]
    ms = jnp.mean(x * x, axis=-1, keepdims=True)
    xn = x * lax.rsqrt(ms + EPS) * g_ref[...]
    h = xn * (1.0 + ada_ref[0, 1:2, :]) + ada_ref[0, 0:1, :]
    h1_ref[0] = h.astype(BF16)
    for cb in range(h.shape[1] // LANES):
        cols = slice(cb * LANES, (cb + 1) * LANES)
        hs_ref[cb] = h[:, cols]
        for r in range(4):
            h4_ref[0, r, :, cols] = hs_ref[cb, pl.ds(r, H_TILE // 4, stride=4), :].astype(BF16)
        for r in range(16):
            h16_ref[0, r, :, cols] = hs_ref[cb, pl.ds(r, H_TILE // 16, stride=16), :].astype(BF16)


def _modulated_norm(x, ada3, g):
    bsz, s, d = x.shape
    t = H_TILE
    outs = pl.pallas_call(
        _h_body,
        grid=(bsz, s // t),
        in_specs=[pl.BlockSpec((1, t, d), lambda b, i: (b, i, 0)),
                  pl.BlockSpec((1, 3, d), lambda b, i: (b, 0, 0)),
                  pl.BlockSpec((1, d), lambda b, i: (0, 0))],
        out_specs=[pl.BlockSpec((1, t, d), lambda b, i: (b, i, 0)),
                   pl.BlockSpec((1, 4, t // 4, d), lambda b, i: (b, 0, i, 0)),
                   pl.BlockSpec((1, 16, t // 16, d), lambda b, i: (b, 0, i, 0))],
        out_shape=[jax.ShapeDtypeStruct((bsz, s, d), BF16),
                   jax.ShapeDtypeStruct((bsz, 4, s // 4, d), BF16),
                   jax.ShapeDtypeStruct((bsz, 16, s // 16, d), BF16)],
        scratch_shapes=[pltpu.VMEM((d // LANES, t, LANES), F32)],
        compiler_params=_params("arbitrary", "arbitrary"),
        name="modulated_norm",
    )(x, ada3, g)
    h1, h4, h16 = outs
    return h1, h4.reshape(bsz, s, d), h16.reshape(bsz, s, d)


PROJ_ROWS = 512


def _cast_weight_tile(w_ref, wbf_ref):
    @pl.when(pl.program_id(1) == 0)
    def _():
        wbf_ref[...] = w_ref[...].astype(BF16)


PROJ_SPLIT = (PROJ_ROWS,) * 4


def _proj_chunks(h_ref, wbf_ref, epilogue, sizes=PROJ_SPLIT):
    assert sum(sizes) == h_ref.shape[1]
    start = 0
    for size in sizes:
        rows = slice(start, start + size)
        epilogue(rows, _dot(h_ref[0, rows, :], wbf_ref[...]))
        start += size


def _proj_rope_body(h_ref, w_ref, cos_ref, sin_ref, o_ref, wbf_ref):
    _cast_weight_tile(w_ref, wbf_ref)
    j = pl.program_id(0)
    n_rot = 2 * ATTN_WIDTH // PROJ_TN

    def rotate(rows, acc):
        cos = cos_ref[0, rows, :]
        sin = sin_ref[0, rows, :]
        for hh in range(PROJ_TN // HEAD_DIM):
            t = acc[:, hh * HEAD_DIM:(hh + 1) * HEAD_DIM]
            rot = t * cos + pltpu.roll(t, HEAD_DIM // 2, 1) * sin
            o_ref[0, hh, rows, :] = rot.astype(BF16)

    def plain(rows, acc):
        for hh in range(PROJ_TN // HEAD_DIM):
            o_ref[0, hh, rows, :] = acc[:, hh * HEAD_DIM:(hh + 1) * HEAD_DIM].astype(BF16)

    @pl.when(j < n_rot)
    def _():
        _proj_chunks(h_ref, wbf_ref, rotate)

    @pl.when(j >= n_rot)
    def _():
        _proj_chunks(h_ref, wbf_ref, plain)


def _proj_silu_body(h_ref, w_ref, o_ref, wbf_ref):
    _cast_weight_tile(w_ref, wbf_ref)

    def epilogue(rows, acc):
        o_ref[0, rows, :] = (acc * _sigmoid(acc)).astype(BF16)

    _proj_chunks(h_ref, wbf_ref, epilogue)


def _proj_sigmoid_body(h_ref, w_ref, o_ref, wbf_ref):
    _cast_weight_tile(w_ref, wbf_ref)

    def epilogue(rows, acc):
        o_ref[0, rows, :] = _sigmoid(acc).astype(BF16)

    _proj_chunks(h_ref, wbf_ref, epilogue)


CONV_PAD = 8
SUBLANES = 8
CONV_ROWS = 256
CONV_STRIDE = CONV_ROWS // SUBLANES + 1
CONV_TAIL = 3 * SUBLANES
CONV_CHUNKS = 8


def _proj_conv_body(h_ref, w_ref, cw_ref, cb_ref, o_ref, wbf_ref, *scratch):
    pads, stages = scratch[:CONV_CHUNKS], scratch[CONV_CHUNKS:]
    n_cb = PROJ_TN // LANES
    half = (CONV_WIDTH - 1) // 2
    end = CONV_PAD + CONV_ROWS
    _cast_weight_tile(w_ref, wbf_ref)
    pads[0][:, 0:CONV_PAD, :] = jnp.zeros((n_cb, CONV_PAD, LANES), F32)
    pads[-1][:, end:end + CONV_TAIL, :] = jnp.zeros((n_cb, CONV_TAIL, LANES), F32)

    def conv_chunk(m):
        base = m * CONV_ROWS
        for cb in range(n_cb):
            cols = slice(cb * LANES, (cb + 1) * LANES)
            taps = [cw_ref[k:k + 1, cols] for k in range(CONV_WIDTH)]
            bias = cb_ref[:, cols]

            def tile(i):
                return pads[m][cb, pl.ds(CONV_PAD + i, SUBLANES, stride=CONV_STRIDE), :]

            win = [tile(i) for i in range(-half, half)]
            for i in range(CONV_STRIDE):
                win.append(tile(i + half))
                out = bias + taps[0] * win[0]
                for k in range(1, CONV_WIDTH):
                    out = out + taps[k] * win[k]
                stages[m][cb, pl.ds(i, SUBLANES, stride=CONV_STRIDE), :] = out * _sigmoid(out)
                win.pop(0)
            o_ref[0, base:base + CONV_ROWS, cols] = stages[m][cb, 0:CONV_ROWS, :].astype(BF16)

    def epilogue(rows, acc):
        m = rows.start // CONV_ROWS
        for cb in range(n_cb):
            a = acc[:, cb * LANES:(cb + 1) * LANES]
            pads[m][cb, CONV_PAD:end, :] = a
            if m > 0:
                pads[m - 1][cb, end:end + CONV_TAIL, :] = a[0:CONV_TAIL]
            if m + 1 < CONV_CHUNKS:
                pads[m + 1][cb, 0:CONV_PAD, :] = a[CONV_ROWS - CONV_PAD:CONV_ROWS]
        if m > 0:
            conv_chunk(m - 1)

    _proj_chunks(h_ref, wbf_ref, epilogue, sizes=(CONV_ROWS,) * CONV_CHUNKS)
    conv_chunk(CONV_CHUNKS - 1)


def _rep_masks(shape):
    lane = lax.broadcasted_iota(jnp.int32, shape, len(shape) - 1) & (SSD_HEADS - 1)
    return lane < SSD_HPG, lane < 2 * SSD_HPG


def _packed_split(v):
    hi, mid, lo = _split3(v)
    m0, m1 = _rep_masks(v.shape)
    return jnp.where(m0, hi, jnp.where(m1, mid, lo))


def _replicate_heads(rolled, g, m0, m1):
    return jnp.where(m0, rolled[g], jnp.where(m1, rolled[g - 1], rolled[g - 2]))


def _proj_dt_body(h_ref, w_ref, bias_ref, o_ref, ot_ref, wbf_ref):
    s = h_ref.shape[1]
    _cast_weight_tile(w_ref, wbf_ref)
    acc = _dot(h_ref[0], wbf_ref[...]) + bias_ref[...]
    sp = jnp.maximum(acc, 0.0) + jnp.log1p(jnp.exp(-jnp.abs(acc)))
    n = 2 * SSD_HEADS
    m0, m1 = _rep_masks(sp.shape)
    rolled = {k: (sp if k == 0 else pltpu.roll(sp, (-k * SSD_HPG) % n, 1))
              for k in range(-2, SSD_GROUPS)}
    for g in range(SSD_GROUPS):
        o_ref[0, :, g * n:(g + 1) * n] = _replicate_heads(rolled, g, m0, m1)
    for c in range(s // SSD_CHUNK):
        t = sp[c * SSD_CHUNK:(c + 1) * SSD_CHUNK, :].T
        for g in range(SSD_GROUPS):
            for d in range(2):
                r0 = d * SSD_HEADS + g * SSD_HPG
                ot_ref[0, g, d, c * SSD_HPG:(c + 1) * SSD_HPG, :] = t[r0:r0 + SSD_HPG, :]


def _proj_dt(h, w_in, dt_bias):
    bsz, s, d = h.shape
    n = 2 * SSD_HEADS
    nc = s // SSD_CHUNK
    blk0 = COL_DT // n
    return pl.pallas_call(
        _proj_dt_body,
        grid=(1, bsz),
        in_specs=[pl.BlockSpec((1, s, d), lambda j, b: (b, 0, 0)),
                  pl.BlockSpec((d, n), lambda j, b: (0, blk0)),
                  pl.BlockSpec((1, n), lambda j, b: (0, 0))],
        out_specs=[pl.BlockSpec((1, s, SSD_GROUPS * n), lambda j, b: (b, 0, 0)),
                   pl.BlockSpec((1, SSD_GROUPS, 2, nc * SSD_HPG, SSD_CHUNK), lambda j, b: (b, 0, 0, 0, 0))],
        out_shape=[jax.ShapeDtypeStruct((bsz, s, SSD_GROUPS * n), F32),
                   jax.ShapeDtypeStruct((bsz, SSD_GROUPS, 2, nc * SSD_HPG, SSD_CHUNK), F32)],
        scratch_shapes=[pltpu.VMEM((d, n), BF16)],
        compiler_params=_params("arbitrary", "arbitrary"),
        name="proj_dt",
    )(h, w_in, dt_bias.reshape(1, n))


def _proj_call(body, h, w_in, col0, n_tiles, tn, out_cols, out_dtype, extra=(), extra_specs=(),
               scratch=(), out_tn=None, name="proj"):
    bsz, s, d = h.shape
    out_tn = tn if out_tn is None else out_tn
    blk0 = col0 // tn
    assert blk0 * tn == col0
    return pl.pallas_call(
        body,
        grid=(n_tiles, bsz),
        in_specs=[pl.BlockSpec((1, s, d), lambda j, b: (b, 0, 0)),
                  pl.BlockSpec((d, tn), lambda j, b: (0, blk0 + j)),
                  *extra_specs],
        out_specs=pl.BlockSpec((1, s, out_tn), lambda j, b: (b, 0, j)),
        out_shape=jax.ShapeDtypeStruct((bsz, s, out_cols), out_dtype),
        scratch_shapes=[pltpu.VMEM((d, tn), BF16), *scratch],
        compiler_params=_params("arbitrary", "arbitrary"),
        name=name,
    )(h, w_in, *extra)


def _proj_qkv(h, w_in, cos, sin, group):
    bsz, s, d = h.shape
    tn = PROJ_TN
    tps = ATTN_WIDTH // tn
    n_groups = len(DILATIONS)

    def wcol(j, b):
        return (0, (j // tps) * (n_groups * tps) + group * tps + j % tps)

    tab = pl.BlockSpec((1, s, HEAD_DIM), lambda j, b: (group * bsz + b, 0, 0))
    return pl.pallas_call(
        _proj_rope_body,
        grid=(3 * tps, bsz),
        in_specs=[pl.BlockSpec((1, s, d), lambda j, b: (b, 0, 0)),
                  pl.BlockSpec((d, tn), wcol), tab, tab],
        out_specs=pl.BlockSpec((1, tn // HEAD_DIM, s, HEAD_DIM), lambda j, b: (b, j, 0, 0)),
        out_shape=jax.ShapeDtypeStruct((bsz, 3 * ATTN_HEADS, s, HEAD_DIM), BF16),
        scratch_shapes=[pltpu.VMEM((d, tn), BF16)],
        compiler_params=_params("arbitrary", "arbitrary"),
        name=f"proj_qkv{group}",
    )(h, w_in, cos, sin)


ATT_BLK = 128


ATT_UNROLL = 16


def _attn_body(q1, k1, v1, q2, k2, v2, q3, k3, v3, z_ref, o_ref, on_ref, ls_ref, va_ref, bias_ref):
    s = o_ref.shape[1]
    scale = HEAD_DIM ** -0.5
    qi = lax.broadcasted_iota(jnp.int32, (ATT_BLK, 2 * ATT_BLK), 0)
    ki = lax.broadcasted_iota(jnp.int32, (ATT_BLK, 2 * ATT_BLK), 1)
    for t in range(3):
        bias_ref[t] = jnp.where(jnp.abs(qi - ki + t * N_SIDE) <= N_SIDE, 0.0, -jnp.inf)
    va_ref[:, :, HEAD_DIM:] = jnp.ones((len(DILATIONS), s, HEAD_DIM), BF16)

    def block(g, q_ref, k_ref, q0, k0, bias, nk):
        q = q_ref[0, 0, pl.ds(q0, ATT_BLK), :]
        kw = k_ref[0, 0, pl.ds(k0, nk), :]
        sc = lax.dot_general(q, kw, (((1,), (1,)), ((), ())), preferred_element_type=F32) * scale + bias
        m = jnp.max(sc, axis=-1, keepdims=True)
        p = jnp.exp(sc - m)
        pv = _dot(p.astype(BF16), va_ref[g, pl.ds(k0, nk), :])
        den = pv[:, HEAD_DIM:]
        return pv[:, :HEAD_DIM] / den, m + jnp.log(den)

    groups = ((q1, k1, v1), (q2, k2, v2), (q3, k3, v3))
    for g, (q_ref, k_ref, v_ref) in enumerate(groups):
        dil = DILATIONS[g]
        sub = s // dil
        nblk = sub // ATT_BLK
        va_ref[g, :, :HEAD_DIM] = v_ref[0, 0]
        for r in range(dil):
            if nblk == 1:
                o, lse = block(g, q_ref, k_ref, r * sub, r * sub, bias_ref[0, :, :ATT_BLK], ATT_BLK)
                on_ref[g, pl.ds(r, ATT_BLK, stride=dil), :] = o
                ls_ref[g, pl.ds(r, ATT_BLK, stride=dil), :] = lse
            else:
                def one(i, q_ref=q_ref, k_ref=k_ref, r=r, sub=sub, dil=dil, g=g):
                    loc = i * ATT_BLK
                    kloc = jnp.clip(loc - N_SIDE, 0, sub - 2 * ATT_BLK)
                    q0 = pl.multiple_of(r * sub + loc, ATT_BLK)
                    k0 = pl.multiple_of(r * sub + kloc, N_SIDE)
                    bias = bias_ref[lax.shift_right_logical(loc - kloc, N_SIDE.bit_length() - 1)]
                    o, lse = block(g, q_ref, k_ref, q0, k0, bias, 2 * ATT_BLK)
                    if dil == 1:
                        rows = pl.ds(q0, ATT_BLK)
                    else:
                        rows = pl.ds(loc * dil + r, ATT_BLK, stride=dil)
                    on_ref[g, rows, :] = o
                    ls_ref[g, rows, :] = lse

                un = min(ATT_UNROLL, nblk)

                def step(io, carry, one=one, un=un):
                    for j in range(un):
                        one(io * un + j)
                    return carry

                lax.fori_loop(0, nblk // un, step, 0)

    l0, l1, l2 = ls_ref[0], ls_ref[1], ls_ref[2]
    mx = jnp.maximum(jnp.maximum(l0, l1), l2)
    e0, e1, e2 = jnp.exp(l0 - mx), jnp.exp(l1 - mx), jnp.exp(l2 - mx)
    o = (e0 * on_ref[0] + e1 * on_ref[1] + e2 * on_ref[2]) / (e0 + e1 + e2)
    o_ref[0] = (o * z_ref[0].astype(F32)).astype(BF16)


def _attention(qkv, zs):
    bsz, _, s, _ = qkv[0].shape
    specs = []
    args = []
    for g in range(len(DILATIONS)):
        for sec in range(3):
            specs.append(pl.BlockSpec((1, 1, s, HEAD_DIM),
                                      lambda b, h, sec=sec: (b, sec * ATTN_HEADS + h, 0, 0)))
            args.append(qkv[g])
    specs.append(pl.BlockSpec((1, s, HEAD_DIM), lambda b, h: (b, 0, h)))
    args.append(zs)
    return pl.pallas_call(
        _attn_body,
        grid=(bsz, ATTN_HEADS),
        in_specs=specs,
        out_specs=pl.BlockSpec((1, s, HEAD_DIM), lambda b, h: (b, 0, h)),
        out_shape=jax.ShapeDtypeStruct((bsz, s, ATTN_WIDTH), BF16),
        scratch_shapes=[pltpu.VMEM((3, s, HEAD_DIM), F32), pltpu.VMEM((3, s, HEAD_DIM), F32),
                        pltpu.VMEM((len(DILATIONS), s, 2 * HEAD_DIM), BF16),
                        pltpu.VMEM((3, ATT_BLK, 2 * ATT_BLK), F32)],
        compiler_params=_params("arbitrary", "arbitrary"),
        name="dilated_attention",
    )(*args)


SSD_UNROLL = 4
SSD_BIG = 1e30


def _ssd_constants():
    q, gw = SSD_CHUNK, SSD_HPG * SSD_HEADDIM
    i = np.arange(q)
    low = i[None, :] <= i[:, None]
    upp = i[None, :] >= i[:, None]
    tri = np.stack([low, upp])
    tri3 = np.stack([np.tile(upp, (3, 1)), np.tile(low, (3, 1))])
    c = np.arange(LANES)[:, None]
    in_f = c < 3 * SSD_HPG
    in_b = (c >= SSD_HEADS) & (c < SSD_HEADS + 3 * SSD_HPG)
    head = np.arange(gw)[None, :] // SSD_HEADDIM
    efb = np.concatenate([in_f & (c % SSD_HPG == head), in_b & (c % SSD_HPG == head)], axis=1)
    k = np.arange(2 * SSD_HPG * LANES)[None, :] // LANES
    ecol = np.where(k < SSD_HPG, in_f & (c % SSD_HPG == k), in_b & (c % SSD_HPG == k - SSD_HPG))
    src = np.arange(2 * SSD_HPG * LANES)[None, :] % LANES
    dst = i[:, None]
    keep = np.where(k < SSD_HPG, dst >= src, dst <= src)
    rhs0 = np.concatenate([ecol.astype(np.float32), np.where(keep, 0.0, -SSD_BIG)], axis=0)
    return tuple(jnp.asarray(m, BF16) for m in (tri, tri3, efb, rhs0))


SSD_DYN_ROW0 = 2 * SSD_HPG
SSD_ONE_LANE0 = 3 * SSD_HPG


def _ssd_body(x_ref, b_ref, c_ref, dt_ref, dtt_ref, alog_ref, alogt_ref, dskip_ref, z_ref,
              tri_ref, tri3_ref, efb_ref, rhs0_ref, y_ref,
              pcum_ref, eexp_ref, cdx_ref, stb_ref, prev_ref, rdyn_ref, rhs_ref):
    q = SSD_CHUNK
    s = x_ref.shape[1]
    nc = s // q
    gw = SSD_HPG * SSD_HEADDIM
    fwd_lane = lax.broadcasted_iota(jnp.int32, (q, LANES), 1) < SSD_HEADS
    lane = lax.broadcasted_iota(jnp.int32, (q, LANES), 1)

    a_row = -jnp.exp(alog_ref[0])
    dt_all = dt_ref[0]
    a_all = dt_all * a_row
    a_wide = jnp.concatenate([a_all[c * q:(c + 1) * q] for c in range(nc)], axis=1)
    fwd_wide = (lax.broadcasted_iota(jnp.int32, a_wide.shape, 1) & (LANES - 1)) < SSD_HEADS
    both = jnp.concatenate([tri_ref[0], tri_ref[1]], axis=1)
    cum_wide = jnp.zeros(a_wide.shape, F32)
    for part in _split3(a_wide):
        zero = jnp.zeros_like(part)
        stacked = jnp.concatenate([jnp.where(fwd_wide, part, zero), jnp.where(fwd_wide, zero, part)], axis=0)
        cum_wide = cum_wide + _dot(both, stacked)

    dt_t = dtt_ref[0, 0]
    a_t = dt_t * -jnp.exp(alogt_ref[0])
    log_dt = jnp.where(dt_t > 0.0, jnp.log(dt_t), -SSD_BIG)
    neg_parts = []
    for d in range(2):
        parts = jnp.concatenate(_split3(a_t[d]), axis=1)
        src_term = _dot(parts, tri3_ref[d]) - log_dt[d]
        neg_parts.append([p.astype(F32) for p in _split3(-src_term)])
    sub = lax.broadcasted_iota(jnp.int32, (2 * SUBLANES, LANES), 0)
    for c in range(nc):
        blocks = []
        for k in range(2 * SSD_HPG):
            d, e = divmod(k, SSD_HPG)
            blk = jnp.where(sub == e, 1.0, 0.0) if d == 0 else jnp.zeros(sub.shape, F32)
            for p in range(3):
                r = c * SSD_HPG + e
                blk = jnp.where(sub == SUBLANES + p, neg_parts[d][p][r:r + 1, :], blk)
            blocks.append(blk)
        rdyn_ref[c] = jnp.concatenate(blocks, axis=1).astype(BF16)
    for j in range(SSD_UNROLL):
        rhs_ref[j] = rhs0_ref[...]

    bt = b_ref[0].T
    state_f = jnp.zeros((SSD_STATE, gw), F32)
    for c in range(nc):
        cum = cum_wide[:, c * LANES:(c + 1) * LANES]
        dt = dt_all[c * q:(c + 1) * q]
        ref = jnp.where(fwd_lane[0:1], cum[q - 1:q], cum[0:1])
        one_lane = (lane >= SSD_ONE_LANE0) & (lane < SSD_ONE_LANE0 + 3)
        pcum_ref[c] = jnp.where(one_lane, 1.0, _packed_split(cum).astype(F32)).astype(BF16)
        eexp_ref[c] = _dot(_packed_split(jnp.exp(cum)), efb_ref[...])
        wexp = _dot(_packed_split(dt * jnp.exp(ref - cum)), efb_ref[...])
        cdx = _dot(_packed_split(jnp.broadcast_to(jnp.exp(ref), (8, LANES))), efb_ref[...])[0:1]
        cdx_ref[c] = cdx
        xb = x_ref[0, c * q:(c + 1) * q, :].astype(F32)
        xw = (jnp.concatenate([xb, xb], axis=1) * wexp).astype(BF16)
        st = _dot(bt[:, c * q:(c + 1) * q], xw)
        stb_ref[c] = st[:, gw:]
        prev_ref[c, :, 0:gw] = state_f.astype(BF16)
        state_f = state_f * cdx[:, 0:gw] + st[:, 0:gw]
    state_b = jnp.zeros((SSD_STATE, gw), F32)
    for c in range(nc - 1, -1, -1):
        prev_ref[c, :, gw:] = state_b.astype(BF16)
        state_b = state_b * cdx_ref[c][:, gw:] + stb_ref[c]

    row = lax.broadcasted_iota(jnp.int32, (q, q), 0)
    col = lax.broadcasted_iota(jnp.int32, (q, q), 1)
    eye = jnp.where(row == col, 1.0, 0.0).astype(BF16)

    def chunk_out(c, slot):
        rows = pl.ds(pl.multiple_of(c * q, q), q)
        bc = b_ref[0, rows, :]
        cc = c_ref[0, rows, :]
        xb = x_ref[0, rows, :]
        cb = lax.dot_general(cc, bc, (((1,), (1,)), ((), ())), preferred_element_type=F32)
        rhs_ref[slot, SSD_DYN_ROW0:SSD_DYN_ROW0 + 2 * SUBLANES, :] = rdyn_ref[c]
        expo = _dot(jnp.concatenate([pcum_ref[c], eye], axis=1), rhs_ref[slot])
        ms = []
        for e in range(SSD_HPG):
            k = SSD_HPG + e
            w = jnp.exp(expo[:, e * LANES:(e + 1) * LANES]) + jnp.exp(expo[:, k * LANES:(k + 1) * LANES])
            ms.append((cb * w).astype(BF16))
        ys = []
        for p in range(SSD_HPG // 2):
            lhs = jnp.concatenate([ms[2 * p], ms[2 * p + 1]], axis=1)
            xp = xb[:, p * LANES:(p + 1) * LANES]
            zero = jnp.zeros_like(xp)
            rhs = jnp.concatenate([jnp.where(lane < SSD_HEADDIM, xp, zero),
                                   jnp.where(lane >= SSD_HEADDIM, xp, zero)], axis=0)
            ys.append(_dot(lhs, rhs))
        y = jnp.concatenate(ys, axis=1)
        yoff = _dot(cc, prev_ref[c]) * eexp_ref[c]
        y = y + yoff[:, 0:gw] + yoff[:, gw:]
        y = (y + dskip_ref[...] * xb.astype(F32)) * z_ref[0, rows, :].astype(F32)
        y_ref[0, rows, :] = y.astype(BF16)

    def chunk_step(io, carry):
        for j in range(SSD_UNROLL):
            chunk_out(io * SSD_UNROLL + j, j)
        return carry

    lax.fori_loop(0, nc // SSD_UNROLL, chunk_step, 0)


def _ssd(xbc, dtx, dtt, alog_g, alogt_g, dskip_x, zs):
    bsz, s, _ = xbc.shape
    q = SSD_CHUNK
    nc = s // q
    gw = SSD_HPG * SSD_HEADDIM
    nb = SSD_WIDTH // SSD_STATE
    zoff = ATTN_WIDTH // gw
    consts = _ssd_constants()
    const_specs = [pl.BlockSpec(m.shape, lambda b, g, nd=m.ndim: (0,) * nd) for m in consts]
    return pl.pallas_call(
        _ssd_body,
        grid=(bsz, SSD_GROUPS),
        in_specs=[pl.BlockSpec((1, s, gw), lambda b, g: (b, 0, g)),
                  pl.BlockSpec((1, s, SSD_STATE), lambda b, g: (b, 0, nb + g)),
                  pl.BlockSpec((1, s, SSD_STATE), lambda b, g: (b, 0, nb + SSD_GROUPS + g)),
                  pl.BlockSpec((1, s, LANES), lambda b, g: (b, 0, g)),
                  pl.BlockSpec((1, 1, 2, nc * SSD_HPG, q), lambda b, g: (b, g, 0, 0, 0)),
                  pl.BlockSpec((1, 1, LANES), lambda b, g: (g, 0, 0)),
                  pl.BlockSpec((1, 2, nc * SSD_HPG, q), lambda b, g: (g, 0, 0, 0)),
                  pl.BlockSpec((1, gw), lambda b, g: (0, g)),
                  pl.BlockSpec((1, s, gw), lambda b, g: (b, 0, zoff + g)),
                  *const_specs],
        out_specs=pl.BlockSpec((1, s, gw), lambda b, g: (b, 0, g)),
        out_shape=jax.ShapeDtypeStruct((bsz, s, SSD_WIDTH), BF16),
        scratch_shapes=[pltpu.VMEM((nc, q, LANES), BF16),
                        pltpu.VMEM((nc, q, 2 * gw), F32),
                        pltpu.VMEM((nc, 1, 2 * gw), F32),
                        pltpu.VMEM((nc, SSD_STATE, gw), F32),
                        pltpu.VMEM((nc, SSD_STATE, 2 * gw), BF16),
                        pltpu.VMEM((nc, 2 * SUBLANES, 2 * SSD_HPG * LANES), BF16),
                        pltpu.VMEM((SSD_UNROLL, 2 * q, 2 * SSD_HPG * LANES), BF16)],
        compiler_params=_params("arbitrary", "arbitrary"),
        name="ssd",
    )(xbc, xbc, xbc, dtx, dtt, alog_g, alogt_g, dskip_x, zs, *consts)


TAIL_TM = 512
TAIL_A_TM = 256


def _tail_a_body(ya_ref, y_ref, g_ref, ng_ref, wa_ref, ws_ref, o_ref):
    d = o_ref.shape[1]
    ya = _dot(ya_ref[...], wa_ref[...])
    y = y_ref[...].astype(F32)
    yn = y * lax.rsqrt(jnp.mean(y * y, axis=-1, keepdims=True) + EPS) * ng_ref[...]
    ys = _dot(yn.astype(BF16), ws_ref[...])
    ga = g_ref[:, 0:d].astype(F32)
    gs = g_ref[:, d:2 * d].astype(F32)
    o_ref[...] = (ga * ya + gs * ys).astype(BF16)


def _tail_a(ya_in, y, gates, norm_g, wa, ws):
    n, d = ya_in.shape[0], wa.shape[1]
    tm = TAIL_A_TM
    resident = functools.partial(pl.BlockSpec, pipeline_mode=pl.Buffered(1))
    return pl.pallas_call(
        _tail_a_body,
        grid=(n // tm,),
        in_specs=[pl.BlockSpec((tm, ya_in.shape[1]), lambda i: (i, 0)),
                  pl.BlockSpec((tm, y.shape[1]), lambda i: (i, 0)),
                  pl.BlockSpec((tm, gates.shape[1]), lambda i: (i, 0)),
                  pl.BlockSpec((1, y.shape[1]), lambda i: (0, 0)),
                  resident(wa.shape, lambda i: (0, 0)),
                  resident(ws.shape, lambda i: (0, 0))],
        out_specs=pl.BlockSpec((tm, d), lambda i: (i, 0)),
        out_shape=jax.ShapeDtypeStruct((n, d), BF16),
        compiler_params=_params("arbitrary"),
        name="tail_a",
    )(ya_in, y, gates, norm_g, wa, ws)


def _tail_b_body(m_ref, x_ref, ada_ref, w_ref, fg_ref, o_ref):
    t = _dot(m_ref[...], w_ref[...])
    xn = x_ref[...] + ada_ref[0, 2:3, :] * t
    o_ref[...] = xn * lax.rsqrt(jnp.mean(xn * xn, axis=-1, keepdims=True) + EPS) * fg_ref[...]


def _tail_b(merged, x2, ada3, w_out, final_g, seq):
    n, d = x2.shape
    tm = TAIL_TM
    per = seq // tm
    resident = functools.partial(pl.BlockSpec, pipeline_mode=pl.Buffered(1))
    return pl.pallas_call(
        _tail_b_body,
        grid=(n // tm,),
        in_specs=[pl.BlockSpec((tm, d), lambda i: (i, 0)),
                  pl.BlockSpec((tm, d), lambda i: (i, 0)),
                  pl.BlockSpec((1, 3, d), lambda i: (i // per, 0, 0)),
                  resident(w_out.shape, lambda i: (0, 0)),
                  pl.BlockSpec((1, d), lambda i: (0, 0))],
        out_specs=pl.BlockSpec((tm, d), lambda i: (i, 0)),
        out_shape=jax.ShapeDtypeStruct((n, d), F32),
        compiler_params=_params("arbitrary"),
        name="tail_b",
    )(merged, x2, ada3, w_out, final_g)


def _layer(x, ada3, pos_tabs, norm_g, w_in, conv_w, conv_b, dt_bias, a_log, d_skip, ssd_norm_g,
           w_br_attn, w_br_ssd, w_out, out_g):
    bsz, s, d = x.shape
    cos, sin = pos_tabs
    hs = _modulated_norm(x, ada3, norm_g[None])
    qkv = [_proj_qkv(hs[g], w_in, cos, sin, g) for g in range(len(DILATIONS))]
    h = hs[0]
    tn = PROJ_TN
    zs = _proj_call(_proj_silu_body, h, w_in, COL_Z, (ATTN_WIDTH + SSD_WIDTH) // tn, tn,
                    ATTN_WIDTH + SSD_WIDTH, BF16, name="proj_z")
    xbc = _proj_call(
        _proj_conv_body, h, w_in, COL_XBC, CONV_CH // tn, tn, CONV_CH, BF16,
        extra=(conv_w, conv_b[None]),
        extra_specs=(pl.BlockSpec((CONV_WIDTH, tn), lambda j, b: (0, j)),
                     pl.BlockSpec((1, tn), lambda j, b: (0, j))),
        scratch=(*[pltpu.VMEM((tn // LANES, CONV_PAD + CONV_ROWS + CONV_TAIL, LANES), F32)] * CONV_CHUNKS,
                 *[pltpu.VMEM((tn // LANES, SUBLANES * CONV_STRIDE, LANES), F32)] * CONV_CHUNKS),
        name="proj_xbc")
    n_dt = 2 * SSD_HEADS
    dtx, dtt = _proj_dt(h, w_in, dt_bias)
    gates = _proj_call(_proj_sigmoid_body, h, w_in[:, COL_GATE:], 0, 2 * d // tn, tn, 2 * d,
                       BF16, name="proj_gates")

    ya_in = _attention(qkv, zs)

    a_flat = a_log.reshape(1, n_dt)
    rolled = {k: jnp.roll(a_flat, -k * SSD_HPG, axis=1) for k in range(-2, SSD_GROUPS)}
    m0, m1 = _rep_masks(a_flat.shape)
    alog_g = jnp.stack([_replicate_heads(rolled, g, m0, m1) for g in range(SSD_GROUPS)])
    nc = s // SSD_CHUNK
    alogt_g = a_log.reshape(2, SSD_GROUPS, SSD_HPG).transpose(1, 0, 2)
    alogt_g = jnp.broadcast_to(alogt_g[:, :, None, :, None], (SSD_GROUPS, 2, nc, SSD_HPG, SSD_CHUNK))
    alogt_g = alogt_g.reshape(SSD_GROUPS, 2, nc * SSD_HPG, SSD_CHUNK)
    dskip_x = jnp.repeat(d_skip, SSD_HEADDIM)[None]
    y = _ssd(xbc, dtx, dtt, alog_g, alogt_g, dskip_x, zs)

    n = bsz * s
    merged = _tail_a(ya_in.reshape(n, ATTN_WIDTH), y.reshape(n, SSD_WIDTH),
                     gates.reshape(n, gates.shape[-1]), ssd_norm_g[None],
                     w_br_attn.astype(BF16), w_br_ssd.astype(BF16))
    out = _tail_b(merged, x.reshape(n, d), ada3, w_out.astype(BF16), out_g, s)
    return out.reshape(bsz, s, d)


def kernel(x, c, positions, norm_g, w_ada, b_ada, w_in, conv_w, conv_b, dt_bias, a_log, d_skip,
           ssd_norm_g, w_br_attn, w_br_ssd, w_out, final_g):
    bsz, s, d = x.shape
    depth = w_in.shape[0]
    inv = ROPE_THETA ** (-jnp.arange(0, HEAD_DIM, 2, dtype=F32) / HEAD_DIM)
    inv2 = jnp.concatenate([inv, inv])[None]
    cos, sin = _rope_tables(positions[..., None], inv2)
    pos_tabs = (cos.reshape(-1, s, HEAD_DIM), sin.reshape(-1, s, HEAD_DIM))
    for i in range(depth):
        ada3 = _ada(c, w_ada[i], b_ada[i][None]).reshape(bsz, 3, d)
        assert depth == 1
        x = _layer(x, ada3, pos_tabs, norm_g[i], w_in[i], conv_w[i], conv_b[i], dt_bias[i], a_log[i],
                   d_skip[i], ssd_norm_g[i], w_br_attn[i], w_br_ssd[i], w_out[i], final_g[None])
    return x
```

```python
import functools

import numpy as np
import jax
import jax.numpy as jnp
from jax import lax
from jax.experimental import pallas as pl
from jax.experimental.pallas import tpu as pltpu

F32 = jnp.float32
BF16 = jnp.bfloat16

D_MODEL = 2048
HEAD_DIM = 128
ATTN_HEADS = 12
DILATIONS = (1, 4, 16)
N_SIDE = 64
ATTN_WIDTH = ATTN_HEADS * HEAD_DIM
ROPE_THETA = 10000.0
SSD_WIDTH = 2 * D_MODEL
SSD_HEADDIM = 64
SSD_GROUPS = 8
SSD_HEADS = SSD_WIDTH // SSD_HEADDIM
SSD_HPG = SSD_HEADS // SSD_GROUPS
SSD_STATE = 128
SSD_CHUNK = 128
CONV_WIDTH = 5
CONV_CH = SSD_WIDTH + 2 * SSD_GROUPS * SSD_STATE
EPS = 1e-6
QKV_COLS = 3 * len(DILATIONS) * ATTN_WIDTH
COL_Z = QKV_COLS
COL_XBC = COL_Z + ATTN_WIDTH + SSD_WIDTH
COL_DT = COL_XBC + CONV_CH
COL_GATE = COL_DT + 2 * SSD_HEADS
IN_COLS = COL_GATE + 2 * D_MODEL

LANES = 128
PROJ_TN = 512
V7X_VMEM_LIMIT_BYTES = 56 * 1024 * 1024


def _params(*sem):
    return pltpu.CompilerParams(dimension_semantics=sem, vmem_limit_bytes=V7X_VMEM_LIMIT_BYTES)


def _dot(a, b):
    return jnp.dot(a, b, preferred_element_type=F32)


def _sigmoid(x):
    return 1.0 / (1.0 + jnp.exp(-x))


def _split3(v):
    hi = v.astype(BF16)
    r = v - hi.astype(F32)
    mid = r.astype(BF16)
    lo = (r - mid.astype(F32)).astype(BF16)
    return hi, mid, lo


def _exact_lmul(t01, v):
    hi, mid, lo = _split3(v)
    return _dot(t01, hi) + _dot(t01, mid) + _dot(t01, lo)


def _exact_rmul(v, e01):
    hi, mid, lo = _split3(v)
    return _dot(hi, e01) + _dot(mid, e01) + _dot(lo, e01)


def _ada_body(c_ref, w_ref, b_ref, o_ref):
    o_ref[...] = _dot(c_ref[...].astype(BF16), w_ref[...].astype(BF16)) + b_ref[...]


def _ada(c, w, bias):
    bsz, d = c.shape
    n = w.shape[1]
    tn = 768
    return pl.pallas_call(
        _ada_body,
        grid=(n // tn,),
        in_specs=[pl.BlockSpec((bsz, d), lambda j: (0, 0)),
                  pl.BlockSpec((d, tn), lambda j: (0, j)),
                  pl.BlockSpec((1, tn), lambda j: (0, j))],
        out_specs=pl.BlockSpec((bsz, tn), lambda j: (0, j)),
        out_shape=jax.ShapeDtypeStruct((bsz, n), F32),
        compiler_params=_params("arbitrary"),
        name="ada",
    )(c, w, bias)


def _rope_body(pos_ref, inv_ref, cos_ref, sin_ref):
    s = pos_ref.shape[1]
    ang = pos_ref[0].astype(F32) * inv_ref[...]
    lane = lax.broadcasted_iota(jnp.int32, ang.shape, 1)
    sin = jnp.sin(ang)
    cos_ref[0, 0] = jnp.cos(ang)
    sin_ref[0, 0] = jnp.where(lane < HEAD_DIM // 2, -sin, sin)
    for tab in (cos_ref, sin_ref):
        for g, dil in enumerate(DILATIONS[1:], start=1):
            sub = s // dil
            for r in range(dil):
                tab[g, 0, r * sub:(r + 1) * sub, :] = tab[0, 0, pl.ds(r, sub, stride=dil), :]


def _rope_tables(pos, inv2):
    bsz, s, _ = pos.shape
    ng = len(DILATIONS)
    spec = pl.BlockSpec((ng, 1, s, HEAD_DIM), lambda b: (0, b, 0, 0))
    return pl.pallas_call(
        _rope_body,
        grid=(bsz,),
        in_specs=[pl.BlockSpec((1, s, 1), lambda b: (b, 0, 0)),
                  pl.BlockSpec((1, HEAD_DIM), lambda b: (0, 0))],
        out_specs=[spec, spec],
        out_shape=[jax.ShapeDtypeStruct((ng, bsz, s, HEAD_DIM), F32)] * 2,
        compiler_params=_params("arbitrary"),
        name="rope_tables",
    )(pos, inv2)


H_TILE = 512


def _h_body(x_ref, ada_ref, g_ref, h1_ref, h4_ref, h16_ref, hs_ref):
    x = x_ref[0]
    ms = jnp.mean(x * x, axis=-1, keepdims=True)
    xn = x * lax.rsqrt(ms + EPS) * g_ref[...]
    h = xn * (1.0 + ada_ref[0, 1:2, :]) + ada_ref[0, 0:1, :]
    h1_ref[0] = h.astype(BF16)
    for cb in range(h.shape[1] // LANES):
        cols = slice(cb * LANES, (cb + 1) * LANES)
        hs_ref[cb] = h[:, cols]
        for r in range(4):
            h4_ref[0, r, :, cols] = hs_ref[cb, pl.ds(r, H_TILE // 4, stride=4), :].astype(BF16)
        for r in range(16):
            h16_ref[0, r, :, cols] = hs_ref[cb, pl.ds(r, H_TILE // 16, stride=16), :].astype(BF16)


def _modulated_norm(x, ada3, g):
    bsz, s, d = x.shape
    t = H_TILE
    outs = pl.pallas_call(
        _h_body,
        grid=(bsz, s // t),
        in_specs=[pl.BlockSpec((1, t, d), lambda b, i: (b, i, 0)),
                  pl.BlockSpec((1, 3, d), lambda b, i: (b, 0, 0)),
                  pl.BlockSpec((1, d), lambda b, i: (0, 0))],
        out_specs=[pl.BlockSpec((1, t, d), lambda b, i: (b, i, 0)),
                   pl.BlockSpec((1, 4, t // 4, d), lambda b, i: (b, 0, i, 0)),
                   pl.BlockSpec((1, 16, t // 16, d), lambda b, i: (b, 0, i, 0))],
        out_shape=[jax.ShapeDtypeStruct((bsz, s, d), BF16),
                   jax.ShapeDtypeStruct((bsz, 4, s // 4, d), BF16),
                   jax.ShapeDtypeStruct((bsz, 16, s // 16, d), BF16)],
        scratch_shapes=[pltpu.VMEM((d // LANES, t, LANES), F32)],
        compiler_params=_params("arbitrary", "arbitrary"),
        name="modulated_norm",
    )(x, ada3, g)
    h1, h4, h16 = outs
    return h1, h4.reshape(bsz, s, d), h16.reshape(bsz, s, d)


PROJ_ROWS = 512


def _cast_weight_tile(w_ref, wbf_ref):
    @pl.when(pl.program_id(1) == 0)
    def _():
        wbf_ref[...] = w_ref[...].astype(BF16)


PROJ_SPLIT = (PROJ_ROWS,) * 4


def _proj_chunks(h_ref, wbf_ref, epilogue, sizes=PROJ_SPLIT):
    assert sum(sizes) == h_ref.shape[1]
    start = 0
    for size in sizes:
        rows = slice(start, start + size)
        epilogue(rows, _dot(h_ref[0, rows, :], wbf_ref[...]))
        start += size


def _proj_rope_body(h_ref, w_ref, cos_ref, sin_ref, o_ref, wbf_ref):
    _cast_weight_tile(w_ref, wbf_ref)
    j = pl.program_id(0)
    n_rot = 2 * ATTN_WIDTH // PROJ_TN

    def rotate(rows, acc):
        cos = cos_ref[0, rows, :]
        sin = sin_ref[0, rows, :]
        for hh in range(PROJ_TN // HEAD_DIM):
            t = acc[:, hh * HEAD_DIM:(hh + 1) * HEAD_DIM]
            rot = t * cos + pltpu.roll(t, HEAD_DIM // 2, 1) * sin
            o_ref[0, hh, rows, :] = rot.astype(BF16)

    def plain(rows, acc):
        for hh in range(PROJ_TN // HEAD_DIM):
            o_ref[0, hh, rows, :] = acc[:, hh * HEAD_DIM:(hh + 1) * HEAD_DIM].astype(BF16)

    @pl.when(j < n_rot)
    def _():
        _proj_chunks(h_ref, wbf_ref, rotate)

    @pl.when(j >= n_rot)
    def _():
        _proj_chunks(h_ref, wbf_ref, plain)


def _proj_silu_body(h_ref, w_ref, o_ref, wbf_ref):
    _cast_weight_tile(w_ref, wbf_ref)

    def epilogue(rows, acc):
        o_ref[0, rows, :] = (acc * _sigmoid(acc)).astype(BF16)

    _proj_chunks(h_ref, wbf_ref, epilogue)


def _proj_sigmoid_body(h_ref, w_ref, o_ref, wbf_ref):
    _cast_weight_tile(w_ref, wbf_ref)

    def epilogue(rows, acc):
        o_ref[0, rows, :] = _sigmoid(acc).astype(BF16)

    _proj_chunks(h_ref, wbf_ref, epilogue)


CONV_PAD = 8
SUBLANES = 8
CONV_ROWS = 512
CONV_STRIDE = CONV_ROWS // SUBLANES + 1
CONV_TAIL = 3 * SUBLANES
CONV_CHUNKS = 4


def _proj_conv_body(h_ref, w_ref, cw_ref, cb_ref, o_ref, wbf_ref, *scratch):
    pads, stages = scratch[:CONV_CHUNKS], scratch[CONV_CHUNKS:]
    n_cb = PROJ_TN // LANES
    half = (CONV_WIDTH - 1) // 2
    end = CONV_PAD + CONV_ROWS
    _cast_weight_tile(w_ref, wbf_ref)
    pads[0][:, 0:CONV_PAD, :] = jnp.zeros((n_cb, CONV_PAD, LANES), F32)
    pads[-1][:, end:end + CONV_TAIL, :] = jnp.zeros((n_cb, CONV_TAIL, LANES), F32)

    def conv_chunk(m):
        base = m * CONV_ROWS
        for pair in range(n_cb // 2):
            cbs = (2 * pair, 2 * pair + 1)
            cols = [slice(cb * LANES, (cb + 1) * LANES) for cb in cbs]
            taps = [[cw_ref[k:k + 1, c] for k in range(CONV_WIDTH)] for c in cols]
            bias = [cb_ref[:, c] for c in cols]

            def tile(cb, i):
                return pads[m][cb, pl.ds(CONV_PAD + i, SUBLANES, stride=CONV_STRIDE), :]

            wins = [[tile(cb, i) for i in range(-half, half)] for cb in cbs]
            for i in range(CONV_STRIDE):
                acts = []
                for n, cb in enumerate(cbs):
                    win = wins[n]
                    win.append(tile(cb, i + half))
                    out = bias[n] + taps[n][0] * win[0]
                    for k in range(1, CONV_WIDTH):
                        out = out + taps[n][k] * win[k]
                    acts.append(out * _sigmoid(out))
                    win.pop(0)
                stages[m][pair, pl.ds(i, SUBLANES, stride=CONV_STRIDE), :] = pltpu.pack_elementwise(
                    acts, packed_dtype=BF16)
            packed = stages[m][pair, 0:CONV_ROWS, :]
            for n, c in enumerate(cols):
                o_ref[0, base:base + CONV_ROWS, c] = pltpu.unpack_elementwise(
                    packed, index=n, packed_dtype=BF16, unpacked_dtype=F32).astype(BF16)

    def epilogue(rows, acc):
        m = rows.start // CONV_ROWS
        for cb in range(n_cb):
            a = acc[:, cb * LANES:(cb + 1) * LANES]
            pads[m][cb, CONV_PAD:end, :] = a
            if m > 0:
                pads[m - 1][cb, end:end + CONV_TAIL, :] = a[0:CONV_TAIL]
            if m + 1 < CONV_CHUNKS:
                pads[m + 1][cb, 0:CONV_PAD, :] = a[CONV_ROWS - CONV_PAD:CONV_ROWS]
        if m > 0:
            conv_chunk(m - 1)

    _proj_chunks(h_ref, wbf_ref, epilogue, sizes=(CONV_ROWS,) * CONV_CHUNKS)
    conv_chunk(CONV_CHUNKS - 1)


def _rep_masks(shape):
    lane = lax.broadcasted_iota(jnp.int32, shape, len(shape) - 1) & (SSD_HEADS - 1)
    return lane < SSD_HPG, lane < 2 * SSD_HPG


def _packed_split(v):
    hi, mid, lo = _split3(v)
    m0, m1 = _rep_masks(v.shape)
    return jnp.where(m0, hi, jnp.where(m1, mid, lo))


def _replicate_heads(rolled, g, m0, m1):
    return jnp.where(m0, rolled[g], jnp.where(m1, rolled[g - 1], rolled[g - 2]))


def _proj_dt_body(h_ref, w_ref, bias_ref, o_ref, ot_ref, wbf_ref):
    s = h_ref.shape[1]
    _cast_weight_tile(w_ref, wbf_ref)
    acc = _dot(h_ref[0], wbf_ref[...]) + bias_ref[...]
    sp = jnp.maximum(acc, 0.0) + jnp.log1p(jnp.exp(-jnp.abs(acc)))
    n = 2 * SSD_HEADS
    m0, m1 = _rep_masks(sp.shape)
    rolled = {k: (sp if k == 0 else pltpu.roll(sp, (-k * SSD_HPG) % n, 1))
              for k in range(-2, SSD_GROUPS)}
    for g in range(SSD_GROUPS):
        o_ref[0, :, g * n:(g + 1) * n] = _replicate_heads(rolled, g, m0, m1)
    for c in range(s // SSD_CHUNK):
        t = sp[c * SSD_CHUNK:(c + 1) * SSD_CHUNK, :].T
        for g in range(SSD_GROUPS):
            for d in range(2):
                r0 = d * SSD_HEADS + g * SSD_HPG
                ot_ref[0, g, d, c * SSD_HPG:(c + 1) * SSD_HPG, :] = t[r0:r0 + SSD_HPG, :]


def _proj_dt(h, w_in, dt_bias):
    bsz, s, d = h.shape
    n = 2 * SSD_HEADS
    nc = s // SSD_CHUNK
    blk0 = COL_DT // n
    return pl.pallas_call(
        _proj_dt_body,
        grid=(1, bsz),
        in_specs=[pl.BlockSpec((1, s, d), lambda j, b: (b, 0, 0)),
                  pl.BlockSpec((d, n), lambda j, b: (0, blk0)),
                  pl.BlockSpec((1, n), lambda j, b: (0, 0))],
        out_specs=[pl.BlockSpec((1, s, SSD_GROUPS * n), lambda j, b: (b, 0, 0)),
                   pl.BlockSpec((1, SSD_GROUPS, 2, nc * SSD_HPG, SSD_CHUNK), lambda j, b: (b, 0, 0, 0, 0))],
        out_shape=[jax.ShapeDtypeStruct((bsz, s, SSD_GROUPS * n), F32),
                   jax.ShapeDtypeStruct((bsz, SSD_GROUPS, 2, nc * SSD_HPG, SSD_CHUNK), F32)],
        scratch_shapes=[pltpu.VMEM((d, n), BF16)],
        compiler_params=_params("arbitrary", "arbitrary"),
        name="proj_dt",
    )(h, w_in, dt_bias.reshape(1, n))


def _proj_call(body, h, w_in, col0, n_tiles, tn, out_cols, out_dtype, extra=(), extra_specs=(),
               scratch=(), out_tn=None, name="proj"):
    bsz, s, d = h.shape
    out_tn = tn if out_tn is None else out_tn
    blk0 = col0 // tn
    assert blk0 * tn == col0
    return pl.pallas_call(
        body,
        grid=(n_tiles, bsz),
        in_specs=[pl.BlockSpec((1, s, d), lambda j, b: (b, 0, 0)),
                  pl.BlockSpec((d, tn), lambda j, b: (0, blk0 + j)),
                  *extra_specs],
        out_specs=pl.BlockSpec((1, s, out_tn), lambda j, b: (b, 0, j)),
        out_shape=jax.ShapeDtypeStruct((bsz, s, out_cols), out_dtype),
        scratch_shapes=[pltpu.VMEM((d, tn), BF16), *scratch],
        compiler_params=_params("arbitrary", "arbitrary"),
        name=name,
    )(h, w_in, *extra)


def _proj_qkv(h, w_in, cos, sin, group):
    bsz, s, d = h.shape
    tn = PROJ_TN
    tps = ATTN_WIDTH // tn
    n_groups = len(DILATIONS)

    def wcol(j, b):
        return (0, (j // tps) * (n_groups * tps) + group * tps + j % tps)

    tab = pl.BlockSpec((1, s, HEAD_DIM), lambda j, b: (group * bsz + b, 0, 0))
    return pl.pallas_call(
        _proj_rope_body,
        grid=(3 * tps, bsz),
        in_specs=[pl.BlockSpec((1, s, d), lambda j, b: (b, 0, 0)),
                  pl.BlockSpec((d, tn), wcol), tab, tab],
        out_specs=pl.BlockSpec((1, tn // HEAD_DIM, s, HEAD_DIM), lambda j, b: (b, j, 0, 0)),
        out_shape=jax.ShapeDtypeStruct((bsz, 3 * ATTN_HEADS, s, HEAD_DIM), BF16),
        scratch_shapes=[pltpu.VMEM((d, tn), BF16)],
        compiler_params=_params("arbitrary", "arbitrary"),
        name=f"proj_qkv{group}",
    )(h, w_in, cos, sin)


ATT_BLK = 128


ATT_UNROLL = 16


def _attn_body(q1, k1, v1, q2, k2, v2, q3, k3, v3, z_ref, o_ref, on_ref, ls_ref, va_ref, bias_ref):
    s = o_ref.shape[1]
    scale = HEAD_DIM ** -0.5
    qi = lax.broadcasted_iota(jnp.int32, (ATT_BLK, 2 * ATT_BLK), 0)
    ki = lax.broadcasted_iota(jnp.int32, (ATT_BLK, 2 * ATT_BLK), 1)
    for t in range(3):
        bias_ref[t] = jnp.where(jnp.abs(qi - ki + t * N_SIDE) <= N_SIDE, 0.0, -jnp.inf)
    va_ref[:, :, HEAD_DIM:] = jnp.ones((len(DILATIONS), s, HEAD_DIM), BF16)

    def block(g, q_ref, k_ref, q0, k0, bias, nk):
        q = q_ref[0, 0, pl.ds(q0, ATT_BLK), :]
        kw = k_ref[0, 0, pl.ds(k0, nk), :]
        sc = lax.dot_general(q, kw, (((1,), (1,)), ((), ())), preferred_element_type=F32) * scale + bias
        m = jnp.max(sc, axis=-1, keepdims=True)
        p = jnp.exp(sc - m)
        pv = _dot(p.astype(BF16), va_ref[g, pl.ds(k0, nk), :])
        den = pv[:, HEAD_DIM:]
        return pv[:, :HEAD_DIM] / den, m + jnp.log(den)

    groups = ((q1, k1, v1), (q2, k2, v2), (q3, k3, v3))
    for g, (q_ref, k_ref, v_ref) in enumerate(groups):
        dil = DILATIONS[g]
        sub = s // dil
        nblk = sub // ATT_BLK
        va_ref[g, :, :HEAD_DIM] = v_ref[0, 0]
        for r in range(dil):
            if nblk == 1:
                o, lse = block(g, q_ref, k_ref, r * sub, r * sub, bias_ref[0, :, :ATT_BLK], ATT_BLK)
                on_ref[g, pl.ds(r, ATT_BLK, stride=dil), :] = o
                ls_ref[g, pl.ds(r, ATT_BLK, stride=dil), :] = lse
            else:
                def one(i, q_ref=q_ref, k_ref=k_ref, r=r, sub=sub, dil=dil, g=g):
                    loc = i * ATT_BLK
                    kloc = jnp.clip(loc - N_SIDE, 0, sub - 2 * ATT_BLK)
                    q0 = pl.multiple_of(r * sub + loc, ATT_BLK)
                    k0 = pl.multiple_of(r * sub + kloc, N_SIDE)
                    bias = bias_ref[lax.shift_right_logical(loc - kloc, N_SIDE.bit_length() - 1)]
                    o, lse = block(g, q_ref, k_ref, q0, k0, bias, 2 * ATT_BLK)
                    if dil == 1:
                        rows = pl.ds(q0, ATT_BLK)
                    else:
                        rows = pl.ds(loc * dil + r, ATT_BLK, stride=dil)
                    on_ref[g, rows, :] = o
                    ls_ref[g, rows, :] = lse

                un = min(ATT_UNROLL, nblk)

                def step(io, carry, one=one, un=un):
                    for j in range(un):
                        one(io * un + j)
                    return carry

                lax.fori_loop(0, nblk // un, step, 0)

    l0, l1, l2 = ls_ref[0], ls_ref[1], ls_ref[2]
    mx = jnp.maximum(jnp.maximum(l0, l1), l2)
    e0, e1, e2 = jnp.exp(l0 - mx), jnp.exp(l1 - mx), jnp.exp(l2 - mx)
    o = (e0 * on_ref[0] + e1 * on_ref[1] + e2 * on_ref[2]) / (e0 + e1 + e2)
    o_ref[0] = (o * z_ref[0].astype(F32)).astype(BF16)


def _attention(qkv, zs):
    bsz, _, s, _ = qkv[0].shape
    specs = []
    args = []
    for g in range(len(DILATIONS)):
        for sec in range(3):
            specs.append(pl.BlockSpec((1, 1, s, HEAD_DIM),
                                      lambda b, h, sec=sec: (b, sec * ATTN_HEADS + h, 0, 0)))
            args.append(qkv[g])
    specs.append(pl.BlockSpec((1, s, HEAD_DIM), lambda b, h: (b, 0, h)))
    args.append(zs)
    return pl.pallas_call(
        _attn_body,
        grid=(bsz, ATTN_HEADS),
        in_specs=specs,
        out_specs=pl.BlockSpec((1, s, HEAD_DIM), lambda b, h: (b, 0, h)),
        out_shape=jax.ShapeDtypeStruct((bsz, s, ATTN_WIDTH), BF16),
        scratch_shapes=[pltpu.VMEM((3, s, HEAD_DIM), F32), pltpu.VMEM((3, s, HEAD_DIM), F32),
                        pltpu.VMEM((len(DILATIONS), s, 2 * HEAD_DIM), BF16),
                        pltpu.VMEM((3, ATT_BLK, 2 * ATT_BLK), F32)],
        compiler_params=_params("arbitrary", "arbitrary"),
        name="dilated_attention",
    )(*args)


SSD_UNROLL = 4
SSD_BIG = 1e30


def _ssd_constants():
    q, gw = SSD_CHUNK, SSD_HPG * SSD_HEADDIM
    i = np.arange(q)
    low = i[None, :] <= i[:, None]
    upp = i[None, :] >= i[:, None]
    tri = np.stack([low, upp])
    tri3 = np.stack([np.tile(upp, (3, 1)), np.tile(low, (3, 1))])
    c = np.arange(LANES)[:, None]
    in_f = c < 3 * SSD_HPG
    in_b = (c >= SSD_HEADS) & (c < SSD_HEADS + 3 * SSD_HPG)
    head = np.arange(gw)[None, :] // SSD_HEADDIM
    efb = np.concatenate([in_f & (c % SSD_HPG == head), in_b & (c % SSD_HPG == head)], axis=1)
    k = np.arange(2 * SSD_HPG * LANES)[None, :] // LANES
    ecol = np.where(k < SSD_HPG, in_f & (c % SSD_HPG == k), in_b & (c % SSD_HPG == k - SSD_HPG))
    src = np.arange(2 * SSD_HPG * LANES)[None, :] % LANES
    dst = i[:, None]
    keep = np.where(k < SSD_HPG, dst >= src, dst <= src)
    rhs0 = np.concatenate([ecol.astype(np.float32), np.where(keep, 0.0, -SSD_BIG)], axis=0)
    return tuple(jnp.asarray(m, BF16) for m in (tri, tri3, efb, rhs0))


SSD_DYN_ROW0 = 2 * SSD_HPG
SSD_ONE_LANE0 = 3 * SSD_HPG


def _ssd_body(x_ref, b_ref, c_ref, dt_ref, dtt_ref, alog_ref, alogt_ref, dskip_ref, z_ref,
              tri_ref, tri3_ref, efb_ref, rhs0_ref, y_ref,
              pcum_ref, eexp_ref, cdx_ref, stb_ref, prev_ref, rdyn_ref, rhs_ref):
    q = SSD_CHUNK
    s = x_ref.shape[1]
    nc = s // q
    gw = SSD_HPG * SSD_HEADDIM
    fwd_lane = lax.broadcasted_iota(jnp.int32, (q, LANES), 1) < SSD_HEADS
    lane = lax.broadcasted_iota(jnp.int32, (q, LANES), 1)

    a_row = -jnp.exp(alog_ref[0])
    dt_all = dt_ref[0]
    a_all = dt_all * a_row
    a_wide = jnp.concatenate([a_all[c * q:(c + 1) * q] for c in range(nc)], axis=1)
    fwd_wide = (lax.broadcasted_iota(jnp.int32, a_wide.shape, 1) & (LANES - 1)) < SSD_HEADS
    both = jnp.concatenate([tri_ref[0], tri_ref[1]], axis=1)
    cum_wide = jnp.zeros(a_wide.shape, F32)
    for part in _split3(a_wide):
        zero = jnp.zeros_like(part)
        stacked = jnp.concatenate([jnp.where(fwd_wide, part, zero), jnp.where(fwd_wide, zero, part)], axis=0)
        cum_wide = cum_wide + _dot(both, stacked)

    dt_t = dtt_ref[0, 0]
    a_t = dt_t * -jnp.exp(alogt_ref[0])
    log_dt = jnp.where(dt_t > 0.0, jnp.log(dt_t), -SSD_BIG)
    neg_parts = []
    for d in range(2):
        parts = jnp.concatenate(_split3(a_t[d]), axis=1)
        src_term = _dot(parts, tri3_ref[d]) - log_dt[d]
        neg_parts.append([p.astype(F32) for p in _split3(-src_term)])
    sub = lax.broadcasted_iota(jnp.int32, (2 * SUBLANES, LANES), 0)
    for c in range(nc):
        blocks = []
        for k in range(2 * SSD_HPG):
            d, e = divmod(k, SSD_HPG)
            blk = jnp.where(sub == e, 1.0, 0.0) if d == 0 else jnp.zeros(sub.shape, F32)
            for p in range(3):
                r = c * SSD_HPG + e
                blk = jnp.where(sub == SUBLANES + p, neg_parts[d][p][r:r + 1, :], blk)
            blocks.append(blk)
        rdyn_ref[c] = jnp.concatenate(blocks, axis=1).astype(BF16)
    for j in range(SSD_UNROLL):
        rhs_ref[j] = rhs0_ref[...]

    bt = b_ref[0].T
    state_f = jnp.zeros((SSD_STATE, gw), F32)
    for c in range(nc):
        cum = cum_wide[:, c * LANES:(c + 1) * LANES]
        dt = dt_all[c * q:(c + 1) * q]
        ref = jnp.where(fwd_lane[0:1], cum[q - 1:q], cum[0:1])
        one_lane = (lane >= SSD_ONE_LANE0) & (lane < SSD_ONE_LANE0 + 3)
        pcum_ref[c] = jnp.where(one_lane, 1.0, _packed_split(cum).astype(F32)).astype(BF16)
        eexp_ref[c] = _dot(_packed_split(jnp.exp(cum)), efb_ref[...])
        wexp = _dot(_packed_split(dt * jnp.exp(ref - cum)), efb_ref[...])
        cdx = _dot(_packed_split(jnp.broadcast_to(jnp.exp(ref), (8, LANES))), efb_ref[...])[0:1]
        cdx_ref[c] = cdx
        xb = x_ref[0, c * q:(c + 1) * q, :].astype(F32)
        xw = (jnp.concatenate([xb, xb], axis=1) * wexp).astype(BF16)
        st = _dot(bt[:, c * q:(c + 1) * q], xw)
        stb_ref[c] = st[:, gw:]
        prev_ref[c, :, 0:gw] = state_f.astype(BF16)
        state_f = state_f * cdx[:, 0:gw] + st[:, 0:gw]
    state_b = jnp.zeros((SSD_STATE, gw), F32)
    for c in range(nc - 1, -1, -1):
        prev_ref[c, :, gw:] = state_b.astype(BF16)
        state_b = state_b * cdx_ref[c][:, gw:] + stb_ref[c]

    row = lax.broadcasted_iota(jnp.int32, (q, q), 0)
    col = lax.broadcasted_iota(jnp.int32, (q, q), 1)
    eye = jnp.where(row == col, 1.0, 0.0).astype(BF16)

    def chunk_out(c, slot):
        rows = pl.ds(pl.multiple_of(c * q, q), q)
        bc = b_ref[0, rows, :]
        cc = c_ref[0, rows, :]
        xb = x_ref[0, rows, :]
        cb = lax.dot_general(cc, bc, (((1,), (1,)), ((), ())), preferred_element_type=F32)
        rhs_ref[slot, SSD_DYN_ROW0:SSD_DYN_ROW0 + 2 * SUBLANES, :] = rdyn_ref[c]
        expo = _dot(jnp.concatenate([pcum_ref[c], eye], axis=1), rhs_ref[slot])
        ms = []
        for e in range(SSD_HPG):
            k = SSD_HPG + e
            w = jnp.exp(expo[:, e * LANES:(e + 1) * LANES]) + jnp.exp(expo[:, k * LANES:(k + 1) * LANES])
            ms.append((cb * w).astype(BF16))
        ys = []
        for p in range(SSD_HPG // 2):
            lhs = jnp.concatenate([ms[2 * p], ms[2 * p + 1]], axis=1)
            xp = xb[:, p * LANES:(p + 1) * LANES]
            zero = jnp.zeros_like(xp)
            rhs = jnp.concatenate([jnp.where(lane < SSD_HEADDIM, xp, zero),
                                   jnp.where(lane >= SSD_HEADDIM, xp, zero)], axis=0)
            ys.append(_dot(lhs, rhs))
        y = jnp.concatenate(ys, axis=1)
        yoff = _dot(cc, prev_ref[c]) * eexp_ref[c]
        y = y + yoff[:, 0:gw] + yoff[:, gw:]
        y = (y + dskip_ref[...] * xb.astype(F32)) * z_ref[0, rows, :].astype(F32)
        y_ref[0, rows, :] = y.astype(BF16)

    def chunk_step(io, carry):
        for j in range(SSD_UNROLL):
            chunk_out(io * SSD_UNROLL + j, j)
        return carry

    lax.fori_loop(0, nc // SSD_UNROLL, chunk_step, 0)


def _ssd(xbc, dtx, dtt, alog_g, alogt_g, dskip_x, zs):
    bsz, s, _ = xbc.shape
    q = SSD_CHUNK
    nc = s // q
    gw = SSD_HPG * SSD_HEADDIM
    nb = SSD_WIDTH // SSD_STATE
    zoff = ATTN_WIDTH // gw
    consts = _ssd_constants()
    const_specs = [pl.BlockSpec(m.shape, lambda b, g, nd=m.ndim: (0,) * nd) for m in consts]
    return pl.pallas_call(
        _ssd_body,
        grid=(bsz, SSD_GROUPS),
        in_specs=[pl.BlockSpec((1, s, gw), lambda b, g: (b, 0, g)),
                  pl.BlockSpec((1, s, SSD_STATE), lambda b, g: (b, 0, nb + g)),
                  pl.BlockSpec((1, s, SSD_STATE), lambda b, g: (b, 0, nb + SSD_GROUPS + g)),
                  pl.BlockSpec((1, s, LANES), lambda b, g: (b, 0, g)),
                  pl.BlockSpec((1, 1, 2, nc * SSD_HPG, q), lambda b, g: (b, g, 0, 0, 0)),
                  pl.BlockSpec((1, 1, LANES), lambda b, g: (g, 0, 0)),
                  pl.BlockSpec((1, 2, nc * SSD_HPG, q), lambda b, g: (g, 0, 0, 0)),
                  pl.BlockSpec((1, gw), lambda b, g: (0, g)),
                  pl.BlockSpec((1, s, gw), lambda b, g: (b, 0, zoff + g)),
                  *const_specs],
        out_specs=pl.BlockSpec((1, s, gw), lambda b, g: (b, 0, g)),
        out_shape=jax.ShapeDtypeStruct((bsz, s, SSD_WIDTH), BF16),
        scratch_shapes=[pltpu.VMEM((nc, q, LANES), BF16),
                        pltpu.VMEM((nc, q, 2 * gw), F32),
                        pltpu.VMEM((nc, 1, 2 * gw), F32),
                        pltpu.VMEM((nc, SSD_STATE, gw), F32),
                        pltpu.VMEM((nc, SSD_STATE, 2 * gw), BF16),
                        pltpu.VMEM((nc, 2 * SUBLANES, 2 * SSD_HPG * LANES), BF16),
                        pltpu.VMEM((SSD_UNROLL, 2 * q, 2 * SSD_HPG * LANES), BF16)],
        compiler_params=_params("arbitrary", "arbitrary"),
        name="ssd",
    )(xbc, xbc, xbc, dtx, dtt, alog_g, alogt_g, dskip_x, zs, *consts)


TAIL_TM = 512
TAIL_A_TM = 256


def _tail_a_body(ya_ref, y_ref, g_ref, ng_ref, wa_ref, ws_ref, o_ref):
    d = o_ref.shape[1]
    ya = _dot(ya_ref[...], wa_ref[...])
    y = y_ref[...].astype(F32)
    yn = y * lax.rsqrt(jnp.mean(y * y, axis=-1, keepdims=True) + EPS) * ng_ref[...]
    ys = _dot(yn.astype(BF16), ws_ref[...])
    ga = g_ref[:, 0:d].astype(F32)
    gs = g_ref[:, d:2 * d].astype(F32)
    o_ref[...] = (ga * ya + gs * ys).astype(BF16)


def _tail_a(ya_in, y, gates, norm_g, wa, ws):
    n, d = ya_in.shape[0], wa.shape[1]
    tm = TAIL_A_TM
    resident = functools.partial(pl.BlockSpec, pipeline_mode=pl.Buffered(1))
    return pl.pallas_call(
        _tail_a_body,
        grid=(n // tm,),
        in_specs=[pl.BlockSpec((tm, ya_in.shape[1]), lambda i: (i, 0)),
                  pl.BlockSpec((tm, y.shape[1]), lambda i: (i, 0)),
                  pl.BlockSpec((tm, gates.shape[1]), lambda i: (i, 0)),
                  pl.BlockSpec((1, y.shape[1]), lambda i: (0, 0)),
                  resident(wa.shape, lambda i: (0, 0)),
                  resident(ws.shape, lambda i: (0, 0))],
        out_specs=pl.BlockSpec((tm, d), lambda i: (i, 0)),
        out_shape=jax.ShapeDtypeStruct((n, d), BF16),
        compiler_params=_params("arbitrary"),
        name="tail_a",
    )(ya_in, y, gates, norm_g, wa, ws)


def _tail_b_body(m_ref, x_ref, ada_ref, w_ref, fg_ref, o_ref):
    t = _dot(m_ref[...], w_ref[...])
    xn = x_ref[...] + ada_ref[0, 2:3, :] * t
    o_ref[...] = xn * lax.rsqrt(jnp.mean(xn * xn, axis=-1, keepdims=True) + EPS) * fg_ref[...]


def _tail_b(merged, x2, ada3, w_out, final_g, seq):
    n, d = x2.shape
    tm = TAIL_TM
    per = seq // tm
    resident = functools.partial(pl.BlockSpec, pipeline_mode=pl.Buffered(1))
    return pl.pallas_call(
        _tail_b_body,
        grid=(n // tm,),
        in_specs=[pl.BlockSpec((tm, d), lambda i: (i, 0)),
                  pl.BlockSpec((tm, d), lambda i: (i, 0)),
                  pl.BlockSpec((1, 3, d), lambda i: (i // per, 0, 0)),
                  resident(w_out.shape, lambda i: (0, 0)),
                  pl.BlockSpec((1, d), lambda i: (0, 0))],
        out_specs=pl.BlockSpec((tm, d), lambda i: (i, 0)),
        out_shape=jax.ShapeDtypeStruct((n, d), F32),
        compiler_params=_params("arbitrary"),
        name="tail_b",
    )(merged, x2, ada3, w_out, final_g)


def _layer(x, ada3, pos_tabs, norm_g, w_in, conv_w, conv_b, dt_bias, a_log, d_skip, ssd_norm_g,
           w_br_attn, w_br_ssd, w_out, out_g):
    bsz, s, d = x.shape
    cos, sin = pos_tabs
    hs = _modulated_norm(x, ada3, norm_g[None])
    qkv = [_proj_qkv(hs[g], w_in, cos, sin, g) for g in range(len(DILATIONS))]
    h = hs[0]
    tn = PROJ_TN
    zs = _proj_call(_proj_silu_body, h, w_in, COL_Z, (ATTN_WIDTH + SSD_WIDTH) // tn, tn,
                    ATTN_WIDTH + SSD_WIDTH, BF16, name="proj_z")
    xbc = _proj_call(
        _proj_conv_body, h, w_in, COL_XBC, CONV_CH // tn, tn, CONV_CH, BF16,
        extra=(conv_w, conv_b[None]),
        extra_specs=(pl.BlockSpec((CONV_WIDTH, tn), lambda j, b: (0, j)),
                     pl.BlockSpec((1, tn), lambda j, b: (0, j))),
        scratch=(*[pltpu.VMEM((tn // LANES, CONV_PAD + CONV_ROWS + CONV_TAIL, LANES), F32)] * CONV_CHUNKS,
                 *[pltpu.VMEM((tn // (2 * LANES), SUBLANES * CONV_STRIDE, LANES), jnp.uint32)] * CONV_CHUNKS),
        name="proj_xbc")
    n_dt = 2 * SSD_HEADS
    dtx, dtt = _proj_dt(h, w_in, dt_bias)
    gates = _proj_call(_proj_sigmoid_body, h, w_in[:, COL_GATE:], 0, 2 * d // tn, tn, 2 * d,
                       BF16, name="proj_gates")

    ya_in = _attention(qkv, zs)

    a_flat = a_log.reshape(1, n_dt)
    rolled = {k: jnp.roll(a_flat, -k * SSD_HPG, axis=1) for k in range(-2, SSD_GROUPS)}
    m0, m1 = _rep_masks(a_flat.shape)
    alog_g = jnp.stack([_replicate_heads(rolled, g, m0, m1) for g in range(SSD_GROUPS)])
    nc = s // SSD_CHUNK
    alogt_g = a_log.reshape(2, SSD_GROUPS, SSD_HPG).transpose(1, 0, 2)
    alogt_g = jnp.broadcast_to(alogt_g[:, :, None, :, None], (SSD_GROUPS, 2, nc, SSD_HPG, SSD_CHUNK))
    alogt_g = alogt_g.reshape(SSD_GROUPS, 2, nc * SSD_HPG, SSD_CHUNK)
    dskip_x = jnp.repeat(d_skip, SSD_HEADDIM)[None]
    y = _ssd(xbc, dtx, dtt, alog_g, alogt_g, dskip_x, zs)

    n = bsz * s
    merged = _tail_a(ya_in.reshape(n, ATTN_WIDTH), y.reshape(n, SSD_WIDTH),
                     gates.reshape(n, gates.shape[-1]), ssd_norm_g[None],
                     w_br_attn.astype(BF16), w_br_ssd.astype(BF16))
    out = _tail_b(merged, x.reshape(n, d), ada3, w_out.astype(BF16), out_g, s)
    return out.reshape(bsz, s, d)


def kernel(x, c, positions, norm_g, w_ada, b_ada, w_in, conv_w, conv_b, dt_bias, a_log, d_skip,
           ssd_norm_g, w_br_attn, w_br_ssd, w_out, final_g):
    bsz, s, d = x.shape
    depth = w_in.shape[0]
    inv = ROPE_THETA ** (-jnp.arange(0, HEAD_DIM, 2, dtype=F32) / HEAD_DIM)
    inv2 = jnp.concatenate([inv, inv])[None]
    cos, sin = _rope_tables(positions[..., None], inv2)
    pos_tabs = (cos.reshape(-1, s, HEAD_DIM), sin.reshape(-1, s, HEAD_DIM))
    for i in range(depth):
        ada3 = _ada(c, w_ada[i], b_ada[i][None]).reshape(bsz, 3, d)
        assert depth == 1
        x = _layer(x, ada3, pos_tabs, norm_g[i], w_in[i], conv_w[i], conv_b[i], dt_bias[i], a_log[i],
                   d_skip[i], ssd_norm_g[i], w_br_attn[i], w_br_ssd[i], w_out[i], final_g[None])
    return x
```

```python
import functools

import numpy as np
import jax
import jax.numpy as jnp
from jax import lax
from jax.experimental import pallas as pl
from jax.experimental.pallas import tpu as pltpu

F32 = jnp.float32
BF16 = jnp.bfloat16

D_MODEL = 2048
HEAD_DIM = 128
ATTN_HEADS = 12
DILATIONS = (1, 4, 16)
N_SIDE = 64
ATTN_WIDTH = ATTN_HEADS * HEAD_DIM
ROPE_THETA = 10000.0
SSD_WIDTH = 2 * D_MODEL
SSD_HEADDIM = 64
SSD_GROUPS = 8
SSD_HEADS = SSD_WIDTH // SSD_HEADDIM
SSD_HPG = SSD_HEADS // SSD_GROUPS
SSD_STATE = 128
SSD_CHUNK = 128
CONV_WIDTH = 5
CONV_CH = SSD_WIDTH + 2 * SSD_GROUPS * SSD_STATE
EPS = 1e-6
QKV_COLS = 3 * len(DILATIONS) * ATTN_WIDTH
COL_Z = QKV_COLS
COL_XBC = COL_Z + ATTN_WIDTH + SSD_WIDTH
COL_DT = COL_XBC + CONV_CH
COL_GATE = COL_DT + 2 * SSD_HEADS
IN_COLS = COL_GATE + 2 * D_MODEL

LANES = 128
PROJ_TN = 512
QKV_TN = 768
WIDE_TN = 1024
V7X_VMEM_LIMIT_BYTES = 56 * 1024 * 1024


def _params(*sem):
    return pltpu.CompilerParams(dimension_semantics=sem, vmem_limit_bytes=V7X_VMEM_LIMIT_BYTES)


def _dot(a, b):
    return jnp.dot(a, b, preferred_element_type=F32)


def _sigmoid(x):
    return 1.0 / (1.0 + jnp.exp(-x))


def _split3(v):
    hi = v.astype(BF16)
    r = v - hi.astype(F32)
    mid = r.astype(BF16)
    lo = (r - mid.astype(F32)).astype(BF16)
    return hi, mid, lo


def _exact_lmul(t01, v):
    hi, mid, lo = _split3(v)
    return _dot(t01, hi) + _dot(t01, mid) + _dot(t01, lo)


def _exact_rmul(v, e01):
    hi, mid, lo = _split3(v)
    return _dot(hi, e01) + _dot(mid, e01) + _dot(lo, e01)


def _ada_body(c_ref, w_ref, b_ref, o_ref):
    o_ref[...] = _dot(c_ref[...].astype(BF16), w_ref[...].astype(BF16)) + b_ref[...]


def _ada(c, w, bias):
    bsz, d = c.shape
    n = w.shape[1]
    tn = 768
    return pl.pallas_call(
        _ada_body,
        grid=(n // tn,),
        in_specs=[pl.BlockSpec((bsz, d), lambda j: (0, 0)),
                  pl.BlockSpec((d, tn), lambda j: (0, j)),
                  pl.BlockSpec((1, tn), lambda j: (0, j))],
        out_specs=pl.BlockSpec((bsz, tn), lambda j: (0, j)),
        out_shape=jax.ShapeDtypeStruct((bsz, n), F32),
        compiler_params=_params("arbitrary"),
        name="ada",
    )(c, w, bias)


def _rope_body(pos_ref, inv_ref, cos_ref, sin_ref):
    s = pos_ref.shape[1]
    ang = pos_ref[0].astype(F32) * inv_ref[...]
    lane = lax.broadcasted_iota(jnp.int32, ang.shape, 1)
    sin = jnp.sin(ang)
    cos_ref[0, 0] = jnp.cos(ang)
    sin_ref[0, 0] = jnp.where(lane < HEAD_DIM // 2, -sin, sin)
    for tab in (cos_ref, sin_ref):
        for g, dil in enumerate(DILATIONS[1:], start=1):
            sub = s // dil
            for r in range(dil):
                tab[g, 0, r * sub:(r + 1) * sub, :] = tab[0, 0, pl.ds(r, sub, stride=dil), :]


def _rope_tables(pos, inv2):
    bsz, s, _ = pos.shape
    ng = len(DILATIONS)
    spec = pl.BlockSpec((ng, 1, s, HEAD_DIM), lambda b: (0, b, 0, 0))
    return pl.pallas_call(
        _rope_body,
        grid=(bsz,),
        in_specs=[pl.BlockSpec((1, s, 1), lambda b: (b, 0, 0)),
                  pl.BlockSpec((1, HEAD_DIM), lambda b: (0, 0))],
        out_specs=[spec, spec],
        out_shape=[jax.ShapeDtypeStruct((ng, bsz, s, HEAD_DIM), F32)] * 2,
        compiler_params=_params("arbitrary"),
        name="rope_tables",
    )(pos, inv2)


H_TILE = 512


def _h_body(x_ref, ada_ref, g_ref, h1_ref, h4_ref, h16_ref, hs_ref):
    x = x_ref[0]
    ms = jnp.mean(x * x, axis=-1, keepdims=True)
    xn = x * lax.rsqrt(ms + EPS) * g_ref[...]
    h = xn * (1.0 + ada_ref[0, 1:2, :]) + ada_ref[0, 0:1, :]
    h1_ref[0] = h.astype(BF16)
    for cb in range(h.shape[1] // LANES):
        cols = slice(cb * LANES, (cb + 1) * LANES)
        hs_ref[cb] = h[:, cols]
        for r in range(4):
            h4_ref[0, r, :, cols] = hs_ref[cb, pl.ds(r, H_TILE // 4, stride=4), :].astype(BF16)
        for r in range(16):
            h16_ref[0, r, :, cols] = hs_ref[cb, pl.ds(r, H_TILE // 16, stride=16), :].astype(BF16)


def _modulated_norm(x, ada3, g):
    bsz, s, d = x.shape
    t = H_TILE
    outs = pl.pallas_call(
        _h_body,
        grid=(bsz, s // t),
        in_specs=[pl.BlockSpec((1, t, d), lambda b, i: (b, i, 0)),
                  pl.BlockSpec((1, 3, d), lambda b, i: (b, 0, 0)),
                  pl.BlockSpec((1, d), lambda b, i: (0, 0))],
        out_specs=[pl.BlockSpec((1, t, d), lambda b, i: (b, i, 0)),
                   pl.BlockSpec((1, 4, t // 4, d), lambda b, i: (b, 0, i, 0)),
                   pl.BlockSpec((1, 16, t // 16, d), lambda b, i: (b, 0, i, 0))],
        out_shape=[jax.ShapeDtypeStruct((bsz, s, d), BF16),
                   jax.ShapeDtypeStruct((bsz, 4, s // 4, d), BF16),
                   jax.ShapeDtypeStruct((bsz, 16, s // 16, d), BF16)],
        scratch_shapes=[pltpu.VMEM((d // LANES, t, LANES), F32)],
        compiler_params=_params("arbitrary", "arbitrary"),
        name="modulated_norm",
    )(x, ada3, g)
    h1, h4, h16 = outs
    return h1, h4.reshape(bsz, s, d), h16.reshape(bsz, s, d)


PROJ_ROWS = 512


def _cast_weight_tile(w_ref, wbf_ref):
    @pl.when(pl.program_id(1) == 0)
    def _():
        wbf_ref[...] = w_ref[...].astype(BF16)


PROJ_SPLIT = (PROJ_ROWS,) * 4


def _proj_chunks(h_ref, wbf_ref, epilogue, sizes=PROJ_SPLIT):
    assert sum(sizes) == h_ref.shape[1]
    start = 0
    for size in sizes:
        rows = slice(start, start + size)
        epilogue(rows, _dot(h_ref[0, rows, :], wbf_ref[...]))
        start += size


def _proj_rope_body(h_ref, w_ref, cos_ref, sin_ref, o_ref, wbf_ref):
    _cast_weight_tile(w_ref, wbf_ref)
    j = pl.program_id(0)
    heads = o_ref.shape[1]
    n_rot = 2 * ATTN_HEADS // heads

    def rotate(rows, acc):
        cos = cos_ref[0, rows, :]
        sin = sin_ref[0, rows, :]
        for hh in range(heads):
            t = acc[:, hh * HEAD_DIM:(hh + 1) * HEAD_DIM]
            rot = t * cos + pltpu.roll(t, HEAD_DIM // 2, 1) * sin
            o_ref[0, hh, rows, :] = rot.astype(BF16)

    def plain(rows, acc):
        for hh in range(heads):
            o_ref[0, hh, rows, :] = acc[:, hh * HEAD_DIM:(hh + 1) * HEAD_DIM].astype(BF16)

    @pl.when(j < n_rot)
    def _():
        _proj_chunks(h_ref, wbf_ref, rotate)

    @pl.when(j >= n_rot)
    def _():
        _proj_chunks(h_ref, wbf_ref, plain)


def _proj_silu_body(h_ref, w_ref, o_ref, wbf_ref):
    _cast_weight_tile(w_ref, wbf_ref)

    def epilogue(rows, acc):
        o_ref[0, rows, :] = (acc * _sigmoid(acc)).astype(BF16)

    _proj_chunks(h_ref, wbf_ref, epilogue)


def _proj_sigmoid_body(h_ref, w_ref, o_ref, wbf_ref):
    _cast_weight_tile(w_ref, wbf_ref)

    def epilogue(rows, acc):
        o_ref[0, rows, :] = _sigmoid(acc).astype(BF16)

    _proj_chunks(h_ref, wbf_ref, epilogue)


CONV_PAD = 8
SUBLANES = 8
CONV_ROWS = 512
CONV_STRIDE = CONV_ROWS // SUBLANES + 1
CONV_TAIL = 3 * SUBLANES
CONV_CHUNKS = 4


def _proj_conv_body(h_ref, w_ref, cw_ref, cb_ref, o_ref, wbf_ref, *scratch):
    pads, stages = scratch[:CONV_CHUNKS], scratch[CONV_CHUNKS:]
    n_cb = PROJ_TN // LANES
    half = (CONV_WIDTH - 1) // 2
    end = CONV_PAD + CONV_ROWS
    _cast_weight_tile(w_ref, wbf_ref)
    pads[0][:, 0:CONV_PAD, :] = jnp.zeros((n_cb, CONV_PAD, LANES), F32)
    pads[-1][:, end:end + CONV_TAIL, :] = jnp.zeros((n_cb, CONV_TAIL, LANES), F32)

    def conv_chunk(m):
        base = m * CONV_ROWS
        for pair in range(n_cb // 2):
            cbs = (2 * pair, 2 * pair + 1)
            cols = [slice(cb * LANES, (cb + 1) * LANES) for cb in cbs]
            taps = [[cw_ref[k:k + 1, c] for k in range(CONV_WIDTH)] for c in cols]
            bias = [cb_ref[:, c] for c in cols]

            def tile(cb, i):
                return pads[m][cb, pl.ds(CONV_PAD + i, SUBLANES, stride=CONV_STRIDE), :]

            wins = [[tile(cb, i) for i in range(-half, half)] for cb in cbs]
            for i in range(CONV_STRIDE):
                acts = []
                for n, cb in enumerate(cbs):
                    win = wins[n]
                    win.append(tile(cb, i + half))
                    out = bias[n] + taps[n][0] * win[0]
                    for k in range(1, CONV_WIDTH):
                        out = out + taps[n][k] * win[k]
                    acts.append(out * _sigmoid(out))
                    win.pop(0)
                stages[m][pair, pl.ds(i, SUBLANES, stride=CONV_STRIDE), :] = pltpu.pack_elementwise(
                    acts, packed_dtype=BF16)
            packed = stages[m][pair, 0:CONV_ROWS, :]
            for n, c in enumerate(cols):
                o_ref[0, base:base + CONV_ROWS, c] = pltpu.unpack_elementwise(
                    packed, index=n, packed_dtype=BF16, unpacked_dtype=F32).astype(BF16)

    def epilogue(rows, acc):
        m = rows.start // CONV_ROWS
        for cb in range(n_cb):
            a = acc[:, cb * LANES:(cb + 1) * LANES]
            pads[m][cb, CONV_PAD:end, :] = a
            if m > 0:
                pads[m - 1][cb, end:end + CONV_TAIL, :] = a[0:CONV_TAIL]
            if m + 1 < CONV_CHUNKS:
                pads[m + 1][cb, 0:CONV_PAD, :] = a[CONV_ROWS - CONV_PAD:CONV_ROWS]
        if m > 0:
            conv_chunk(m - 1)

    _proj_chunks(h_ref, wbf_ref, epilogue, sizes=(CONV_ROWS,) * CONV_CHUNKS)
    conv_chunk(CONV_CHUNKS - 1)


def _rep_masks(shape):
    lane = lax.broadcasted_iota(jnp.int32, shape, len(shape) - 1) & (SSD_HEADS - 1)
    return lane < SSD_HPG, lane < 2 * SSD_HPG


def _packed_split(v):
    hi, mid, lo = _split3(v)
    m0, m1 = _rep_masks(v.shape)
    return jnp.where(m0, hi, jnp.where(m1, mid, lo))


def _replicate_heads(rolled, g, m0, m1):
    return jnp.where(m0, rolled[g], jnp.where(m1, rolled[g - 1], rolled[g - 2]))


def _proj_dt_body(h_ref, w_ref, bias_ref, o_ref, ot_ref, wbf_ref):
    s = h_ref.shape[1]
    _cast_weight_tile(w_ref, wbf_ref)
    acc = _dot(h_ref[0], wbf_ref[...]) + bias_ref[...]
    sp = jnp.maximum(acc, 0.0) + jnp.log1p(jnp.exp(-jnp.abs(acc)))
    n = 2 * SSD_HEADS
    m0, m1 = _rep_masks(sp.shape)
    rolled = {k: (sp if k == 0 else pltpu.roll(sp, (-k * SSD_HPG) % n, 1))
              for k in range(-2, SSD_GROUPS)}
    for g in range(SSD_GROUPS):
        o_ref[0, :, g * n:(g + 1) * n] = _replicate_heads(rolled, g, m0, m1)
    for c in range(s // SSD_CHUNK):
        t = sp[c * SSD_CHUNK:(c + 1) * SSD_CHUNK, :].T
        for g in range(SSD_GROUPS):
            for d in range(2):
                r0 = d * SSD_HEADS + g * SSD_HPG
                ot_ref[0, g, d, c * SSD_HPG:(c + 1) * SSD_HPG, :] = t[r0:r0 + SSD_HPG, :]


def _proj_dt(h, w_in, dt_bias):
    bsz, s, d = h.shape
    n = 2 * SSD_HEADS
    nc = s // SSD_CHUNK
    blk0 = COL_DT // n
    return pl.pallas_call(
        _proj_dt_body,
        grid=(1, bsz),
        in_specs=[pl.BlockSpec((1, s, d), lambda j, b: (b, 0, 0)),
                  pl.BlockSpec((d, n), lambda j, b: (0, blk0)),
                  pl.BlockSpec((1, n), lambda j, b: (0, 0))],
        out_specs=[pl.BlockSpec((1, s, SSD_GROUPS * n), lambda j, b: (b, 0, 0)),
                   pl.BlockSpec((1, SSD_GROUPS, 2, nc * SSD_HPG, SSD_CHUNK), lambda j, b: (b, 0, 0, 0, 0))],
        out_shape=[jax.ShapeDtypeStruct((bsz, s, SSD_GROUPS * n), F32),
                   jax.ShapeDtypeStruct((bsz, SSD_GROUPS, 2, nc * SSD_HPG, SSD_CHUNK), F32)],
        scratch_shapes=[pltpu.VMEM((d, n), BF16)],
        compiler_params=_params("arbitrary", "arbitrary"),
        name="proj_dt",
    )(h, w_in, dt_bias.reshape(1, n))


def _proj_call(body, h, w_in, col0, n_tiles, tn, out_cols, out_dtype, extra=(), extra_specs=(),
               scratch=(), out_tn=None, name="proj"):
    bsz, s, d = h.shape
    out_tn = tn if out_tn is None else out_tn
    blk0 = col0 // tn
    assert blk0 * tn == col0
    return pl.pallas_call(
        body,
        grid=(n_tiles, bsz),
        in_specs=[pl.BlockSpec((1, s, d), lambda j, b: (b, 0, 0)),
                  pl.BlockSpec((d, tn), lambda j, b: (0, blk0 + j)),
                  *extra_specs],
        out_specs=pl.BlockSpec((1, s, out_tn), lambda j, b: (b, 0, j)),
        out_shape=jax.ShapeDtypeStruct((bsz, s, out_cols), out_dtype),
        scratch_shapes=[pltpu.VMEM((d, tn), BF16), *scratch],
        compiler_params=_params("arbitrary", "arbitrary"),
        name=name,
    )(h, w_in, *extra)


def _proj_qkv(h, w_in, cos, sin, group):
    bsz, s, d = h.shape
    tn = QKV_TN
    tps = ATTN_WIDTH // tn
    n_groups = len(DILATIONS)

    def wcol(j, b):
        return (0, (j // tps) * (n_groups * tps) + group * tps + j % tps)

    tab = pl.BlockSpec((1, s, HEAD_DIM), lambda j, b: (group * bsz + b, 0, 0))
    return pl.pallas_call(
        _proj_rope_body,
        grid=(3 * tps, bsz),
        in_specs=[pl.BlockSpec((1, s, d), lambda j, b: (b, 0, 0)),
                  pl.BlockSpec((d, tn), wcol), tab, tab],
        out_specs=pl.BlockSpec((1, tn // HEAD_DIM, s, HEAD_DIM), lambda j, b: (b, j, 0, 0)),
        out_shape=jax.ShapeDtypeStruct((bsz, 3 * ATTN_HEADS, s, HEAD_DIM), BF16),
        scratch_shapes=[pltpu.VMEM((d, tn), BF16)],
        compiler_params=_params("arbitrary", "arbitrary"),
        name=f"proj_qkv{group}",
    )(h, w_in, cos, sin)


ATT_BLK = 128


ATT_UNROLL = 16


def _attn_body(q1, k1, v1, q2, k2, v2, q3, k3, v3, z_ref, o_ref, on_ref, ls_ref, va_ref, bias_ref):
    s = o_ref.shape[1]
    scale = HEAD_DIM ** -0.5
    qi = lax.broadcasted_iota(jnp.int32, (ATT_BLK, 2 * ATT_BLK), 0)
    ki = lax.broadcasted_iota(jnp.int32, (ATT_BLK, 2 * ATT_BLK), 1)
    for t in range(3):
        bias_ref[t] = jnp.where(jnp.abs(qi - ki + t * N_SIDE) <= N_SIDE, 0.0, -jnp.inf)
    va_ref[:, :, HEAD_DIM:] = jnp.ones((len(DILATIONS), s, HEAD_DIM), BF16)

    def block(g, q_ref, k_ref, q0, k0, bias, nk):
        q = q_ref[0, 0, pl.ds(q0, ATT_BLK), :]
        kw = k_ref[0, 0, pl.ds(k0, nk), :]
        sc = lax.dot_general(q, kw, (((1,), (1,)), ((), ())), preferred_element_type=F32) * scale + bias
        m = jnp.max(sc, axis=-1, keepdims=True)
        p = jnp.exp(sc - m)
        pv = _dot(p.astype(BF16), va_ref[g, pl.ds(k0, nk), :])
        den = pv[:, HEAD_DIM:]
        return pv[:, :HEAD_DIM] / den, m + jnp.log(den)

    groups = ((q1, k1, v1), (q2, k2, v2), (q3, k3, v3))
    for g, (q_ref, k_ref, v_ref) in enumerate(groups):
        dil = DILATIONS[g]
        sub = s // dil
        nblk = sub // ATT_BLK
        va_ref[g, :, :HEAD_DIM] = v_ref[0, 0]
        for r in range(dil):
            if nblk == 1:
                o, lse = block(g, q_ref, k_ref, r * sub, r * sub, bias_ref[0, :, :ATT_BLK], ATT_BLK)
                on_ref[g, pl.ds(r, ATT_BLK, stride=dil), :] = o
                ls_ref[g, pl.ds(r, ATT_BLK, stride=dil), :] = lse
            else:
                def one(i, q_ref=q_ref, k_ref=k_ref, r=r, sub=sub, dil=dil, g=g):
                    loc = i * ATT_BLK
                    kloc = jnp.clip(loc - N_SIDE, 0, sub - 2 * ATT_BLK)
                    q0 = pl.multiple_of(r * sub + loc, ATT_BLK)
                    k0 = pl.multiple_of(r * sub + kloc, N_SIDE)
                    bias = bias_ref[lax.shift_right_logical(loc - kloc, N_SIDE.bit_length() - 1)]
                    o, lse = block(g, q_ref, k_ref, q0, k0, bias, 2 * ATT_BLK)
                    if dil == 1:
                        rows = pl.ds(q0, ATT_BLK)
                    else:
                        rows = pl.ds(loc * dil + r, ATT_BLK, stride=dil)
                    on_ref[g, rows, :] = o
                    ls_ref[g, rows, :] = lse

                un = min(ATT_UNROLL, nblk)

                def step(io, carry, one=one, un=un):
                    for j in range(un):
                        one(io * un + j)
                    return carry

                lax.fori_loop(0, nblk // un, step, 0)

    l0, l1, l2 = ls_ref[0], ls_ref[1], ls_ref[2]
    mx = jnp.maximum(jnp.maximum(l0, l1), l2)
    e0, e1, e2 = jnp.exp(l0 - mx), jnp.exp(l1 - mx), jnp.exp(l2 - mx)
    o = (e0 * on_ref[0] + e1 * on_ref[1] + e2 * on_ref[2]) / (e0 + e1 + e2)
    o_ref[0] = (o * z_ref[0].astype(F32)).astype(BF16)


def _attention(qkv, zs):
    bsz, _, s, _ = qkv[0].shape
    specs = []
    args = []
    for g in range(len(DILATIONS)):
        for sec in range(3):
            specs.append(pl.BlockSpec((1, 1, s, HEAD_DIM),
                                      lambda b, h, sec=sec: (b, sec * ATTN_HEADS + h, 0, 0)))
            args.append(qkv[g])
    specs.append(pl.BlockSpec((1, s, HEAD_DIM), lambda b, h: (b, 0, h)))
    args.append(zs)
    return pl.pallas_call(
        _attn_body,
        grid=(bsz, ATTN_HEADS),
        in_specs=specs,
        out_specs=pl.BlockSpec((1, s, HEAD_DIM), lambda b, h: (b, 0, h)),
        out_shape=jax.ShapeDtypeStruct((bsz, s, ATTN_WIDTH), BF16),
        scratch_shapes=[pltpu.VMEM((3, s, HEAD_DIM), F32), pltpu.VMEM((3, s, HEAD_DIM), F32),
                        pltpu.VMEM((len(DILATIONS), s, 2 * HEAD_DIM), BF16),
                        pltpu.VMEM((3, ATT_BLK, 2 * ATT_BLK), F32)],
        compiler_params=_params("arbitrary", "arbitrary"),
        name="dilated_attention",
    )(*args)


SSD_UNROLL = 4
SSD_BIG = 1e30


def _ssd_constants():
    q, gw = SSD_CHUNK, SSD_HPG * SSD_HEADDIM
    i = np.arange(q)
    low = i[None, :] <= i[:, None]
    upp = i[None, :] >= i[:, None]
    tri = np.stack([low, upp])
    tri3 = np.stack([np.tile(upp, (3, 1)), np.tile(low, (3, 1))])
    c = np.arange(LANES)[:, None]
    in_f = c < 3 * SSD_HPG
    in_b = (c >= SSD_HEADS) & (c < SSD_HEADS + 3 * SSD_HPG)
    head = np.arange(gw)[None, :] // SSD_HEADDIM
    efb = np.concatenate([in_f & (c % SSD_HPG == head), in_b & (c % SSD_HPG == head)], axis=1)
    k = np.arange(2 * SSD_HPG * LANES)[None, :] // LANES
    ecol = np.where(k < SSD_HPG, in_f & (c % SSD_HPG == k), in_b & (c % SSD_HPG == k - SSD_HPG))
    src = np.arange(2 * SSD_HPG * LANES)[None, :] % LANES
    dst = i[:, None]
    keep = np.where(k < SSD_HPG, dst >= src, dst <= src)
    rhs0 = np.concatenate([ecol.astype(np.float32), np.where(keep, 0.0, -SSD_BIG)], axis=0)
    return tuple(jnp.asarray(m, BF16) for m in (tri, tri3, efb, rhs0))


SSD_DYN_ROW0 = 2 * SSD_HPG
SSD_ONE_LANE0 = 3 * SSD_HPG


def _ssd_body(x_ref, b_ref, c_ref, dt_ref, dtt_ref, alog_ref, alogt_ref, dskip_ref, z_ref,
              tri_ref, tri3_ref, efb_ref, rhs0_ref, y_ref,
              pcum_ref, eexp_ref, cdx_ref, stb_ref, prev_ref, rdyn_ref, rhs_ref):
    q = SSD_CHUNK
    s = x_ref.shape[1]
    nc = s // q
    gw = SSD_HPG * SSD_HEADDIM
    fwd_lane = lax.broadcasted_iota(jnp.int32, (q, LANES), 1) < SSD_HEADS
    lane = lax.broadcasted_iota(jnp.int32, (q, LANES), 1)

    a_row = -jnp.exp(alog_ref[0])
    dt_all = dt_ref[0]
    a_all = dt_all * a_row
    a_wide = jnp.concatenate([a_all[c * q:(c + 1) * q] for c in range(nc)], axis=1)
    fwd_wide = (lax.broadcasted_iota(jnp.int32, a_wide.shape, 1) & (LANES - 1)) < SSD_HEADS
    both = jnp.concatenate([tri_ref[0], tri_ref[1]], axis=1)
    cum_wide = jnp.zeros(a_wide.shape, F32)
    for part in _split3(a_wide):
        zero = jnp.zeros_like(part)
        stacked = jnp.concatenate([jnp.where(fwd_wide, part, zero), jnp.where(fwd_wide, zero, part)], axis=0)
        cum_wide = cum_wide + _dot(both, stacked)

    dt_t = dtt_ref[0, 0]
    a_t = dt_t * -jnp.exp(alogt_ref[0])
    log_dt = jnp.where(dt_t > 0.0, jnp.log(dt_t), -SSD_BIG)
    neg_parts = []
    for d in range(2):
        parts = jnp.concatenate(_split3(a_t[d]), axis=1)
        src_term = _dot(parts, tri3_ref[d]) - log_dt[d]
        neg_parts.append([p.astype(F32) for p in _split3(-src_term)])
    sub = lax.broadcasted_iota(jnp.int32, (2 * SUBLANES, LANES), 0)
    for c in range(nc):
        blocks = []
        for k in range(2 * SSD_HPG):
            d, e = divmod(k, SSD_HPG)
            blk = jnp.where(sub == e, 1.0, 0.0) if d == 0 else jnp.zeros(sub.shape, F32)
            for p in range(3):
                r = c * SSD_HPG + e
                blk = jnp.where(sub == SUBLANES + p, neg_parts[d][p][r:r + 1, :], blk)
            blocks.append(blk)
        rdyn_ref[c] = jnp.concatenate(blocks, axis=1).astype(BF16)
    for j in range(SSD_UNROLL):
        rhs_ref[j] = rhs0_ref[...]

    bt = b_ref[0].T
    state_f = jnp.zeros((SSD_STATE, gw), F32)
    for c in range(nc):
        cum = cum_wide[:, c * LANES:(c + 1) * LANES]
        dt = dt_all[c * q:(c + 1) * q]
        ref = jnp.where(fwd_lane[0:1], cum[q - 1:q], cum[0:1])
        one_lane = (lane >= SSD_ONE_LANE0) & (lane < SSD_ONE_LANE0 + 3)
        pcum_ref[c] = jnp.where(one_lane, 1.0, _packed_split(cum).astype(F32)).astype(BF16)
        eexp_ref[c] = _dot(_packed_split(jnp.exp(cum)), efb_ref[...])
        wexp = _dot(_packed_split(dt * jnp.exp(ref - cum)), efb_ref[...])
        cdx = _dot(_packed_split(jnp.broadcast_to(jnp.exp(ref), (8, LANES))), efb_ref[...])[0:1]
        cdx_ref[c] = cdx
        xb = x_ref[0, c * q:(c + 1) * q, :].astype(F32)
        xw = (jnp.concatenate([xb, xb], axis=1) * wexp).astype(BF16)
        st = _dot(bt[:, c * q:(c + 1) * q], xw)
        stb_ref[c] = st[:, gw:]
        prev_ref[c, :, 0:gw] = state_f.astype(BF16)
        state_f = state_f * cdx[:, 0:gw] + st[:, 0:gw]
    state_b = jnp.zeros((SSD_STATE, gw), F32)
    for c in range(nc - 1, -1, -1):
        prev_ref[c, :, gw:] = state_b.astype(BF16)
        state_b = state_b * cdx_ref[c][:, gw:] + stb_ref[c]

    row = lax.broadcasted_iota(jnp.int32, (q, q), 0)
    col = lax.broadcasted_iota(jnp.int32, (q, q), 1)
    eye = jnp.where(row == col, 1.0, 0.0).astype(BF16)

    def chunk_out(c, slot):
        rows = pl.ds(pl.multiple_of(c * q, q), q)
        bc = b_ref[0, rows, :]
        cc = c_ref[0, rows, :]
        xb = x_ref[0, rows, :]
        cb = lax.dot_general(cc, bc, (((1,), (1,)), ((), ())), preferred_element_type=F32)
        rhs_ref[slot, SSD_DYN_ROW0:SSD_DYN_ROW0 + 2 * SUBLANES, :] = rdyn_ref[c]
        expo = _dot(jnp.concatenate([pcum_ref[c], eye], axis=1), rhs_ref[slot])
        ms = []
        for e in range(SSD_HPG):
            k = SSD_HPG + e
            w = jnp.exp(expo[:, e * LANES:(e + 1) * LANES]) + jnp.exp(expo[:, k * LANES:(k + 1) * LANES])
            ms.append((cb * w).astype(BF16))
        ys = []
        for p in range(SSD_HPG // 2):
            lhs = jnp.concatenate([ms[2 * p], ms[2 * p + 1]], axis=1)
            xp = xb[:, p * LANES:(p + 1) * LANES]
            zero = jnp.zeros_like(xp)
            rhs = jnp.concatenate([jnp.where(lane < SSD_HEADDIM, xp, zero),
                                   jnp.where(lane >= SSD_HEADDIM, xp, zero)], axis=0)
            ys.append(_dot(lhs, rhs))
        y = jnp.concatenate(ys, axis=1)
        yoff = _dot(cc, prev_ref[c]) * eexp_ref[c]
        y = y + yoff[:, 0:gw] + yoff[:, gw:]
        y = (y + dskip_ref[...] * xb.astype(F32)) * z_ref[0, rows, :].astype(F32)
        y_ref[0, rows, :] = y.astype(BF16)

    def chunk_step(io, carry):
        for j in range(SSD_UNROLL):
            chunk_out(io * SSD_UNROLL + j, j)
        return carry

    lax.fori_loop(0, nc // SSD_UNROLL, chunk_step, 0)


def _ssd(xbc, dtx, dtt, alog_g, alogt_g, dskip_x, zs):
    bsz, s, _ = xbc.shape
    q = SSD_CHUNK
    nc = s // q
    gw = SSD_HPG * SSD_HEADDIM
    nb = SSD_WIDTH // SSD_STATE
    consts = _ssd_constants()
    const_specs = [pl.BlockSpec(m.shape, lambda b, g, nd=m.ndim: (0,) * nd) for m in consts]
    return pl.pallas_call(
        _ssd_body,
        grid=(bsz, SSD_GROUPS),
        in_specs=[pl.BlockSpec((1, s, gw), lambda b, g: (b, 0, g)),
                  pl.BlockSpec((1, s, SSD_STATE), lambda b, g: (b, 0, nb + g)),
                  pl.BlockSpec((1, s, SSD_STATE), lambda b, g: (b, 0, nb + SSD_GROUPS + g)),
                  pl.BlockSpec((1, s, LANES), lambda b, g: (b, 0, g)),
                  pl.BlockSpec((1, 1, 2, nc * SSD_HPG, q), lambda b, g: (b, g, 0, 0, 0)),
                  pl.BlockSpec((1, 1, LANES), lambda b, g: (g, 0, 0)),
                  pl.BlockSpec((1, 2, nc * SSD_HPG, q), lambda b, g: (g, 0, 0, 0)),
                  pl.BlockSpec((1, gw), lambda b, g: (0, g)),
                  pl.BlockSpec((1, s, gw), lambda b, g: (b, 0, g)),
                  *const_specs],
        out_specs=pl.BlockSpec((1, s, gw), lambda b, g: (b, 0, g)),
        out_shape=jax.ShapeDtypeStruct((bsz, s, SSD_WIDTH), BF16),
        scratch_shapes=[pltpu.VMEM((nc, q, LANES), BF16),
                        pltpu.VMEM((nc, q, 2 * gw), F32),
                        pltpu.VMEM((nc, 1, 2 * gw), F32),
                        pltpu.VMEM((nc, SSD_STATE, gw), F32),
                        pltpu.VMEM((nc, SSD_STATE, 2 * gw), BF16),
                        pltpu.VMEM((nc, 2 * SUBLANES, 2 * SSD_HPG * LANES), BF16),
                        pltpu.VMEM((SSD_UNROLL, 2 * q, 2 * SSD_HPG * LANES), BF16)],
        compiler_params=_params("arbitrary", "arbitrary"),
        name="ssd",
    )(xbc, xbc, xbc, dtx, dtt, alog_g, alogt_g, dskip_x, zs, *consts)


TAIL_TM = 512
TAIL_A_TM = 256


def _tail_a_body(ya_ref, y_ref, g_ref, ng_ref, wa_ref, ws_ref, o_ref):
    d = o_ref.shape[1]
    ya = _dot(ya_ref[...], wa_ref[...])
    y = y_ref[...].astype(F32)
    yn = y * lax.rsqrt(jnp.mean(y * y, axis=-1, keepdims=True) + EPS) * ng_ref[...]
    ys = _dot(yn.astype(BF16), ws_ref[...])
    ga = g_ref[:, 0:d].astype(F32)
    gs = g_ref[:, d:2 * d].astype(F32)
    o_ref[...] = (ga * ya + gs * ys).astype(BF16)


def _tail_a(ya_in, y, gates, norm_g, wa, ws):
    n, d = ya_in.shape[0], wa.shape[1]
    tm = TAIL_A_TM
    resident = functools.partial(pl.BlockSpec, pipeline_mode=pl.Buffered(1))
    return pl.pallas_call(
        _tail_a_body,
        grid=(n // tm,),
        in_specs=[pl.BlockSpec((tm, ya_in.shape[1]), lambda i: (i, 0)),
                  pl.BlockSpec((tm, y.shape[1]), lambda i: (i, 0)),
                  pl.BlockSpec((tm, gates.shape[1]), lambda i: (i, 0)),
                  pl.BlockSpec((1, y.shape[1]), lambda i: (0, 0)),
                  resident(wa.shape, lambda i: (0, 0)),
                  resident(ws.shape, lambda i: (0, 0))],
        out_specs=pl.BlockSpec((tm, d), lambda i: (i, 0)),
        out_shape=jax.ShapeDtypeStruct((n, d), BF16),
        compiler_params=_params("arbitrary"),
        name="tail_a",
    )(ya_in, y, gates, norm_g, wa, ws)


def _tail_b_body(m_ref, x_ref, ada_ref, w_ref, fg_ref, o_ref):
    t = _dot(m_ref[...], w_ref[...])
    xn = x_ref[...] + ada_ref[0, 2:3, :] * t
    o_ref[...] = xn * lax.rsqrt(jnp.mean(xn * xn, axis=-1, keepdims=True) + EPS) * fg_ref[...]


def _tail_b(merged, x2, ada3, w_out, final_g, seq):
    n, d = x2.shape
    tm = TAIL_TM
    per = seq // tm
    resident = functools.partial(pl.BlockSpec, pipeline_mode=pl.Buffered(1))
    return pl.pallas_call(
        _tail_b_body,
        grid=(n // tm,),
        in_specs=[pl.BlockSpec((tm, d), lambda i: (i, 0)),
                  pl.BlockSpec((tm, d), lambda i: (i, 0)),
                  pl.BlockSpec((1, 3, d), lambda i: (i // per, 0, 0)),
                  resident(w_out.shape, lambda i: (0, 0)),
                  pl.BlockSpec((1, d), lambda i: (0, 0))],
        out_specs=pl.BlockSpec((tm, d), lambda i: (i, 0)),
        out_shape=jax.ShapeDtypeStruct((n, d), F32),
        compiler_params=_params("arbitrary"),
        name="tail_b",
    )(merged, x2, ada3, w_out, final_g)


def _layer(x, ada3, pos_tabs, norm_g, w_in, conv_w, conv_b, dt_bias, a_log, d_skip, ssd_norm_g,
           w_br_attn, w_br_ssd, w_out, out_g):
    bsz, s, d = x.shape
    cos, sin = pos_tabs
    hs = _modulated_norm(x, ada3, norm_g[None])
    qkv = [_proj_qkv(hs[g], w_in, cos, sin, g) for g in range(len(DILATIONS))]
    h = hs[0]
    tn = PROJ_TN
    zs_a = _proj_call(_proj_silu_body, h, w_in, COL_Z, ATTN_WIDTH // QKV_TN, QKV_TN, ATTN_WIDTH, BF16,
                      name="proj_za")
    zs_s = _proj_call(_proj_silu_body, h, w_in, COL_Z + ATTN_WIDTH, SSD_WIDTH // WIDE_TN, WIDE_TN, SSD_WIDTH,
                      BF16, name="proj_zs")
    xbc = _proj_call(
        _proj_conv_body, h, w_in, COL_XBC, CONV_CH // tn, tn, CONV_CH, BF16,
        extra=(conv_w, conv_b[None]),
        extra_specs=(pl.BlockSpec((CONV_WIDTH, tn), lambda j, b: (0, j)),
                     pl.BlockSpec((1, tn), lambda j, b: (0, j))),
        scratch=(*[pltpu.VMEM((tn // LANES, CONV_PAD + CONV_ROWS + CONV_TAIL, LANES), F32)] * CONV_CHUNKS,
                 *[pltpu.VMEM((tn // (2 * LANES), SUBLANES * CONV_STRIDE, LANES), jnp.uint32)] * CONV_CHUNKS),
        name="proj_xbc")
    n_dt = 2 * SSD_HEADS
    dtx, dtt = _proj_dt(h, w_in, dt_bias)
    gates = _proj_call(_proj_sigmoid_body, h, w_in[:, COL_GATE:], 0, 2 * d // WIDE_TN, WIDE_TN, 2 * d,
                       BF16, name="proj_gates")

    ya_in = _attention(qkv, zs_a)

    a_flat = a_log.reshape(1, n_dt)
    rolled = {k: jnp.roll(a_flat, -k * SSD_HPG, axis=1) for k in range(-2, SSD_GROUPS)}
    m0, m1 = _rep_masks(a_flat.shape)
    alog_g = jnp.stack([_replicate_heads(rolled, g, m0, m1) for g in range(SSD_GROUPS)])
    nc = s // SSD_CHUNK
    alogt_g = a_log.reshape(2, SSD_GROUPS, SSD_HPG).transpose(1, 0, 2)
    alogt_g = jnp.broadcast_to(alogt_g[:, :, None, :, None], (SSD_GROUPS, 2, nc, SSD_HPG, SSD_CHUNK))
    alogt_g = alogt_g.reshape(SSD_GROUPS, 2, nc * SSD_HPG, SSD_CHUNK)
    dskip_x = jnp.repeat(d_skip, SSD_HEADDIM)[None]
    y = _ssd(xbc, dtx, dtt, alog_g, alogt_g, dskip_x, zs_s)

    n = bsz * s
    merged = _tail_a(ya_in.reshape(n, ATTN_WIDTH), y.reshape(n, SSD_WIDTH),
                     gates.reshape(n, gates.shape[-1]), ssd_norm_g[None],
                     w_br_attn.astype(BF16), w_br_ssd.astype(BF16))
    out = _tail_b(merged, x.reshape(n, d), ada3, w_out.astype(BF16), out_g, s)
    return out.reshape(bsz, s, d)


def kernel(x, c, positions, norm_g, w_ada, b_ada, w_in, conv_w, conv_b, dt_bias, a_log, d_skip,
           ssd_norm_g, w_br_attn, w_br_ssd, w_out, final_g):
    bsz, s, d = x.shape
    depth = w_in.shape[0]
    inv = ROPE_THETA ** (-jnp.arange(0, HEAD_DIM, 2, dtype=F32) / HEAD_DIM)
    inv2 = jnp.concatenate([inv, inv])[None]
    cos, sin = _rope_tables(positions[..., None], inv2)
    pos_tabs = (cos.reshape(-1, s, HEAD_DIM), sin.reshape(-1, s, HEAD_DIM))
    for i in range(depth):
        ada3 = _ada(c, w_ada[i], b_ada[i][None]).reshape(bsz, 3, d)
        assert depth == 1
        x = _layer(x, ada3, pos_tabs, norm_g[i], w_in[i], conv_w[i], conv_b[i], dt_bias[i], a_log[i],
                   d_skip[i], ssd_norm_g[i], w_br_attn[i], w_br_ssd[i], w_out[i], final_g[None])
    return x
```

```python
import functools

import numpy as np
import jax
import jax.numpy as jnp
from jax import lax
from jax.experimental import pallas as pl
from jax.experimental.pallas import tpu as pltpu

F32 = jnp.float32
BF16 = jnp.bfloat16

D_MODEL = 2048
HEAD_DIM = 128
ATTN_HEADS = 12
DILATIONS = (1, 4, 16)
N_SIDE = 64
ATTN_WIDTH = ATTN_HEADS * HEAD_DIM
ROPE_THETA = 10000.0
SSD_WIDTH = 2 * D_MODEL
SSD_HEADDIM = 64
SSD_GROUPS = 8
SSD_HEADS = SSD_WIDTH // SSD_HEADDIM
SSD_HPG = SSD_HEADS // SSD_GROUPS
SSD_STATE = 128
SSD_CHUNK = 128
CONV_WIDTH = 5
CONV_CH = SSD_WIDTH + 2 * SSD_GROUPS * SSD_STATE
EPS = 1e-6
QKV_COLS = 3 * len(DILATIONS) * ATTN_WIDTH
COL_Z = QKV_COLS
COL_XBC = COL_Z + ATTN_WIDTH + SSD_WIDTH
COL_DT = COL_XBC + CONV_CH
COL_GATE = COL_DT + 2 * SSD_HEADS
IN_COLS = COL_GATE + 2 * D_MODEL

LANES = 128
PROJ_TN = 512
QKV_TN = 768
WIDE_TN = 1024
V7X_VMEM_LIMIT_BYTES = 56 * 1024 * 1024


def _params(*sem):
    return pltpu.CompilerParams(dimension_semantics=sem, vmem_limit_bytes=V7X_VMEM_LIMIT_BYTES)


def _dot(a, b):
    return jnp.dot(a, b, preferred_element_type=F32)


def _sigmoid(x):
    return 0.5 * jnp.tanh(0.5 * x) + 0.5


def _split3(v):
    hi = v.astype(BF16)
    r = v - hi.astype(F32)
    mid = r.astype(BF16)
    lo = (r - mid.astype(F32)).astype(BF16)
    return hi, mid, lo


def _ada_body(c_ref, w_ref, b_ref, o_ref):
    o_ref[...] = _dot(c_ref[...].astype(BF16), w_ref[...].astype(BF16)) + b_ref[...]


def _ada(c, w, bias):
    bsz, d = c.shape
    n = w.shape[1]
    tn = 768
    return pl.pallas_call(
        _ada_body,
        grid=(n // tn,),
        in_specs=[pl.BlockSpec((bsz, d), lambda j: (0, 0)),
                  pl.BlockSpec((d, tn), lambda j: (0, j)),
                  pl.BlockSpec((1, tn), lambda j: (0, j))],
        out_specs=pl.BlockSpec((bsz, tn), lambda j: (0, j)),
        out_shape=jax.ShapeDtypeStruct((bsz, n), F32),
        compiler_params=_params("arbitrary"),
        name="ada",
    )(c, w, bias)


def _rope_body(pos_ref, inv_ref, cos_ref, sin_ref):
    s = pos_ref.shape[1]
    ang = pos_ref[0].astype(F32) * inv_ref[...]
    lane = lax.broadcasted_iota(jnp.int32, ang.shape, 1)
    sin = jnp.sin(ang)
    cos_ref[0, 0] = jnp.cos(ang)
    sin_ref[0, 0] = jnp.where(lane < HEAD_DIM // 2, -sin, sin)
    for tab in (cos_ref, sin_ref):
        for g, dil in enumerate(DILATIONS[1:], start=1):
            sub = s // dil
            for r in range(dil):
                tab[g, 0, r * sub:(r + 1) * sub, :] = tab[0, 0, pl.ds(r, sub, stride=dil), :]


def _rope_tables(pos, inv2):
    bsz, s, _ = pos.shape
    ng = len(DILATIONS)
    spec = pl.BlockSpec((ng, 1, s, HEAD_DIM), lambda b: (0, b, 0, 0))
    return pl.pallas_call(
        _rope_body,
        grid=(bsz,),
        in_specs=[pl.BlockSpec((1, s, 1), lambda b: (b, 0, 0)),
                  pl.BlockSpec((1, HEAD_DIM), lambda b: (0, 0))],
        out_specs=[spec, spec],
        out_shape=[jax.ShapeDtypeStruct((ng, bsz, s, HEAD_DIM), F32)] * 2,
        compiler_params=_params("arbitrary"),
        name="rope_tables",
    )(pos, inv2)


H_TILE = 512


def _h_body(x_ref, ada_ref, g_ref, h1_ref, h4_ref, h16_ref, hs_ref):
    x = x_ref[0]
    ms = jnp.mean(x * x, axis=-1, keepdims=True)
    xn = x * lax.rsqrt(ms + EPS) * g_ref[...]
    h = xn * (1.0 + ada_ref[0, 1:2, :]) + ada_ref[0, 0:1, :]
    h1_ref[0] = h.astype(BF16)
    for cb in range(h.shape[1] // LANES):
        cols = slice(cb * LANES, (cb + 1) * LANES)
        hs_ref[cb] = h[:, cols]
        for r in range(4):
            h4_ref[0, r, :, cols] = hs_ref[cb, pl.ds(r, H_TILE // 4, stride=4), :].astype(BF16)
        for r in range(16):
            h16_ref[0, r, :, cols] = hs_ref[cb, pl.ds(r, H_TILE // 16, stride=16), :].astype(BF16)


def _modulated_norm(x, ada3, g):
    bsz, s, d = x.shape
    t = H_TILE
    outs = pl.pallas_call(
        _h_body,
        grid=(bsz, s // t),
        in_specs=[pl.BlockSpec((1, t, d), lambda b, i: (b, i, 0)),
                  pl.BlockSpec((1, 3, d), lambda b, i: (b, 0, 0)),
                  pl.BlockSpec((1, d), lambda b, i: (0, 0))],
        out_specs=[pl.BlockSpec((1, t, d), lambda b, i: (b, i, 0)),
                   pl.BlockSpec((1, 4, t // 4, d), lambda b, i: (b, 0, i, 0)),
                   pl.BlockSpec((1, 16, t // 16, d), lambda b, i: (b, 0, i, 0))],
        out_shape=[jax.ShapeDtypeStruct((bsz, s, d), BF16),
                   jax.ShapeDtypeStruct((bsz, 4, s // 4, d), BF16),
                   jax.ShapeDtypeStruct((bsz, 16, s // 16, d), BF16)],
        scratch_shapes=[pltpu.VMEM((d // LANES, t, LANES), F32)],
        compiler_params=_params("arbitrary", "arbitrary"),
        name="modulated_norm",
    )(x, ada3, g)
    h1, h4, h16 = outs
    return h1, h4.reshape(bsz, s, d), h16.reshape(bsz, s, d)


PROJ_ROWS = 512


def _cast_weight_tile(w_ref, wbf_ref):
    @pl.when(pl.program_id(1) == 0)
    def _():
        wbf_ref[...] = w_ref[...].astype(BF16)


PROJ_SPLIT = (PROJ_ROWS,) * 4


def _proj_chunks(h_ref, wbf_ref, epilogue, sizes=PROJ_SPLIT):
    assert sum(sizes) == h_ref.shape[1]
    start = 0
    for size in sizes:
        rows = slice(start, start + size)
        epilogue(rows, _dot(h_ref[0, rows, :], wbf_ref[...]))
        start += size


def _proj_rope_body(h_ref, w_ref, cos_ref, sin_ref, o_ref, wbf_ref):
    _cast_weight_tile(w_ref, wbf_ref)
    j = pl.program_id(0)
    heads = o_ref.shape[1]
    n_rot = 2 * ATTN_HEADS // heads

    def rotate(rows, acc):
        cos = cos_ref[0, rows, :]
        sin = sin_ref[0, rows, :]
        for hh in range(heads):
            t = acc[:, hh * HEAD_DIM:(hh + 1) * HEAD_DIM]
            rot = t * cos + pltpu.roll(t, HEAD_DIM // 2, 1) * sin
            o_ref[0, hh, rows, :] = rot.astype(BF16)

    def plain(rows, acc):
        for hh in range(heads):
            o_ref[0, hh, rows, :] = acc[:, hh * HEAD_DIM:(hh + 1) * HEAD_DIM].astype(BF16)

    @pl.when(j < n_rot)
    def _():
        _proj_chunks(h_ref, wbf_ref, rotate)

    @pl.when(j >= n_rot)
    def _():
        _proj_chunks(h_ref, wbf_ref, plain)


def _proj_silu_body(h_ref, w_ref, o_ref, wbf_ref):
    _cast_weight_tile(w_ref, wbf_ref)

    def epilogue(rows, acc):
        o_ref[0, rows, :] = (acc * _sigmoid(acc)).astype(BF16)

    _proj_chunks(h_ref, wbf_ref, epilogue)


def _proj_sigmoid_body(h_ref, w_ref, o_ref, wbf_ref):
    _cast_weight_tile(w_ref, wbf_ref)

    def epilogue(rows, acc):
        o_ref[0, rows, :] = _sigmoid(acc).astype(BF16)

    _proj_chunks(h_ref, wbf_ref, epilogue)


CONV_PAD = 8
SUBLANES = 8
CONV_ROWS = 512
CONV_STRIDE = CONV_ROWS // SUBLANES + 1
CONV_TAIL = 3 * SUBLANES
CONV_CHUNKS = 4


def _proj_conv_body(h_ref, w_ref, cw_ref, cb_ref, o_ref, wbf_ref, *scratch):
    pads, stages = scratch[:CONV_CHUNKS], scratch[CONV_CHUNKS:]
    n_cb = PROJ_TN // LANES
    half = (CONV_WIDTH - 1) // 2
    end = CONV_PAD + CONV_ROWS
    _cast_weight_tile(w_ref, wbf_ref)
    pads[0][:, 0:CONV_PAD, :] = jnp.zeros((n_cb, CONV_PAD, LANES), F32)
    pads[-1][:, end:end + CONV_TAIL, :] = jnp.zeros((n_cb, CONV_TAIL, LANES), F32)

    def conv_chunk(m):
        base = m * CONV_ROWS
        for pair in range(n_cb // 2):
            cbs = (2 * pair, 2 * pair + 1)
            cols = [slice(cb * LANES, (cb + 1) * LANES) for cb in cbs]
            taps = [[cw_ref[k:k + 1, c] for k in range(CONV_WIDTH)] for c in cols]
            bias = [cb_ref[:, c] for c in cols]

            def tile(cb, i):
                return pads[m][cb, pl.ds(CONV_PAD + i, SUBLANES, stride=CONV_STRIDE), :]

            wins = [[tile(cb, i) for i in range(-half, half)] for cb in cbs]
            for i in range(CONV_STRIDE):
                acts = []
                for n, cb in enumerate(cbs):
                    win = wins[n]
                    win.append(tile(cb, i + half))
                    out = bias[n] + taps[n][0] * win[0]
                    for k in range(1, CONV_WIDTH):
                        out = out + taps[n][k] * win[k]
                    acts.append(out * _sigmoid(out))
                    win.pop(0)
                stages[m][pair, pl.ds(i, SUBLANES, stride=CONV_STRIDE), :] = pltpu.pack_elementwise(
                    acts, packed_dtype=BF16)
            packed = stages[m][pair, 0:CONV_ROWS, :]
            for n, c in enumerate(cols):
                o_ref[0, base:base + CONV_ROWS, c] = pltpu.unpack_elementwise(
                    packed, index=n, packed_dtype=BF16, unpacked_dtype=F32).astype(BF16)

    def epilogue(rows, acc):
        m = rows.start // CONV_ROWS
        for cb in range(n_cb):
            a = acc[:, cb * LANES:(cb + 1) * LANES]
            pads[m][cb, CONV_PAD:end, :] = a
            if m > 0:
                pads[m - 1][cb, end:end + CONV_TAIL, :] = a[0:CONV_TAIL]
            if m + 1 < CONV_CHUNKS:
                pads[m + 1][cb, 0:CONV_PAD, :] = a[CONV_ROWS - CONV_PAD:CONV_ROWS]
        if m > 0:
            conv_chunk(m - 1)

    _proj_chunks(h_ref, wbf_ref, epilogue, sizes=(CONV_ROWS,) * CONV_CHUNKS)
    conv_chunk(CONV_CHUNKS - 1)


def _rep_masks(shape):
    lane = lax.broadcasted_iota(jnp.int32, shape, len(shape) - 1) & (SSD_HEADS - 1)
    return lane < SSD_HPG, lane < 2 * SSD_HPG


def _packed_split(v):
    hi, mid, lo = _split3(v)
    m0, m1 = _rep_masks(v.shape)
    return jnp.where(m0, hi, jnp.where(m1, mid, lo))


def _replicate_heads(rolled, g, m0, m1):
    return jnp.where(m0, rolled[g], jnp.where(m1, rolled[g - 1], rolled[g - 2]))


def _proj_dt_body(h_ref, w_ref, bias_ref, o_ref, ot_ref, wbf_ref):
    s = h_ref.shape[1]
    _cast_weight_tile(w_ref, wbf_ref)
    acc = _dot(h_ref[0], wbf_ref[...]) + bias_ref[...]
    sp = jnp.maximum(acc, 0.0) + jnp.log1p(jnp.exp(-jnp.abs(acc)))
    n = 2 * SSD_HEADS
    m0, m1 = _rep_masks(sp.shape)
    rolled = {k: (sp if k == 0 else pltpu.roll(sp, (-k * SSD_HPG) % n, 1))
              for k in range(-2, SSD_GROUPS)}
    for g in range(SSD_GROUPS):
        o_ref[0, :, g * n:(g + 1) * n] = _replicate_heads(rolled, g, m0, m1)
    for c in range(s // SSD_CHUNK):
        t = sp[c * SSD_CHUNK:(c + 1) * SSD_CHUNK, :].T
        for g in range(SSD_GROUPS):
            for d in range(2):
                r0 = d * SSD_HEADS + g * SSD_HPG
                ot_ref[0, g, d, c * SSD_HPG:(c + 1) * SSD_HPG, :] = t[r0:r0 + SSD_HPG, :]


def _proj_dt(h, w_in, dt_bias):
    bsz, s, d = h.shape
    n = 2 * SSD_HEADS
    nc = s // SSD_CHUNK
    blk0 = COL_DT // n
    return pl.pallas_call(
        _proj_dt_body,
        grid=(1, bsz),
        in_specs=[pl.BlockSpec((1, s, d), lambda j, b: (b, 0, 0)),
                  pl.BlockSpec((d, n), lambda j, b: (0, blk0)),
                  pl.BlockSpec((1, n), lambda j, b: (0, 0))],
        out_specs=[pl.BlockSpec((1, s, SSD_GROUPS * n), lambda j, b: (b, 0, 0)),
                   pl.BlockSpec((1, SSD_GROUPS, 2, nc * SSD_HPG, SSD_CHUNK), lambda j, b: (b, 0, 0, 0, 0))],
        out_shape=[jax.ShapeDtypeStruct((bsz, s, SSD_GROUPS * n), F32),
                   jax.ShapeDtypeStruct((bsz, SSD_GROUPS, 2, nc * SSD_HPG, SSD_CHUNK), F32)],
        scratch_shapes=[pltpu.VMEM((d, n), BF16)],
        compiler_params=_params("arbitrary", "arbitrary"),
        name="proj_dt",
    )(h, w_in, dt_bias.reshape(1, n))


def _proj_call(body, h, w_in, col0, n_tiles, tn, out_cols, out_dtype, extra=(), extra_specs=(),
               scratch=(), name="proj"):
    bsz, s, d = h.shape
    blk0, rem = divmod(col0, tn)
    assert rem == 0
    return pl.pallas_call(
        body,
        grid=(n_tiles, bsz),
        in_specs=[pl.BlockSpec((1, s, d), lambda j, b: (b, 0, 0)),
                  pl.BlockSpec((d, tn), lambda j, b: (0, blk0 + j)),
                  *extra_specs],
        out_specs=pl.BlockSpec((1, s, tn), lambda j, b: (b, 0, j)),
        out_shape=jax.ShapeDtypeStruct((bsz, s, out_cols), out_dtype),
        scratch_shapes=[pltpu.VMEM((d, tn), BF16), *scratch],
        compiler_params=_params("arbitrary", "arbitrary"),
        name=name,
    )(h, w_in, *extra)


def _proj_qkv(h, w_in, cos, sin, group):
    bsz, s, d = h.shape
    tn = QKV_TN
    tps = ATTN_WIDTH // tn
    n_groups = len(DILATIONS)

    def wcol(j, b):
        return (0, (j // tps) * (n_groups * tps) + group * tps + j % tps)

    tab = pl.BlockSpec((1, s, HEAD_DIM), lambda j, b: (group * bsz + b, 0, 0))
    return pl.pallas_call(
        _proj_rope_body,
        grid=(3 * tps, bsz),
        in_specs=[pl.BlockSpec((1, s, d), lambda j, b: (b, 0, 0)),
                  pl.BlockSpec((d, tn), wcol), tab, tab],
        out_specs=pl.BlockSpec((1, tn // HEAD_DIM, s, HEAD_DIM), lambda j, b: (b, j, 0, 0)),
        out_shape=jax.ShapeDtypeStruct((bsz, 3 * ATTN_HEADS, s, HEAD_DIM), BF16),
        scratch_shapes=[pltpu.VMEM((d, tn), BF16)],
        compiler_params=_params("arbitrary", "arbitrary"),
        name=f"proj_qkv{group}",
    )(h, w_in, cos, sin)


ATT_BLK = 128


ATT_UNROLL = 16


def _attn_body(q1, k1, v1, q2, k2, v2, q3, k3, v3, z_ref, o_ref, on_ref, ls_ref, va_ref, bias_ref):
    s = o_ref.shape[1]
    scale = HEAD_DIM ** -0.5
    qi = lax.broadcasted_iota(jnp.int32, (ATT_BLK, 2 * ATT_BLK), 0)
    ki = lax.broadcasted_iota(jnp.int32, (ATT_BLK, 2 * ATT_BLK), 1)
    for t in range(3):
        bias_ref[t] = jnp.where(jnp.abs(qi - ki + t * N_SIDE) <= N_SIDE, 0.0, -jnp.inf)
    va_ref[:, :, HEAD_DIM:] = jnp.ones((len(DILATIONS), s, HEAD_DIM), BF16)

    def block(g, q_ref, k_ref, q0, k0, bias, nk):
        q = q_ref[0, 0, pl.ds(q0, ATT_BLK), :]
        kw = k_ref[0, 0, pl.ds(k0, nk), :]
        sc = lax.dot_general(q, kw, (((1,), (1,)), ((), ())), preferred_element_type=F32) * scale + bias
        m = jnp.max(sc, axis=-1, keepdims=True)
        p = jnp.exp(sc - m)
        pv = _dot(p.astype(BF16), va_ref[g, pl.ds(k0, nk), :])
        den = pv[:, HEAD_DIM:]
        return pv[:, :HEAD_DIM] / den, m + jnp.log(den)

    groups = ((q1, k1, v1), (q2, k2, v2), (q3, k3, v3))
    for g, (q_ref, k_ref, v_ref) in enumerate(groups):
        dil = DILATIONS[g]
        sub = s // dil
        nblk = sub // ATT_BLK
        va_ref[g, :, :HEAD_DIM] = v_ref[0, 0]
        for r in range(dil):
            if nblk == 1:
                o, lse = block(g, q_ref, k_ref, r * sub, r * sub, bias_ref[0, :, :ATT_BLK], ATT_BLK)
                on_ref[g, pl.ds(r, ATT_BLK, stride=dil), :] = o
                ls_ref[g, pl.ds(r, ATT_BLK, stride=dil), :] = lse
            else:
                def one(i, q_ref=q_ref, k_ref=k_ref, r=r, sub=sub, dil=dil, g=g):
                    loc = i * ATT_BLK
                    kloc = jnp.clip(loc - N_SIDE, 0, sub - 2 * ATT_BLK)
                    q0 = pl.multiple_of(r * sub + loc, ATT_BLK)
                    k0 = pl.multiple_of(r * sub + kloc, N_SIDE)
                    bias = bias_ref[lax.shift_right_logical(loc - kloc, N_SIDE.bit_length() - 1)]
                    o, lse = block(g, q_ref, k_ref, q0, k0, bias, 2 * ATT_BLK)
                    if dil == 1:
                        rows = pl.ds(q0, ATT_BLK)
                    else:
                        rows = pl.ds(loc * dil + r, ATT_BLK, stride=dil)
                    on_ref[g, rows, :] = o
                    ls_ref[g, rows, :] = lse

                un = min(ATT_UNROLL, nblk)

                def step(io, carry, one=one, un=un):
                    for j in range(un):
                        one(io * un + j)
                    return carry

                lax.fori_loop(0, nblk // un, step, 0)

    l0, l1, l2 = ls_ref[0], ls_ref[1], ls_ref[2]
    mx = jnp.maximum(jnp.maximum(l0, l1), l2)
    e0, e1, e2 = jnp.exp(l0 - mx), jnp.exp(l1 - mx), jnp.exp(l2 - mx)
    o = (e0 * on_ref[0] + e1 * on_ref[1] + e2 * on_ref[2]) / (e0 + e1 + e2)
    o_ref[0] = (o * z_ref[0].astype(F32)).astype(BF16)


def _attention(qkv, zs):
    bsz, _, s, _ = qkv[0].shape
    specs = []
    args = []
    for g in range(len(DILATIONS)):
        for sec in range(3):
            specs.append(pl.BlockSpec((1, 1, s, HEAD_DIM),
                                      lambda b, h, sec=sec: (b, sec * ATTN_HEADS + h, 0, 0)))
            args.append(qkv[g])
    specs.append(pl.BlockSpec((1, s, HEAD_DIM), lambda b, h: (b, 0, h)))
    args.append(zs)
    return pl.pallas_call(
        _attn_body,
        grid=(bsz, ATTN_HEADS),
        in_specs=specs,
        out_specs=pl.BlockSpec((1, s, HEAD_DIM), lambda b, h: (b, 0, h)),
        out_shape=jax.ShapeDtypeStruct((bsz, s, ATTN_WIDTH), BF16),
        scratch_shapes=[pltpu.VMEM((3, s, HEAD_DIM), F32), pltpu.VMEM((3, s, HEAD_DIM), F32),
                        pltpu.VMEM((len(DILATIONS), s, 2 * HEAD_DIM), BF16),
                        pltpu.VMEM((3, ATT_BLK, 2 * ATT_BLK), F32)],
        compiler_params=_params("arbitrary", "arbitrary"),
        name="dilated_attention",
    )(*args)


SSD_UNROLL = 8
SSD_BIG = 1e30


def _ssd_constants():
    q, gw = SSD_CHUNK, SSD_HPG * SSD_HEADDIM
    i = np.arange(q)
    low = i[None, :] <= i[:, None]
    upp = i[None, :] >= i[:, None]
    tri = np.stack([low, upp])
    tri3 = np.stack([np.tile(upp, (3, 1)), np.tile(low, (3, 1))])
    c = np.arange(LANES)[:, None]
    in_f = c < 3 * SSD_HPG
    in_b = (c >= SSD_HEADS) & (c < SSD_HEADS + 3 * SSD_HPG)
    head = np.arange(gw)[None, :] // SSD_HEADDIM
    efb = np.concatenate([in_f & (c % SSD_HPG == head), in_b & (c % SSD_HPG == head)], axis=1)
    k = np.arange(2 * SSD_HPG * LANES)[None, :] // LANES
    ecol = np.where(k < SSD_HPG, in_f & (c % SSD_HPG == k), in_b & (c % SSD_HPG == k - SSD_HPG))
    src = np.arange(2 * SSD_HPG * LANES)[None, :] % LANES
    dst = i[:, None]
    keep = np.where(k < SSD_HPG, dst >= src, dst <= src)
    rhs0 = np.concatenate([ecol.astype(np.float32), np.where(keep, 0.0, -SSD_BIG)], axis=0)
    return tuple(jnp.asarray(m, BF16) for m in (tri, tri3, efb, rhs0))


SSD_DYN_ROW0 = 2 * SSD_HPG
SSD_ONE_LANE0 = 3 * SSD_HPG


def _ssd_body(x_ref, b_ref, c_ref, dt_ref, dtt_ref, alog_ref, alogt_ref, dskip_ref, z_ref,
              tri_ref, tri3_ref, efb_ref, rhs0_ref, y_ref,
              pcum_ref, eexp_ref, cdx_ref, stb_ref, prev_ref, rdyn_ref, rhs_ref):
    q = SSD_CHUNK
    s = x_ref.shape[1]
    nc = s // q
    gw = SSD_HPG * SSD_HEADDIM
    fwd_lane = lax.broadcasted_iota(jnp.int32, (q, LANES), 1) < SSD_HEADS
    lane = lax.broadcasted_iota(jnp.int32, (q, LANES), 1)

    a_row = -jnp.exp(alog_ref[0])
    dt_all = dt_ref[0]
    a_all = dt_all * a_row
    a_wide = jnp.concatenate([a_all[c * q:(c + 1) * q] for c in range(nc)], axis=1)
    fwd_wide = (lax.broadcasted_iota(jnp.int32, a_wide.shape, 1) & (LANES - 1)) < SSD_HEADS
    both = jnp.concatenate([tri_ref[0], tri_ref[1]], axis=1)
    cum_wide = jnp.zeros(a_wide.shape, F32)
    for part in _split3(a_wide):
        zero = jnp.zeros_like(part)
        stacked = jnp.concatenate([jnp.where(fwd_wide, part, zero), jnp.where(fwd_wide, zero, part)], axis=0)
        cum_wide = cum_wide + _dot(both, stacked)

    dt_t = dtt_ref[0, 0]
    a_t = dt_t * -jnp.exp(alogt_ref[0])
    log_dt = jnp.where(dt_t > 0.0, jnp.log(dt_t), -SSD_BIG)
    neg_parts = []
    for d in range(2):
        parts = jnp.concatenate(_split3(a_t[d]), axis=1)
        src_term = _dot(parts, tri3_ref[d]) - log_dt[d]
        neg_parts.append([p.astype(F32) for p in _split3(-src_term)])
    sub = lax.broadcasted_iota(jnp.int32, (2 * SUBLANES, LANES), 0)
    for c in range(nc):
        blocks = []
        for k in range(2 * SSD_HPG):
            d, e = divmod(k, SSD_HPG)
            blk = jnp.where(sub == e, 1.0, 0.0) if d == 0 else jnp.zeros(sub.shape, F32)
            for p in range(3):
                r = c * SSD_HPG + e
                blk = jnp.where(sub == SUBLANES + p, neg_parts[d][p][r:r + 1, :], blk)
            blocks.append(blk)
        rdyn_ref[c] = jnp.concatenate(blocks, axis=1).astype(BF16)
    for j in range(SSD_UNROLL):
        rhs_ref[j] = rhs0_ref[...]

    bt = b_ref[0].T
    state_f = jnp.zeros((SSD_STATE, gw), F32)
    for c in range(nc):
        cum = cum_wide[:, c * LANES:(c + 1) * LANES]
        dt = dt_all[c * q:(c + 1) * q]
        ref = jnp.where(fwd_lane[0:1], cum[q - 1:q], cum[0:1])
        one_lane = (lane >= SSD_ONE_LANE0) & (lane < SSD_ONE_LANE0 + 3)
        pcum_ref[c] = jnp.where(one_lane, 1.0, _packed_split(cum).astype(F32)).astype(BF16)
        eexp_ref[c] = _dot(_packed_split(jnp.exp(cum)), efb_ref[...])
        wexp = _dot(_packed_split(dt * jnp.exp(ref - cum)), efb_ref[...])
        cdx = _dot(_packed_split(jnp.broadcast_to(jnp.exp(ref), (8, LANES))), efb_ref[...])[0:1]
        cdx_ref[c] = cdx
        xb = x_ref[0, c * q:(c + 1) * q, :].astype(F32)
        xw = (jnp.concatenate([xb, xb], axis=1) * wexp).astype(BF16)
        st = _dot(bt[:, c * q:(c + 1) * q], xw)
        stb_ref[c] = st[:, gw:]
        prev_ref[c, :, 0:gw] = state_f.astype(BF16)
        state_f = state_f * cdx[:, 0:gw] + st[:, 0:gw]
    state_b = jnp.zeros((SSD_STATE, gw), F32)
    for c in range(nc - 1, -1, -1):
        prev_ref[c, :, gw:] = state_b.astype(BF16)
        state_b = state_b * cdx_ref[c][:, gw:] + stb_ref[c]

    row = lax.broadcasted_iota(jnp.int32, (q, q), 0)
    col = lax.broadcasted_iota(jnp.int32, (q, q), 1)
    eye = jnp.where(row == col, 1.0, 0.0).astype(BF16)

    def chunk_out(c, slot):
        rows = pl.ds(pl.multiple_of(c * q, q), q)
        bc = b_ref[0, rows, :]
        cc = c_ref[0, rows, :]
        xb = x_ref[0, rows, :]
        cb = lax.dot_general(cc, bc, (((1,), (1,)), ((), ())), preferred_element_type=F32)
        rhs_ref[slot, SSD_DYN_ROW0:SSD_DYN_ROW0 + 2 * SUBLANES, :] = rdyn_ref[c]
        expo = _dot(jnp.concatenate([pcum_ref[c], eye], axis=1), rhs_ref[slot])
        ms = []
        for e in range(SSD_HPG):
            k = SSD_HPG + e
            w = jnp.exp(expo[:, e * LANES:(e + 1) * LANES]) + jnp.exp(expo[:, k * LANES:(k + 1) * LANES])
            ms.append((cb * w).astype(BF16))
        ys = []
        for p in range(SSD_HPG // 2):
            lhs = jnp.concatenate([ms[2 * p], ms[2 * p + 1]], axis=1)
            xp = xb[:, p * LANES:(p + 1) * LANES]
            zero = jnp.zeros_like(xp)
            rhs = jnp.concatenate([jnp.where(lane < SSD_HEADDIM, xp, zero),
                                   jnp.where(lane >= SSD_HEADDIM, xp, zero)], axis=0)
            ys.append(_dot(lhs, rhs))
        y = jnp.concatenate(ys, axis=1)
        yoff = _dot(cc, prev_ref[c]) * eexp_ref[c]
        y = y + yoff[:, 0:gw] + yoff[:, gw:]
        y = (y + dskip_ref[...] * xb.astype(F32)) * z_ref[0, rows, :].astype(F32)
        y_ref[0, rows, :] = y.astype(BF16)

    def chunk_step(io, carry):
        for j in range(SSD_UNROLL):
            chunk_out(io * SSD_UNROLL + j, j)
        return carry

    lax.fori_loop(0, nc // SSD_UNROLL, chunk_step, 0)


def _ssd(xbc, dtx, dtt, alog_g, alogt_g, dskip_x, zs):
    bsz, s, _ = xbc.shape
    q = SSD_CHUNK
    nc = s // q
    gw = SSD_HPG * SSD_HEADDIM
    nb = SSD_WIDTH // SSD_STATE
    consts = _ssd_constants()
    const_specs = [pl.BlockSpec(m.shape, lambda b, g, nd=m.ndim: (0,) * nd) for m in consts]
    return pl.pallas_call(
        _ssd_body,
        grid=(bsz, SSD_GROUPS),
        in_specs=[pl.BlockSpec((1, s, gw), lambda b, g: (b, 0, g)),
                  pl.BlockSpec((1, s, SSD_STATE), lambda b, g: (b, 0, nb + g)),
                  pl.BlockSpec((1, s, SSD_STATE), lambda b, g: (b, 0, nb + SSD_GROUPS + g)),
                  pl.BlockSpec((1, s, LANES), lambda b, g: (b, 0, g)),
                  pl.BlockSpec((1, 1, 2, nc * SSD_HPG, q), lambda b, g: (b, g, 0, 0, 0)),
                  pl.BlockSpec((1, 1, LANES), lambda b, g: (g, 0, 0)),
                  pl.BlockSpec((1, 2, nc * SSD_HPG, q), lambda b, g: (g, 0, 0, 0)),
                  pl.BlockSpec((1, gw), lambda b, g: (0, g)),
                  pl.BlockSpec((1, s, gw), lambda b, g: (b, 0, g)),
                  *const_specs],
        out_specs=pl.BlockSpec((1, s, gw), lambda b, g: (b, 0, g)),
        out_shape=jax.ShapeDtypeStruct((bsz, s, SSD_WIDTH), BF16),
        scratch_shapes=[pltpu.VMEM((nc, q, LANES), BF16),
                        pltpu.VMEM((nc, q, 2 * gw), F32),
                        pltpu.VMEM((nc, 1, 2 * gw), F32),
                        pltpu.VMEM((nc, SSD_STATE, gw), F32),
                        pltpu.VMEM((nc, SSD_STATE, 2 * gw), BF16),
                        pltpu.VMEM((nc, 2 * SUBLANES, 2 * SSD_HPG * LANES), BF16),
                        pltpu.VMEM((SSD_UNROLL, 2 * q, 2 * SSD_HPG * LANES), BF16)],
        compiler_params=_params("arbitrary", "arbitrary"),
        name="ssd",
    )(xbc, xbc, xbc, dtx, dtt, alog_g, alogt_g, dskip_x, zs, *consts)


TAIL_TM = 512
TAIL_A_TM = 256


def _tail_a_body(ya_ref, y_ref, g_ref, ng_ref, wa_ref, ws_ref, o_ref):
    d = o_ref.shape[1]
    ya = _dot(ya_ref[...], wa_ref[...])
    y = y_ref[...].astype(F32)
    yn = y * lax.rsqrt(jnp.mean(y * y, axis=-1, keepdims=True) + EPS) * ng_ref[...]
    ys = _dot(yn.astype(BF16), ws_ref[...])
    ga = g_ref[:, 0:d].astype(F32)
    gs = g_ref[:, d:2 * d].astype(F32)
    o_ref[...] = (ga * ya + gs * ys).astype(BF16)


def _tail_a(ya_in, y, gates, norm_g, wa, ws):
    n, d = ya_in.shape[0], wa.shape[1]
    tm = TAIL_A_TM
    resident = functools.partial(pl.BlockSpec, pipeline_mode=pl.Buffered(1))
    return pl.pallas_call(
        _tail_a_body,
        grid=(n // tm,),
        in_specs=[pl.BlockSpec((tm, ya_in.shape[1]), lambda i: (i, 0)),
                  pl.BlockSpec((tm, y.shape[1]), lambda i: (i, 0)),
                  pl.BlockSpec((tm, gates.shape[1]), lambda i: (i, 0)),
                  pl.BlockSpec((1, y.shape[1]), lambda i: (0, 0)),
                  resident(wa.shape, lambda i: (0, 0)),
                  resident(ws.shape, lambda i: (0, 0))],
        out_specs=pl.BlockSpec((tm, d), lambda i: (i, 0)),
        out_shape=jax.ShapeDtypeStruct((n, d), BF16),
        compiler_params=_params("arbitrary"),
        name="tail_a",
    )(ya_in, y, gates, norm_g, wa, ws)


def _tail_b_body(m_ref, x_ref, ada_ref, w_ref, fg_ref, o_ref):
    t = _dot(m_ref[...], w_ref[...])
    xn = x_ref[...] + ada_ref[0, 2:3, :] * t
    o_ref[...] = xn * lax.rsqrt(jnp.mean(xn * xn, axis=-1, keepdims=True) + EPS) * fg_ref[...]


def _tail_b(merged, x2, ada3, w_out, final_g, seq):
    n, d = x2.shape
    tm = TAIL_TM
    per = seq // tm
    resident = functools.partial(pl.BlockSpec, pipeline_mode=pl.Buffered(1))
    return pl.pallas_call(
        _tail_b_body,
        grid=(n // tm,),
        in_specs=[pl.BlockSpec((tm, d), lambda i: (i, 0)),
                  pl.BlockSpec((tm, d), lambda i: (i, 0)),
                  pl.BlockSpec((1, 3, d), lambda i: (i // per, 0, 0)),
                  resident(w_out.shape, lambda i: (0, 0)),
                  pl.BlockSpec((1, d), lambda i: (0, 0))],
        out_specs=pl.BlockSpec((tm, d), lambda i: (i, 0)),
        out_shape=jax.ShapeDtypeStruct((n, d), F32),
        compiler_params=_params("arbitrary"),
        name="tail_b",
    )(merged, x2, ada3, w_out, final_g)


def _layer(x, ada3, pos_tabs, norm_g, w_in, conv_w, conv_b, dt_bias, a_log, d_skip, ssd_norm_g,
           w_br_attn, w_br_ssd, w_out, out_g):
    bsz, s, d = x.shape
    cos, sin = pos_tabs
    hs = _modulated_norm(x, ada3, norm_g[None])
    qkv = [_proj_qkv(hs[g], w_in, cos, sin, g) for g in range(len(DILATIONS))]
    h = hs[0]
    tn = PROJ_TN
    zs_a = _proj_call(_proj_silu_body, h, w_in, COL_Z, ATTN_WIDTH // QKV_TN, QKV_TN, ATTN_WIDTH, BF16,
                      name="proj_za")
    zs_s = _proj_call(_proj_silu_body, h, w_in, COL_Z + ATTN_WIDTH, SSD_WIDTH // WIDE_TN, WIDE_TN, SSD_WIDTH,
                      BF16, name="proj_zs")
    xbc = _proj_call(
        _proj_conv_body, h, w_in, COL_XBC, CONV_CH // tn, tn, CONV_CH, BF16,
        extra=(conv_w, conv_b[None]),
        extra_specs=(pl.BlockSpec((CONV_WIDTH, tn), lambda j, b: (0, j)),
                     pl.BlockSpec((1, tn), lambda j, b: (0, j))),
        scratch=(*[pltpu.VMEM((tn // LANES, CONV_PAD + CONV_ROWS + CONV_TAIL, LANES), F32)] * CONV_CHUNKS,
                 *[pltpu.VMEM((tn // (2 * LANES), SUBLANES * CONV_STRIDE, LANES), jnp.uint32)] * CONV_CHUNKS),
        name="proj_xbc")
    n_dt = 2 * SSD_HEADS
    dtx, dtt = _proj_dt(h, w_in, dt_bias)
    gates = _proj_call(_proj_sigmoid_body, h, w_in[:, COL_GATE:], 0, 2 * d // WIDE_TN, WIDE_TN, 2 * d,
                       BF16, name="proj_gates")

    ya_in = _attention(qkv, zs_a)

    a_flat = a_log.reshape(1, n_dt)
    rolled = {k: jnp.roll(a_flat, -k * SSD_HPG, axis=1) for k in range(-2, SSD_GROUPS)}
    m0, m1 = _rep_masks(a_flat.shape)
    alog_g = jnp.stack([_replicate_heads(rolled, g, m0, m1) for g in range(SSD_GROUPS)])
    nc = s // SSD_CHUNK
    alogt_g = a_log.reshape(2, SSD_GROUPS, SSD_HPG).transpose(1, 0, 2)
    alogt_g = jnp.broadcast_to(alogt_g[:, :, None, :, None], (SSD_GROUPS, 2, nc, SSD_HPG, SSD_CHUNK))
    alogt_g = alogt_g.reshape(SSD_GROUPS, 2, nc * SSD_HPG, SSD_CHUNK)
    dskip_x = jnp.repeat(d_skip, SSD_HEADDIM)[None]
    y = _ssd(xbc, dtx, dtt, alog_g, alogt_g, dskip_x, zs_s)

    n = bsz * s
    merged = _tail_a(ya_in.reshape(n, ATTN_WIDTH), y.reshape(n, SSD_WIDTH),
                     gates.reshape(n, gates.shape[-1]), ssd_norm_g[None],
                     w_br_attn.astype(BF16), w_br_ssd.astype(BF16))
    out = _tail_b(merged, x.reshape(n, d), ada3, w_out.astype(BF16), out_g, s)
    return out.reshape(bsz, s, d)


def kernel(x, c, positions, norm_g, w_ada, b_ada, w_in, conv_w, conv_b, dt_bias, a_log, d_skip,
           ssd_norm_g, w_br_attn, w_br_ssd, w_out, final_g):
    bsz, s, d = x.shape
    depth = w_in.shape[0]
    inv = ROPE_THETA ** (-jnp.arange(0, HEAD_DIM, 2, dtype=F32) / HEAD_DIM)
    inv2 = jnp.concatenate([inv, inv])[None]
    cos, sin = _rope_tables(positions[..., None], inv2)
    pos_tabs = (cos.reshape(-1, s, HEAD_DIM), sin.reshape(-1, s, HEAD_DIM))
    for i in range(depth):
        ada3 = _ada(c, w_ada[i], b_ada[i][None]).reshape(bsz, 3, d)
        assert depth == 1
        x = _layer(x, ada3, pos_tabs, norm_g[i], w_in[i], conv_w[i], conv_b[i], dt_bias[i], a_log[i],
                   d_skip[i], ssd_norm_g[i], w_br_attn[i], w_br_ssd[i], w_out[i], final_g[None])
    return x
```

```python
import functools

import numpy as np
import jax
import jax.numpy as jnp
from jax import lax
from jax.experimental import pallas as pl
from jax.experimental.pallas import tpu as pltpu

F32 = jnp.float32
BF16 = jnp.bfloat16

D_MODEL = 2048
HEAD_DIM = 128
ATTN_HEADS = 12
DILATIONS = (1, 4, 16)
N_SIDE = 64
ATTN_WIDTH = ATTN_HEADS * HEAD_DIM
ROPE_THETA = 10000.0
SSD_WIDTH = 2 * D_MODEL
SSD_HEADDIM = 64
SSD_GROUPS = 8
SSD_HEADS = SSD_WIDTH // SSD_HEADDIM
SSD_HPG = SSD_HEADS // SSD_GROUPS
SSD_STATE = 128
SSD_CHUNK = 128
CONV_WIDTH = 5
CONV_CH = SSD_WIDTH + 2 * SSD_GROUPS * SSD_STATE
EPS = 1e-6
QKV_COLS = 3 * len(DILATIONS) * ATTN_WIDTH
COL_Z = QKV_COLS
COL_XBC = COL_Z + ATTN_WIDTH + SSD_WIDTH
COL_DT = COL_XBC + CONV_CH
COL_GATE = COL_DT + 2 * SSD_HEADS
IN_COLS = COL_GATE + 2 * D_MODEL

LANES = 128
PROJ_TN = 512
QKV_TN = 768
WIDE_TN = 1024
V7X_VMEM_LIMIT_BYTES = 56 * 1024 * 1024


def _params(*sem):
    return pltpu.CompilerParams(dimension_semantics=sem, vmem_limit_bytes=V7X_VMEM_LIMIT_BYTES)


def _dot(a, b):
    return jnp.dot(a, b, preferred_element_type=F32)


def _sigmoid(x):
    return 0.5 * jnp.tanh(0.5 * x) + 0.5


def _split3(v):
    hi = v.astype(BF16)
    r = v - hi.astype(F32)
    mid = r.astype(BF16)
    lo = (r - mid.astype(F32)).astype(BF16)
    return hi, mid, lo


def _ada_body(c_ref, w_ref, b_ref, o_ref):
    o_ref[...] = _dot(c_ref[...].astype(BF16), w_ref[...].astype(BF16)) + b_ref[...]


def _ada(c, w, bias):
    bsz, d = c.shape
    n = w.shape[1]
    tn = 768
    return pl.pallas_call(
        _ada_body,
        grid=(n // tn,),
        in_specs=[pl.BlockSpec((bsz, d), lambda j: (0, 0)),
                  pl.BlockSpec((d, tn), lambda j: (0, j)),
                  pl.BlockSpec((1, tn), lambda j: (0, j))],
        out_specs=pl.BlockSpec((bsz, tn), lambda j: (0, j)),
        out_shape=jax.ShapeDtypeStruct((bsz, n), F32),
        compiler_params=_params("arbitrary"),
        name="ada",
    )(c, w, bias)


def _rope_body(pos_ref, inv_ref, cos_ref, sin_ref):
    s = pos_ref.shape[1]
    ang = pos_ref[0].astype(F32) * inv_ref[...]
    lane = lax.broadcasted_iota(jnp.int32, ang.shape, 1)
    sin = jnp.sin(ang)
    cos_ref[0, 0] = jnp.cos(ang)
    sin_ref[0, 0] = jnp.where(lane < HEAD_DIM // 2, -sin, sin)
    for tab in (cos_ref, sin_ref):
        for g, dil in enumerate(DILATIONS[1:], start=1):
            sub = s // dil
            for r in range(dil):
                tab[g, 0, r * sub:(r + 1) * sub, :] = tab[0, 0, pl.ds(r, sub, stride=dil), :]


def _rope_tables(pos, inv2):
    bsz, s, _ = pos.shape
    ng = len(DILATIONS)
    spec = pl.BlockSpec((ng, 1, s, HEAD_DIM), lambda b: (0, b, 0, 0))
    return pl.pallas_call(
        _rope_body,
        grid=(bsz,),
        in_specs=[pl.BlockSpec((1, s, 1), lambda b: (b, 0, 0)),
                  pl.BlockSpec((1, HEAD_DIM), lambda b: (0, 0))],
        out_specs=[spec, spec],
        out_shape=[jax.ShapeDtypeStruct((ng, bsz, s, HEAD_DIM), F32)] * 2,
        compiler_params=_params("arbitrary"),
        name="rope_tables",
    )(pos, inv2)


H_TILE = 512


def _h_body(x_ref, ada_ref, g_ref, h1_ref, h4_ref, h16_ref, hs_ref):
    x = x_ref[0]
    ms = jnp.mean(x * x, axis=-1, keepdims=True)
    xn = x * lax.rsqrt(ms + EPS) * g_ref[...]
    h = xn * (1.0 + ada_ref[0, 1:2, :]) + ada_ref[0, 0:1, :]
    h1_ref[0] = h.astype(BF16)
    for cb in range(h.shape[1] // LANES):
        cols = slice(cb * LANES, (cb + 1) * LANES)
        hs_ref[cb] = h[:, cols]
        for r in range(4):
            h4_ref[0, r, :, cols] = hs_ref[cb, pl.ds(r, H_TILE // 4, stride=4), :].astype(BF16)
        for r in range(16):
            h16_ref[0, r, :, cols] = hs_ref[cb, pl.ds(r, H_TILE // 16, stride=16), :].astype(BF16)


def _modulated_norm(x, ada3, g):
    bsz, s, d = x.shape
    t = H_TILE
    outs = pl.pallas_call(
        _h_body,
        grid=(bsz, s // t),
        in_specs=[pl.BlockSpec((1, t, d), lambda b, i: (b, i, 0)),
                  pl.BlockSpec((1, 3, d), lambda b, i: (b, 0, 0)),
                  pl.BlockSpec((1, d), lambda b, i: (0, 0))],
        out_specs=[pl.BlockSpec((1, t, d), lambda b, i: (b, i, 0)),
                   pl.BlockSpec((1, 4, t // 4, d), lambda b, i: (b, 0, i, 0)),
                   pl.BlockSpec((1, 16, t // 16, d), lambda b, i: (b, 0, i, 0))],
        out_shape=[jax.ShapeDtypeStruct((bsz, s, d), BF16),
                   jax.ShapeDtypeStruct((bsz, 4, s // 4, d), BF16),
                   jax.ShapeDtypeStruct((bsz, 16, s // 16, d), BF16)],
        scratch_shapes=[pltpu.VMEM((d // LANES, t, LANES), F32)],
        compiler_params=_params("arbitrary", "arbitrary"),
        name="modulated_norm",
    )(x, ada3, g)
    h1, h4, h16 = outs
    return h1, h4.reshape(bsz, s, d), h16.reshape(bsz, s, d)


PROJ_ROWS = 512


def _cast_weight_tile(w_ref, wbf_ref):
    @pl.when(pl.program_id(1) == 0)
    def _():
        wbf_ref[...] = w_ref[...].astype(BF16)


PROJ_SPLIT = (PROJ_ROWS,) * 4


def _proj_chunks(h_ref, wbf_ref, epilogue, sizes=PROJ_SPLIT):
    assert sum(sizes) == h_ref.shape[1]
    start = 0
    for size in sizes:
        rows = slice(start, start + size)
        epilogue(rows, _dot(h_ref[0, rows, :], wbf_ref[...]))
        start += size


def _proj_rope_body(h_ref, w_ref, cos_ref, sin_ref, o_ref, wbf_ref):
    _cast_weight_tile(w_ref, wbf_ref)
    j = pl.program_id(0)
    heads = o_ref.shape[1]
    n_rot = 2 * ATTN_HEADS // heads

    def rotate(rows, acc):
        cos = cos_ref[0, rows, :]
        sin = sin_ref[0, rows, :]
        for hh in range(heads):
            t = acc[:, hh * HEAD_DIM:(hh + 1) * HEAD_DIM]
            rot = t * cos + pltpu.roll(t, HEAD_DIM // 2, 1) * sin
            o_ref[0, hh, rows, :] = rot.astype(BF16)

    def plain(rows, acc):
        for hh in range(heads):
            o_ref[0, hh, rows, :] = acc[:, hh * HEAD_DIM:(hh + 1) * HEAD_DIM].astype(BF16)

    @pl.when(j < n_rot)
    def _():
        _proj_chunks(h_ref, wbf_ref, rotate)

    @pl.when(j >= n_rot)
    def _():
        _proj_chunks(h_ref, wbf_ref, plain)


def _proj_silu_body(h_ref, w_ref, o_ref, wbf_ref):
    _cast_weight_tile(w_ref, wbf_ref)

    def epilogue(rows, acc):
        o_ref[0, rows, :] = (acc * _sigmoid(acc)).astype(BF16)

    _proj_chunks(h_ref, wbf_ref, epilogue)


def _proj_sigmoid_body(h_ref, w_ref, o_ref, wbf_ref):
    _cast_weight_tile(w_ref, wbf_ref)

    def epilogue(rows, acc):
        o_ref[0, rows, :] = _sigmoid(acc).astype(BF16)

    _proj_chunks(h_ref, wbf_ref, epilogue)


CONV_PAD = 8
SUBLANES = 8
CONV_ROWS = 512
CONV_STRIDE = CONV_ROWS // SUBLANES + 1
CONV_TAIL = 3 * SUBLANES
CONV_CHUNKS = 4


def _proj_conv_body(h_ref, w_ref, cw_ref, cb_ref, o_ref, wbf_ref, *scratch):
    pads, stages = scratch[:CONV_CHUNKS], scratch[CONV_CHUNKS:]
    n_cb = PROJ_TN // LANES
    half = (CONV_WIDTH - 1) // 2
    end = CONV_PAD + CONV_ROWS
    _cast_weight_tile(w_ref, wbf_ref)
    pads[0][:, 0:CONV_PAD, :] = jnp.zeros((n_cb, CONV_PAD, LANES), F32)
    pads[-1][:, end:end + CONV_TAIL, :] = jnp.zeros((n_cb, CONV_TAIL, LANES), F32)

    def conv_chunk(m):
        base = m * CONV_ROWS
        for pair in range(n_cb // 2):
            cbs = (2 * pair, 2 * pair + 1)
            cols = [slice(cb * LANES, (cb + 1) * LANES) for cb in cbs]
            taps = [[cw_ref[k:k + 1, c] for k in range(CONV_WIDTH)] for c in cols]
            bias = [cb_ref[:, c] for c in cols]

            def tile(cb, i):
                return pads[m][cb, pl.ds(CONV_PAD + i, SUBLANES, stride=CONV_STRIDE), :]

            wins = [[tile(cb, i) for i in range(-half, half)] for cb in cbs]
            for i in range(CONV_STRIDE):
                acts = []
                for n, cb in enumerate(cbs):
                    win = wins[n]
                    win.append(tile(cb, i + half))
                    out = bias[n] + taps[n][0] * win[0]
                    for k in range(1, CONV_WIDTH):
                        out = out + taps[n][k] * win[k]
                    acts.append(out * _sigmoid(out))
                    win.pop(0)
                stages[m][pair, pl.ds(i, SUBLANES, stride=CONV_STRIDE), :] = pltpu.pack_elementwise(
                    acts, packed_dtype=BF16)
            packed = stages[m][pair, 0:CONV_ROWS, :]
            for n, c in enumerate(cols):
                o_ref[0, base:base + CONV_ROWS, c] = pltpu.unpack_elementwise(
                    packed, index=n, packed_dtype=BF16, unpacked_dtype=F32).astype(BF16)

    def epilogue(rows, acc):
        m = rows.start // CONV_ROWS
        for cb in range(n_cb):
            a = acc[:, cb * LANES:(cb + 1) * LANES]
            pads[m][cb, CONV_PAD:end, :] = a
            if m > 0:
                pads[m - 1][cb, end:end + CONV_TAIL, :] = a[0:CONV_TAIL]
            if m + 1 < CONV_CHUNKS:
                pads[m + 1][cb, 0:CONV_PAD, :] = a[CONV_ROWS - CONV_PAD:CONV_ROWS]
        if m > 0:
            conv_chunk(m - 1)

    _proj_chunks(h_ref, wbf_ref, epilogue, sizes=(CONV_ROWS,) * CONV_CHUNKS)
    conv_chunk(CONV_CHUNKS - 1)


def _rep_masks(shape):
    lane = lax.broadcasted_iota(jnp.int32, shape, len(shape) - 1) & (SSD_HEADS - 1)
    return lane < SSD_HPG, lane < 2 * SSD_HPG


def _packed_split(v):
    hi, mid, lo = _split3(v)
    m0, m1 = _rep_masks(v.shape)
    return jnp.where(m0, hi, jnp.where(m1, mid, lo))


def _replicate_heads(rolled, g, m0, m1):
    return jnp.where(m0, rolled[g], jnp.where(m1, rolled[g - 1], rolled[g - 2]))


def _proj_dt_body(h_ref, w_ref, bias_ref, o_ref, ot_ref, wbf_ref):
    s = h_ref.shape[1]
    _cast_weight_tile(w_ref, wbf_ref)
    acc = _dot(h_ref[0], wbf_ref[...]) + bias_ref[...]
    sp = jnp.maximum(acc, 0.0) + jnp.log1p(jnp.exp(-jnp.abs(acc)))
    n = 2 * SSD_HEADS
    m0, m1 = _rep_masks(sp.shape)
    rolled = {k: (sp if k == 0 else pltpu.roll(sp, (-k * SSD_HPG) % n, 1))
              for k in range(-2, SSD_GROUPS)}
    for g in range(SSD_GROUPS):
        o_ref[0, :, g * n:(g + 1) * n] = _replicate_heads(rolled, g, m0, m1)
    for c in range(s // SSD_CHUNK):
        t = sp[c * SSD_CHUNK:(c + 1) * SSD_CHUNK, :].T
        for g in range(SSD_GROUPS):
            for d in range(2):
                r0 = d * SSD_HEADS + g * SSD_HPG
                ot_ref[0, g, d, c * SSD_HPG:(c + 1) * SSD_HPG, :] = t[r0:r0 + SSD_HPG, :]


def _proj_dt(h, w_in, dt_bias):
    bsz, s, d = h.shape
    n = 2 * SSD_HEADS
    nc = s // SSD_CHUNK
    blk0 = COL_DT // n
    return pl.pallas_call(
        _proj_dt_body,
        grid=(1, bsz),
        in_specs=[pl.BlockSpec((1, s, d), lambda j, b: (b, 0, 0)),
                  pl.BlockSpec((d, n), lambda j, b: (0, blk0)),
                  pl.BlockSpec((1, n), lambda j, b: (0, 0))],
        out_specs=[pl.BlockSpec((1, s, SSD_GROUPS * n), lambda j, b: (b, 0, 0)),
                   pl.BlockSpec((1, SSD_GROUPS, 2, nc * SSD_HPG, SSD_CHUNK), lambda j, b: (b, 0, 0, 0, 0))],
        out_shape=[jax.ShapeDtypeStruct((bsz, s, SSD_GROUPS * n), F32),
                   jax.ShapeDtypeStruct((bsz, SSD_GROUPS, 2, nc * SSD_HPG, SSD_CHUNK), F32)],
        scratch_shapes=[pltpu.VMEM((d, n), BF16)],
        compiler_params=_params("arbitrary", "arbitrary"),
        name="proj_dt",
    )(h, w_in, dt_bias.reshape(1, n))


def _proj_call(body, h, w_in, col0, n_tiles, tn, out_cols, out_dtype, extra=(), extra_specs=(),
               scratch=(), name="proj"):
    bsz, s, d = h.shape
    blk0, rem = divmod(col0, tn)
    assert rem == 0
    return pl.pallas_call(
        body,
        grid=(n_tiles, bsz),
        in_specs=[pl.BlockSpec((1, s, d), lambda j, b: (b, 0, 0)),
                  pl.BlockSpec((d, tn), lambda j, b: (0, blk0 + j)),
                  *extra_specs],
        out_specs=pl.BlockSpec((1, s, tn), lambda j, b: (b, 0, j)),
        out_shape=jax.ShapeDtypeStruct((bsz, s, out_cols), out_dtype),
        scratch_shapes=[pltpu.VMEM((d, tn), BF16), *scratch],
        compiler_params=_params("arbitrary", "arbitrary"),
        name=name,
    )(h, w_in, *extra)


def _proj_qkv(h, w_in, cos, sin, group):
    bsz, s, d = h.shape
    tn = QKV_TN
    tps = ATTN_WIDTH // tn
    n_groups = len(DILATIONS)

    def wcol(j, b):
        return (0, (j // tps) * (n_groups * tps) + group * tps + j % tps)

    tab = pl.BlockSpec((1, s, HEAD_DIM), lambda j, b: (group * bsz + b, 0, 0))
    return pl.pallas_call(
        _proj_rope_body,
        grid=(3 * tps, bsz),
        in_specs=[pl.BlockSpec((1, s, d), lambda j, b: (b, 0, 0)),
                  pl.BlockSpec((d, tn), wcol), tab, tab],
        out_specs=pl.BlockSpec((1, tn // HEAD_DIM, s, HEAD_DIM), lambda j, b: (b, j, 0, 0)),
        out_shape=jax.ShapeDtypeStruct((bsz, 3 * ATTN_HEADS, s, HEAD_DIM), BF16),
        scratch_shapes=[pltpu.VMEM((d, tn), BF16)],
        compiler_params=_params("arbitrary", "arbitrary"),
        name=f"proj_qkv{group}",
    )(h, w_in, cos, sin)


ATT_BLK = 128


ATT_UNROLL = 16


def _attn_body(q1, k1, v1, q2, k2, v2, q3, k3, v3, z_ref, o_ref, on_ref, ls_ref, va_ref, bias_ref):
    s = o_ref.shape[1]
    scale = HEAD_DIM ** -0.5
    qi = lax.broadcasted_iota(jnp.int32, (ATT_BLK, 2 * ATT_BLK), 0)
    ki = lax.broadcasted_iota(jnp.int32, (ATT_BLK, 2 * ATT_BLK), 1)
    for t in range(3):
        bias_ref[t] = jnp.where(jnp.abs(qi - ki + t * N_SIDE) <= N_SIDE, 0.0, -jnp.inf)
    va_ref[:, :, HEAD_DIM:] = jnp.ones((len(DILATIONS), s, HEAD_DIM), BF16)

    def block(g, q_ref, k_ref, q0, k0, bias, nk):
        q = q_ref[0, 0, pl.ds(q0, ATT_BLK), :]
        kw = k_ref[0, 0, pl.ds(k0, nk), :]
        sc = lax.dot_general(q, kw, (((1,), (1,)), ((), ())), preferred_element_type=F32) * scale + bias
        m = jnp.max(sc, axis=-1, keepdims=True)
        p = jnp.exp(sc - m)
        pv = _dot(p.astype(BF16), va_ref[g, pl.ds(k0, nk), :])
        den = pv[:, HEAD_DIM:]
        return pv[:, :HEAD_DIM] / den, m + jnp.log(den)

    groups = ((q1, k1, v1), (q2, k2, v2), (q3, k3, v3))
    for g, (q_ref, k_ref, v_ref) in enumerate(groups):
        dil = DILATIONS[g]
        sub = s // dil
        nblk = sub // ATT_BLK
        va_ref[g, :, :HEAD_DIM] = v_ref[0, 0]
        for r in range(dil):
            if nblk == 1:
                o, lse = block(g, q_ref, k_ref, r * sub, r * sub, bias_ref[0, :, :ATT_BLK], ATT_BLK)
                on_ref[g, pl.ds(r, ATT_BLK, stride=dil), :] = o
                ls_ref[g, pl.ds(r, ATT_BLK, stride=dil), :] = lse
            else:
                def one(i, q_ref=q_ref, k_ref=k_ref, r=r, sub=sub, dil=dil, g=g):
                    loc = i * ATT_BLK
                    kloc = jnp.clip(loc - N_SIDE, 0, sub - 2 * ATT_BLK)
                    q0 = pl.multiple_of(r * sub + loc, ATT_BLK)
                    k0 = pl.multiple_of(r * sub + kloc, N_SIDE)
                    bias = bias_ref[lax.shift_right_logical(loc - kloc, N_SIDE.bit_length() - 1)]
                    o, lse = block(g, q_ref, k_ref, q0, k0, bias, 2 * ATT_BLK)
                    if dil == 1:
                        rows = pl.ds(q0, ATT_BLK)
                    else:
                        rows = pl.ds(loc * dil + r, ATT_BLK, stride=dil)
                    on_ref[g, rows, :] = o
                    ls_ref[g, rows, :] = lse

                un = min(ATT_UNROLL, nblk)

                def step(io, carry, one=one, un=un):
                    for j in range(un):
                        one(io * un + j)
                    return carry

                lax.fori_loop(0, nblk // un, step, 0)

    l0, l1, l2 = ls_ref[0], ls_ref[1], ls_ref[2]
    mx = jnp.maximum(jnp.maximum(l0, l1), l2)
    e0, e1, e2 = jnp.exp(l0 - mx), jnp.exp(l1 - mx), jnp.exp(l2 - mx)
    o = (e0 * on_ref[0] + e1 * on_ref[1] + e2 * on_ref[2]) / (e0 + e1 + e2)
    o_ref[0] = (o * z_ref[0].astype(F32)).astype(BF16)


def _attention(qkv, zs):
    bsz, _, s, _ = qkv[0].shape
    specs = []
    args = []
    for g in range(len(DILATIONS)):
        for sec in range(3):
            specs.append(pl.BlockSpec((1, 1, s, HEAD_DIM),
                                      lambda b, h, sec=sec: (b, sec * ATTN_HEADS + h, 0, 0)))
            args.append(qkv[g])
    specs.append(pl.BlockSpec((1, s, HEAD_DIM), lambda b, h: (b, 0, h)))
    args.append(zs)
    return pl.pallas_call(
        _attn_body,
        grid=(bsz, ATTN_HEADS),
        in_specs=specs,
        out_specs=pl.BlockSpec((1, s, HEAD_DIM), lambda b, h: (b, 0, h)),
        out_shape=jax.ShapeDtypeStruct((bsz, s, ATTN_WIDTH), BF16),
        scratch_shapes=[pltpu.VMEM((3, s, HEAD_DIM), F32), pltpu.VMEM((3, s, HEAD_DIM), F32),
                        pltpu.VMEM((len(DILATIONS), s, 2 * HEAD_DIM), BF16),
                        pltpu.VMEM((3, ATT_BLK, 2 * ATT_BLK), F32)],
        compiler_params=_params("arbitrary", "arbitrary"),
        name="dilated_attention",
    )(*args)


SSD_UNROLL = 16
SSD_BIG = 1e30


def _ssd_constants():
    q, gw = SSD_CHUNK, SSD_HPG * SSD_HEADDIM
    i = np.arange(q)
    low = i[None, :] <= i[:, None]
    upp = i[None, :] >= i[:, None]
    tri = np.stack([low, upp])
    tri3 = np.stack([np.tile(upp, (3, 1)), np.tile(low, (3, 1))])
    c = np.arange(LANES)[:, None]
    in_f = c < 3 * SSD_HPG
    in_b = (c >= SSD_HEADS) & (c < SSD_HEADS + 3 * SSD_HPG)
    head = np.arange(gw)[None, :] // SSD_HEADDIM
    efb = np.concatenate([in_f & (c % SSD_HPG == head), in_b & (c % SSD_HPG == head)], axis=1)
    k = np.arange(2 * SSD_HPG * LANES)[None, :] // LANES
    ecol = np.where(k < SSD_HPG, in_f & (c % SSD_HPG == k), in_b & (c % SSD_HPG == k - SSD_HPG))
    src = np.arange(2 * SSD_HPG * LANES)[None, :] % LANES
    dst = i[:, None]
    keep = np.where(k < SSD_HPG, dst >= src, dst <= src)
    rhs0 = np.concatenate([ecol.astype(np.float32), np.where(keep, 0.0, -SSD_BIG)], axis=0)
    return tuple(jnp.asarray(m, BF16) for m in (tri, tri3, efb, rhs0))


SSD_DYN_ROW0 = 2 * SSD_HPG
SSD_ONE_LANE0 = 3 * SSD_HPG


def _ssd_body(x_ref, b_ref, c_ref, dt_ref, dtt_ref, alog_ref, alogt_ref, dskip_ref, z_ref,
              tri_ref, tri3_ref, efb_ref, rhs0_ref, y_ref,
              pcum_ref, eexp_ref, cdx_ref, stb_ref, prev_ref, rdyn_ref, rhs_ref):
    q = SSD_CHUNK
    s = x_ref.shape[1]
    nc = s // q
    gw = SSD_HPG * SSD_HEADDIM
    fwd_lane = lax.broadcasted_iota(jnp.int32, (q, LANES), 1) < SSD_HEADS
    lane = lax.broadcasted_iota(jnp.int32, (q, LANES), 1)

    a_row = -jnp.exp(alog_ref[0])
    dt_all = dt_ref[0]
    a_all = dt_all * a_row
    a_wide = jnp.concatenate([a_all[c * q:(c + 1) * q] for c in range(nc)], axis=1)
    fwd_wide = (lax.broadcasted_iota(jnp.int32, a_wide.shape, 1) & (LANES - 1)) < SSD_HEADS
    both = jnp.concatenate([tri_ref[0], tri_ref[1]], axis=1)
    cum_wide = jnp.zeros(a_wide.shape, F32)
    for part in _split3(a_wide):
        zero = jnp.zeros_like(part)
        stacked = jnp.concatenate([jnp.where(fwd_wide, part, zero), jnp.where(fwd_wide, zero, part)], axis=0)
        cum_wide = cum_wide + _dot(both, stacked)

    dt_t = dtt_ref[0, 0]
    a_t = dt_t * -jnp.exp(alogt_ref[0])
    log_dt = jnp.where(dt_t > 0.0, jnp.log(dt_t), -SSD_BIG)
    neg_parts = []
    for d in range(2):
        parts = jnp.concatenate(_split3(a_t[d]), axis=1)
        src_term = _dot(parts, tri3_ref[d]) - log_dt[d]
        neg_parts.append([p.astype(F32) for p in _split3(-src_term)])
    sub = lax.broadcasted_iota(jnp.int32, (2 * SUBLANES, LANES), 0)
    for c in range(nc):
        blocks = []
        for k in range(2 * SSD_HPG):
            d, e = divmod(k, SSD_HPG)
            blk = jnp.where(sub == e, 1.0, 0.0) if d == 0 else jnp.zeros(sub.shape, F32)
            for p in range(3):
                r = c * SSD_HPG + e
                blk = jnp.where(sub == SUBLANES + p, neg_parts[d][p][r:r + 1, :], blk)
            blocks.append(blk)
        rdyn_ref[c] = jnp.concatenate(blocks, axis=1).astype(BF16)
    for j in range(SSD_UNROLL):
        rhs_ref[j] = rhs0_ref[...]

    bt = b_ref[0].T
    state_f = jnp.zeros((SSD_STATE, gw), F32)
    for c in range(nc):
        cum = cum_wide[:, c * LANES:(c + 1) * LANES]
        dt = dt_all[c * q:(c + 1) * q]
        ref = jnp.where(fwd_lane[0:1], cum[q - 1:q], cum[0:1])
        one_lane = (lane >= SSD_ONE_LANE0) & (lane < SSD_ONE_LANE0 + 3)
        pcum_ref[c] = jnp.where(one_lane, 1.0, _packed_split(cum).astype(F32)).astype(BF16)
        eexp_ref[c] = _dot(_packed_split(jnp.exp(cum)), efb_ref[...])
        wexp = _dot(_packed_split(dt * jnp.exp(ref - cum)), efb_ref[...])
        cdx = _dot(_packed_split(jnp.broadcast_to(jnp.exp(ref), (8, LANES))), efb_ref[...])[0:1]
        cdx_ref[c] = cdx
        xb = x_ref[0, c * q:(c + 1) * q, :].astype(F32)
        xw = (jnp.concatenate([xb, xb], axis=1) * wexp).astype(BF16)
        st = _dot(bt[:, c * q:(c + 1) * q], xw)
        stb_ref[c] = st[:, gw:]
        prev_ref[c, :, 0:gw] = state_f.astype(BF16)
        state_f = state_f * cdx[:, 0:gw] + st[:, 0:gw]
    state_b = jnp.zeros((SSD_STATE, gw), F32)
    for c in range(nc - 1, -1, -1):
        prev_ref[c, :, gw:] = state_b.astype(BF16)
        state_b = state_b * cdx_ref[c][:, gw:] + stb_ref[c]

    row = lax.broadcasted_iota(jnp.int32, (q, q), 0)
    col = lax.broadcasted_iota(jnp.int32, (q, q), 1)
    eye = jnp.where(row == col, 1.0, 0.0).astype(BF16)

    def chunk_out(c, slot):
        rows = pl.ds(pl.multiple_of(c * q, q), q)
        bc = b_ref[0, rows, :]
        cc = c_ref[0, rows, :]
        xb = x_ref[0, rows, :]
        cb = lax.dot_general(cc, bc, (((1,), (1,)), ((), ())), preferred_element_type=F32)
        rhs_ref[slot, SSD_DYN_ROW0:SSD_DYN_ROW0 + 2 * SUBLANES, :] = rdyn_ref[c]
        expo = _dot(jnp.concatenate([pcum_ref[c], eye], axis=1), rhs_ref[slot])
        ms = []
        for e in range(SSD_HPG):
            k = SSD_HPG + e
            w = jnp.exp(expo[:, e * LANES:(e + 1) * LANES]) + jnp.exp(expo[:, k * LANES:(k + 1) * LANES])
            ms.append((cb * w).astype(BF16))
        ys = []
        for p in range(SSD_HPG // 2):
            lhs = jnp.concatenate([ms[2 * p], ms[2 * p + 1]], axis=1)
            xp = xb[:, p * LANES:(p + 1) * LANES]
            zero = jnp.zeros_like(xp)
            rhs = jnp.concatenate([jnp.where(lane < SSD_HEADDIM, xp, zero),
                                   jnp.where(lane >= SSD_HEADDIM, xp, zero)], axis=0)
            ys.append(_dot(lhs, rhs))
        y = jnp.concatenate(ys, axis=1)
        yoff = _dot(cc, prev_ref[c]) * eexp_ref[c]
        y = y + yoff[:, 0:gw] + yoff[:, gw:]
        y = (y + dskip_ref[...] * xb.astype(F32)) * z_ref[0, rows, :].astype(F32)
        y_ref[0, rows, :] = y.astype(BF16)

    def chunk_step(io, carry):
        for j in range(SSD_UNROLL):
            chunk_out(io * SSD_UNROLL + j, j)
        return carry

    lax.fori_loop(0, nc // SSD_UNROLL, chunk_step, 0)


def _ssd(xbc, dtx, dtt, alog_g, alogt_g, dskip_x, zs):
    bsz, s, _ = xbc.shape
    q = SSD_CHUNK
    nc = s // q
    gw = SSD_HPG * SSD_HEADDIM
    nb = SSD_WIDTH // SSD_STATE
    consts = _ssd_constants()
    const_specs = [pl.BlockSpec(m.shape, lambda b, g, nd=m.ndim: (0,) * nd) for m in consts]
    return pl.pallas_call(
        _ssd_body,
        grid=(bsz, SSD_GROUPS),
        in_specs=[pl.BlockSpec((1, s, gw), lambda b, g: (b, 0, g)),
                  pl.BlockSpec((1, s, SSD_STATE), lambda b, g: (b, 0, nb + g)),
                  pl.BlockSpec((1, s, SSD_STATE), lambda b, g: (b, 0, nb + SSD_GROUPS + g)),
                  pl.BlockSpec((1, s, LANES), lambda b, g: (b, 0, g)),
                  pl.BlockSpec((1, 1, 2, nc * SSD_HPG, q), lambda b, g: (b, g, 0, 0, 0)),
                  pl.BlockSpec((1, 1, LANES), lambda b, g: (g, 0, 0)),
                  pl.BlockSpec((1, 2, nc * SSD_HPG, q), lambda b, g: (g, 0, 0, 0)),
                  pl.BlockSpec((1, gw), lambda b, g: (0, g)),
                  pl.BlockSpec((1, s, gw), lambda b, g: (b, 0, g)),
                  *const_specs],
        out_specs=pl.BlockSpec((1, s, gw), lambda b, g: (b, 0, g)),
        out_shape=jax.ShapeDtypeStruct((bsz, s, SSD_WIDTH), BF16),
        scratch_shapes=[pltpu.VMEM((nc, q, LANES), BF16),
                        pltpu.VMEM((nc, q, 2 * gw), F32),
                        pltpu.VMEM((nc, 1, 2 * gw), F32),
                        pltpu.VMEM((nc, SSD_STATE, gw), F32),
                        pltpu.VMEM((nc, SSD_STATE, 2 * gw), BF16),
                        pltpu.VMEM((nc, 2 * SUBLANES, 2 * SSD_HPG * LANES), BF16),
                        pltpu.VMEM((SSD_UNROLL, 2 * q, 2 * SSD_HPG * LANES), BF16)],
        compiler_params=_params("arbitrary", "arbitrary"),
        name="ssd",
    )(xbc, xbc, xbc, dtx, dtt, alog_g, alogt_g, dskip_x, zs, *consts)


TAIL_TM = 512
TAIL_A_TM = 256


def _tail_a_body(ya_ref, y_ref, g_ref, ng_ref, wa_ref, ws_ref, o_ref):
    d = o_ref.shape[1]
    ya = _dot(ya_ref[...], wa_ref[...])
    y = y_ref[...].astype(F32)
    yn = y * lax.rsqrt(jnp.mean(y * y, axis=-1, keepdims=True) + EPS) * ng_ref[...]
    ys = _dot(yn.astype(BF16), ws_ref[...])
    ga = g_ref[:, 0:d].astype(F32)
    gs = g_ref[:, d:2 * d].astype(F32)
    o_ref[...] = (ga * ya + gs * ys).astype(BF16)


def _tail_a(ya_in, y, gates, norm_g, wa, ws):
    n, d = ya_in.shape[0], wa.shape[1]
    tm = TAIL_A_TM
    resident = functools.partial(pl.BlockSpec, pipeline_mode=pl.Buffered(1))
    return pl.pallas_call(
        _tail_a_body,
        grid=(n // tm,),
        in_specs=[pl.BlockSpec((tm, ya_in.shape[1]), lambda i: (i, 0)),
                  pl.BlockSpec((tm, y.shape[1]), lambda i: (i, 0)),
                  pl.BlockSpec((tm, gates.shape[1]), lambda i: (i, 0)),
                  pl.BlockSpec((1, y.shape[1]), lambda i: (0, 0)),
                  resident(wa.shape, lambda i: (0, 0)),
                  resident(ws.shape, lambda i: (0, 0))],
        out_specs=pl.BlockSpec((tm, d), lambda i: (i, 0)),
        out_shape=jax.ShapeDtypeStruct((n, d), BF16),
        compiler_params=_params("arbitrary"),
        name="tail_a",
    )(ya_in, y, gates, norm_g, wa, ws)


def _tail_b_body(m_ref, x_ref, ada_ref, w_ref, fg_ref, o_ref):
    t = _dot(m_ref[...], w_ref[...])
    xn = x_ref[...] + ada_ref[0, 2:3, :] * t
    o_ref[...] = xn * lax.rsqrt(jnp.mean(xn * xn, axis=-1, keepdims=True) + EPS) * fg_ref[...]


def _tail_b(merged, x2, ada3, w_out, final_g, seq):
    n, d = x2.shape
    tm = TAIL_TM
    per = seq // tm
    resident = functools.partial(pl.BlockSpec, pipeline_mode=pl.Buffered(1))
    return pl.pallas_call(
        _tail_b_body,
        grid=(n // tm,),
        in_specs=[pl.BlockSpec((tm, d), lambda i: (i, 0)),
                  pl.BlockSpec((tm, d), lambda i: (i, 0)),
                  pl.BlockSpec((1, 3, d), lambda i: (i // per, 0, 0)),
                  resident(w_out.shape, lambda i: (0, 0)),
                  pl.BlockSpec((1, d), lambda i: (0, 0))],
        out_specs=pl.BlockSpec((tm, d), lambda i: (i, 0)),
        out_shape=jax.ShapeDtypeStruct((n, d), F32),
        compiler_params=_params("arbitrary"),
        name="tail_b",
    )(merged, x2, ada3, w_out, final_g)


def _layer(x, ada3, pos_tabs, norm_g, w_in, conv_w, conv_b, dt_bias, a_log, d_skip, ssd_norm_g,
           w_br_attn, w_br_ssd, w_out, out_g):
    bsz, s, d = x.shape
    cos, sin = pos_tabs
    hs = _modulated_norm(x, ada3, norm_g[None])
    qkv = [_proj_qkv(hs[g], w_in, cos, sin, g) for g in range(len(DILATIONS))]
    h = hs[0]
    tn = PROJ_TN
    zs_a = _proj_call(_proj_silu_body, h, w_in, COL_Z, ATTN_WIDTH // QKV_TN, QKV_TN, ATTN_WIDTH, BF16,
                      name="proj_za")
    zs_s = _proj_call(_proj_silu_body, h, w_in, COL_Z + ATTN_WIDTH, SSD_WIDTH // WIDE_TN, WIDE_TN, SSD_WIDTH,
                      BF16, name="proj_zs")
    xbc = _proj_call(
        _proj_conv_body, h, w_in, COL_XBC, CONV_CH // tn, tn, CONV_CH, BF16,
        extra=(conv_w, conv_b[None]),
        extra_specs=(pl.BlockSpec((CONV_WIDTH, tn), lambda j, b: (0, j)),
                     pl.BlockSpec((1, tn), lambda j, b: (0, j))),
        scratch=(*[pltpu.VMEM((tn // LANES, CONV_PAD + CONV_ROWS + CONV_TAIL, LANES), F32)] * CONV_CHUNKS,
                 *[pltpu.VMEM((tn // (2 * LANES), SUBLANES * CONV_STRIDE, LANES), jnp.uint32)] * CONV_CHUNKS),
        name="proj_xbc")
    n_dt = 2 * SSD_HEADS
    dtx, dtt = _proj_dt(h, w_in, dt_bias)
    gates = _proj_call(_proj_sigmoid_body, h, w_in[:, COL_GATE:], 0, 2 * d // WIDE_TN, WIDE_TN, 2 * d,
                       BF16, name="proj_gates")

    ya_in = _attention(qkv, zs_a)

    a_flat = a_log.reshape(1, n_dt)
    rolled = {k: jnp.roll(a_flat, -k * SSD_HPG, axis=1) for k in range(-2, SSD_GROUPS)}
    m0, m1 = _rep_masks(a_flat.shape)
    alog_g = jnp.stack([_replicate_heads(rolled, g, m0, m1) for g in range(SSD_GROUPS)])
    nc = s // SSD_CHUNK
    alogt_g = a_log.reshape(2, SSD_GROUPS, SSD_HPG).transpose(1, 0, 2)
    alogt_g = jnp.broadcast_to(alogt_g[:, :, None, :, None], (SSD_GROUPS, 2, nc, SSD_HPG, SSD_CHUNK))
    alogt_g = alogt_g.reshape(SSD_GROUPS, 2, nc * SSD_HPG, SSD_CHUNK)
    dskip_x = jnp.repeat(d_skip, SSD_HEADDIM)[None]
    y = _ssd(xbc, dtx, dtt, alog_g, alogt_g, dskip_x, zs_s)

    n = bsz * s
    merged = _tail_a(ya_in.reshape(n, ATTN_WIDTH), y.reshape(n, SSD_WIDTH),
                     gates.reshape(n, gates.shape[-1]), ssd_norm_g[None],
                     w_br_attn.astype(BF16), w_br_ssd.astype(BF16))
    out = _tail_b(merged, x.reshape(n, d), ada3, w_out.astype(BF16), out_g, s)
    return out.reshape(bsz, s, d)


def kernel(x, c, positions, norm_g, w_ada, b_ada, w_in, conv_w, conv_b, dt_bias, a_log, d_skip,
           ssd_norm_g, w_br_attn, w_br_ssd, w_out, final_g):
    bsz, s, d = x.shape
    depth = w_in.shape[0]
    inv = ROPE_THETA ** (-jnp.arange(0, HEAD_DIM, 2, dtype=F32) / HEAD_DIM)
    inv2 = jnp.concatenate([inv, inv])[None]
    cos, sin = _rope_tables(positions[..., None], inv2)
    pos_tabs = (cos.reshape(-1, s, HEAD_DIM), sin.reshape(-1, s, HEAD_DIM))
    for i in range(depth):
        ada3 = _ada(c, w_ada[i], b_ada[i][None]).reshape(bsz, 3, d)
        assert depth == 1
        x = _layer(x, ada3, pos_tabs, norm_g[i], w_in[i], conv_w[i], conv_b[i], dt_bias[i], a_log[i],
                   d_skip[i], ssd_norm_g[i], w_br_attn[i], w_br_ssd[i], w_out[i], final_g[None])
    return x
```

```python
import functools

import numpy as np
import jax
import jax.numpy as jnp
from jax import lax
from jax.experimental import pallas as pl
from jax.experimental.pallas import tpu as pltpu

F32 = jnp.float32
BF16 = jnp.bfloat16

D_MODEL = 2048
HEAD_DIM = 128
ATTN_HEADS = 12
DILATIONS = (1, 4, 16)
N_SIDE = 64
ATTN_WIDTH = ATTN_HEADS * HEAD_DIM
ROPE_THETA = 10000.0
SSD_WIDTH = 2 * D_MODEL
SSD_HEADDIM = 64
SSD_GROUPS = 8
SSD_HEADS = SSD_WIDTH // SSD_HEADDIM
SSD_HPG = SSD_HEADS // SSD_GROUPS
SSD_STATE = 128
SSD_CHUNK = 128
CONV_WIDTH = 5
CONV_CH = SSD_WIDTH + 2 * SSD_GROUPS * SSD_STATE
EPS = 1e-6
QKV_COLS = 3 * len(DILATIONS) * ATTN_WIDTH
COL_Z = QKV_COLS
COL_XBC = COL_Z + ATTN_WIDTH + SSD_WIDTH
COL_DT = COL_XBC + CONV_CH
COL_GATE = COL_DT + 2 * SSD_HEADS
IN_COLS = COL_GATE + 2 * D_MODEL

LANES = 128
PROJ_TN = 512
QKV_TN = 768
WIDE_TN = 1024
V7X_VMEM_LIMIT_BYTES = 56 * 1024 * 1024


def _params(*sem):
    return pltpu.CompilerParams(dimension_semantics=sem, vmem_limit_bytes=V7X_VMEM_LIMIT_BYTES)


def _dot(a, b):
    return jnp.dot(a, b, preferred_element_type=F32)


def _sigmoid(x):
    return 0.5 * jnp.tanh(0.5 * x) + 0.5


def _split3(v):
    hi = v.astype(BF16)
    r = v - hi.astype(F32)
    mid = r.astype(BF16)
    lo = (r - mid.astype(F32)).astype(BF16)
    return hi, mid, lo


def _ada_body(c_ref, w_ref, b_ref, o_ref):
    o_ref[...] = _dot(c_ref[...].astype(BF16), w_ref[...].astype(BF16)) + b_ref[...]


def _ada(c, w, bias):
    bsz, d = c.shape
    n = w.shape[1]
    tn = 768
    return pl.pallas_call(
        _ada_body,
        grid=(n // tn,),
        in_specs=[pl.BlockSpec((bsz, d), lambda j: (0, 0)),
                  pl.BlockSpec((d, tn), lambda j: (0, j)),
                  pl.BlockSpec((1, tn), lambda j: (0, j))],
        out_specs=pl.BlockSpec((bsz, tn), lambda j: (0, j)),
        out_shape=jax.ShapeDtypeStruct((bsz, n), F32),
        compiler_params=_params("arbitrary"),
        name="ada",
    )(c, w, bias)


def _rope_body(pos_ref, inv_ref, cos_ref, sin_ref):
    s = pos_ref.shape[1]
    ang = pos_ref[0].astype(F32) * inv_ref[...]
    lane = lax.broadcasted_iota(jnp.int32, ang.shape, 1)
    sin = jnp.sin(ang)
    cos_ref[0, 0] = jnp.cos(ang)
    sin_ref[0, 0] = jnp.where(lane < HEAD_DIM // 2, -sin, sin)
    assert DILATIONS == (1, 4, 16)
    for tab in (cos_ref, sin_ref):
        for g in (1, 2):
            sub = s // DILATIONS[g]
            for r in range(DILATIONS[g]):
                start = r if g == 1 else (r % 4) * (s // 4) + r // 4
                tab[g, 0, r * sub:(r + 1) * sub, :] = tab[g - 1, 0, pl.ds(start, sub, stride=4), :]


def _rope_tables(pos, inv2):
    bsz, s, _ = pos.shape
    ng = len(DILATIONS)
    spec = pl.BlockSpec((ng, 1, s, HEAD_DIM), lambda b: (0, b, 0, 0))
    return pl.pallas_call(
        _rope_body,
        grid=(bsz,),
        in_specs=[pl.BlockSpec((1, s, 1), lambda b: (b, 0, 0)),
                  pl.BlockSpec((1, HEAD_DIM), lambda b: (0, 0))],
        out_specs=[spec, spec],
        out_shape=[jax.ShapeDtypeStruct((ng, bsz, s, HEAD_DIM), F32)] * 2,
        compiler_params=_params("arbitrary"),
        name="rope_tables",
    )(pos, inv2)


H_TILE = 512


def _h_body(x_ref, ada_ref, g_ref, h1_ref, h4_ref, h16_ref, hs_ref, hs4_ref):
    x = x_ref[0]
    ms = jnp.mean(x * x, axis=-1, keepdims=True)
    xn = x * lax.rsqrt(ms + EPS) * g_ref[...]
    h = xn * (1.0 + ada_ref[0, 1:2, :]) + ada_ref[0, 0:1, :]
    h1_ref[0] = h.astype(BF16)
    quarter = H_TILE // 4
    for cb in range(h.shape[1] // LANES):
        cols = slice(cb * LANES, (cb + 1) * LANES)
        hs_ref[cb] = h[:, cols]
        for r in range(4):
            blk = hs_ref[cb, pl.ds(r, quarter, stride=4), :]
            h4_ref[0, r, :, cols] = blk.astype(BF16)
            hs4_ref[cb, r * quarter:(r + 1) * quarter, :] = blk
        for r in range(16):
            rows = pl.ds((r % 4) * quarter + r // 4, H_TILE // 16, stride=4)
            h16_ref[0, r, :, cols] = hs4_ref[cb, rows, :].astype(BF16)


def _modulated_norm(x, ada3, g):
    bsz, s, d = x.shape
    t = H_TILE
    outs = pl.pallas_call(
        _h_body,
        grid=(bsz, s // t),
        in_specs=[pl.BlockSpec((1, t, d), lambda b, i: (b, i, 0)),
                  pl.BlockSpec((1, 3, d), lambda b, i: (b, 0, 0)),
                  pl.BlockSpec((1, d), lambda b, i: (0, 0))],
        out_specs=[pl.BlockSpec((1, t, d), lambda b, i: (b, i, 0)),
                   pl.BlockSpec((1, 4, t // 4, d), lambda b, i: (b, 0, i, 0)),
                   pl.BlockSpec((1, 16, t // 16, d), lambda b, i: (b, 0, i, 0))],
        out_shape=[jax.ShapeDtypeStruct((bsz, s, d), BF16),
                   jax.ShapeDtypeStruct((bsz, 4, s // 4, d), BF16),
                   jax.ShapeDtypeStruct((bsz, 16, s // 16, d), BF16)],
        scratch_shapes=[pltpu.VMEM((d // LANES, t, LANES), F32)] * 2,
        compiler_params=_params("arbitrary", "arbitrary"),
        name="modulated_norm",
    )(x, ada3, g)
    h1, h4, h16 = outs
    return h1, h4.reshape(bsz, s, d), h16.reshape(bsz, s, d)


PROJ_ROWS = 512


def _cast_weight_tile(w_ref, wbf_ref):
    @pl.when(pl.program_id(1) == 0)
    def _():
        wbf_ref[...] = w_ref[...].astype(BF16)


PROJ_SPLIT = (PROJ_ROWS,) * 4


def _proj_chunks(h_ref, wbf_ref, epilogue, sizes=PROJ_SPLIT):
    assert sum(sizes) == h_ref.shape[1]
    start = 0
    for size in sizes:
        rows = slice(start, start + size)
        epilogue(rows, _dot(h_ref[0, rows, :], wbf_ref[...]))
        start += size


def _proj_rope_body(h_ref, w_ref, cos_ref, sin_ref, o_ref, wbf_ref):
    _cast_weight_tile(w_ref, wbf_ref)
    j = pl.program_id(0)
    heads = o_ref.shape[1]
    n_rot = 2 * ATTN_HEADS // heads

    def rotate(rows, acc):
        cos = cos_ref[0, rows, :]
        sin = sin_ref[0, rows, :]
        for hh in range(heads):
            t = acc[:, hh * HEAD_DIM:(hh + 1) * HEAD_DIM]
            rot = t * cos + pltpu.roll(t, HEAD_DIM // 2, 1) * sin
            o_ref[0, hh, rows, :] = rot.astype(BF16)

    def plain(rows, acc):
        for hh in range(heads):
            o_ref[0, hh, rows, :] = acc[:, hh * HEAD_DIM:(hh + 1) * HEAD_DIM].astype(BF16)

    @pl.when(j < n_rot)
    def _():
        _proj_chunks(h_ref, wbf_ref, rotate)

    @pl.when(j >= n_rot)
    def _():
        _proj_chunks(h_ref, wbf_ref, plain)


def _proj_silu_body(h_ref, w_ref, o_ref, wbf_ref):
    _cast_weight_tile(w_ref, wbf_ref)

    def epilogue(rows, acc):
        o_ref[0, rows, :] = (acc * _sigmoid(acc)).astype(BF16)

    _proj_chunks(h_ref, wbf_ref, epilogue)


def _proj_sigmoid_body(h_ref, w_ref, o_ref, wbf_ref):
    _cast_weight_tile(w_ref, wbf_ref)

    def epilogue(rows, acc):
        o_ref[0, rows, :] = _sigmoid(acc).astype(BF16)

    _proj_chunks(h_ref, wbf_ref, epilogue)


CONV_PAD = 8
SUBLANES = 8
CONV_ROWS = 512
CONV_STRIDE = CONV_ROWS // SUBLANES + 1
CONV_TAIL = 3 * SUBLANES
CONV_CHUNKS = 4


def _proj_conv_body(h_ref, w_ref, cw_ref, cb_ref, o_ref, wbf_ref, *scratch):
    pads, stages = scratch[:CONV_CHUNKS], scratch[CONV_CHUNKS:]
    n_cb = PROJ_TN // LANES
    half = (CONV_WIDTH - 1) // 2
    end = CONV_PAD + CONV_ROWS
    _cast_weight_tile(w_ref, wbf_ref)
    pads[0][:, 0:CONV_PAD, :] = jnp.zeros((n_cb, CONV_PAD, LANES), F32)
    pads[-1][:, end:end + CONV_TAIL, :] = jnp.zeros((n_cb, CONV_TAIL, LANES), F32)

    def conv_chunk(m):
        base = m * CONV_ROWS
        for pair in range(n_cb // 2):
            cbs = (2 * pair, 2 * pair + 1)
            cols = [slice(cb * LANES, (cb + 1) * LANES) for cb in cbs]
            taps = [[cw_ref[k:k + 1, c] for k in range(CONV_WIDTH)] for c in cols]
            bias = [cb_ref[:, c] for c in cols]

            def tile(cb, i):
                return pads[m][cb, pl.ds(CONV_PAD + i, SUBLANES, stride=CONV_STRIDE), :]

            wins = [[tile(cb, i) for i in range(-half, half)] for cb in cbs]
            for i in range(CONV_STRIDE):
                acts = []
                for n, cb in enumerate(cbs):
                    win = wins[n]
                    win.append(tile(cb, i + half))
                    out = bias[n] + taps[n][0] * win[0]
                    for k in range(1, CONV_WIDTH):
                        out = out + taps[n][k] * win[k]
                    acts.append(out * _sigmoid(out))
                    win.pop(0)
                stages[m][pair, pl.ds(i, SUBLANES, stride=CONV_STRIDE), :] = pltpu.pack_elementwise(
                    acts, packed_dtype=BF16)
            packed = stages[m][pair, 0:CONV_ROWS, :]
            for n, c in enumerate(cols):
                o_ref[0, base:base + CONV_ROWS, c] = pltpu.unpack_elementwise(
                    packed, index=n, packed_dtype=BF16, unpacked_dtype=F32).astype(BF16)

    def epilogue(rows, acc):
        m = rows.start // CONV_ROWS
        for cb in range(n_cb):
            a = acc[:, cb * LANES:(cb + 1) * LANES]
            pads[m][cb, CONV_PAD:end, :] = a
            if m > 0:
                pads[m - 1][cb, end:end + CONV_TAIL, :] = a[0:CONV_TAIL]
            if m + 1 < CONV_CHUNKS:
                pads[m + 1][cb, 0:CONV_PAD, :] = a[CONV_ROWS - CONV_PAD:CONV_ROWS]
        if m > 0:
            conv_chunk(m - 1)

    _proj_chunks(h_ref, wbf_ref, epilogue, sizes=(CONV_ROWS,) * CONV_CHUNKS)
    conv_chunk(CONV_CHUNKS - 1)


def _rep_masks(shape):
    lane = lax.broadcasted_iota(jnp.int32, shape, len(shape) - 1) & (SSD_HEADS - 1)
    return lane < SSD_HPG, lane < 2 * SSD_HPG


def _packed_split(v):
    hi, mid, lo = _split3(v)
    m0, m1 = _rep_masks(v.shape)
    return jnp.where(m0, hi, jnp.where(m1, mid, lo))


def _replicate_heads(rolled, g, m0, m1):
    return jnp.where(m0, rolled[g], jnp.where(m1, rolled[g - 1], rolled[g - 2]))


def _proj_dt_body(h_ref, w_ref, bias_ref, o_ref, ot_ref, wbf_ref):
    s = h_ref.shape[1]
    _cast_weight_tile(w_ref, wbf_ref)
    acc = _dot(h_ref[0], wbf_ref[...]) + bias_ref[...]
    sp = jnp.maximum(acc, 0.0) + jnp.log1p(jnp.exp(-jnp.abs(acc)))
    n = 2 * SSD_HEADS
    m0, m1 = _rep_masks(sp.shape)
    rolled = {k: (sp if k == 0 else pltpu.roll(sp, (-k * SSD_HPG) % n, 1))
              for k in range(-2, SSD_GROUPS)}
    for g in range(SSD_GROUPS):
        o_ref[0, :, g * n:(g + 1) * n] = _replicate_heads(rolled, g, m0, m1)
    for c in range(s // SSD_CHUNK):
        t = sp[c * SSD_CHUNK:(c + 1) * SSD_CHUNK, :].T
        for g in range(SSD_GROUPS):
            for d in range(2):
                r0 = d * SSD_HEADS + g * SSD_HPG
                ot_ref[0, g, d, c * SSD_HPG:(c + 1) * SSD_HPG, :] = t[r0:r0 + SSD_HPG, :]


def _proj_dt(h, w_in, dt_bias):
    bsz, s, d = h.shape
    n = 2 * SSD_HEADS
    nc = s // SSD_CHUNK
    blk0 = COL_DT // n
    return pl.pallas_call(
        _proj_dt_body,
        grid=(1, bsz),
        in_specs=[pl.BlockSpec((1, s, d), lambda j, b: (b, 0, 0)),
                  pl.BlockSpec((d, n), lambda j, b: (0, blk0)),
                  pl.BlockSpec((1, n), lambda j, b: (0, 0))],
        out_specs=[pl.BlockSpec((1, s, SSD_GROUPS * n), lambda j, b: (b, 0, 0)),
                   pl.BlockSpec((1, SSD_GROUPS, 2, nc * SSD_HPG, SSD_CHUNK), lambda j, b: (b, 0, 0, 0, 0))],
        out_shape=[jax.ShapeDtypeStruct((bsz, s, SSD_GROUPS * n), F32),
                   jax.ShapeDtypeStruct((bsz, SSD_GROUPS, 2, nc * SSD_HPG, SSD_CHUNK), F32)],
        scratch_shapes=[pltpu.VMEM((d, n), BF16)],
        compiler_params=_params("arbitrary", "arbitrary"),
        name="proj_dt",
    )(h, w_in, dt_bias.reshape(1, n))


def _proj_call(body, h, w_in, col0, n_tiles, tn, out_cols, out_dtype, extra=(), extra_specs=(),
               scratch=(), name="proj"):
    bsz, s, d = h.shape
    blk0, rem = divmod(col0, tn)
    assert rem == 0
    return pl.pallas_call(
        body,
        grid=(n_tiles, bsz),
        in_specs=[pl.BlockSpec((1, s, d), lambda j, b: (b, 0, 0)),
                  pl.BlockSpec((d, tn), lambda j, b: (0, blk0 + j)),
                  *extra_specs],
        out_specs=pl.BlockSpec((1, s, tn), lambda j, b: (b, 0, j)),
        out_shape=jax.ShapeDtypeStruct((bsz, s, out_cols), out_dtype),
        scratch_shapes=[pltpu.VMEM((d, tn), BF16), *scratch],
        compiler_params=_params("arbitrary", "arbitrary"),
        name=name,
    )(h, w_in, *extra)


def _proj_qkv(h, w_in, cos, sin, group):
    bsz, s, d = h.shape
    tn = QKV_TN
    tps = ATTN_WIDTH // tn
    n_groups = len(DILATIONS)

    def wcol(j, b):
        return (0, (j // tps) * (n_groups * tps) + group * tps + j % tps)

    tab = pl.BlockSpec((1, s, HEAD_DIM), lambda j, b: (group * bsz + b, 0, 0))
    return pl.pallas_call(
        _proj_rope_body,
        grid=(3 * tps, bsz),
        in_specs=[pl.BlockSpec((1, s, d), lambda j, b: (b, 0, 0)),
                  pl.BlockSpec((d, tn), wcol), tab, tab],
        out_specs=pl.BlockSpec((1, tn // HEAD_DIM, s, HEAD_DIM), lambda j, b: (b, j, 0, 0)),
        out_shape=jax.ShapeDtypeStruct((bsz, 3 * ATTN_HEADS, s, HEAD_DIM), BF16),
        scratch_shapes=[pltpu.VMEM((d, tn), BF16)],
        compiler_params=_params("arbitrary", "arbitrary"),
        name=f"proj_qkv{group}",
    )(h, w_in, cos, sin)


ATT_BLK = 128


ATT_UNROLL = 16


def _attn_body(q1, k1, v1, q2, k2, v2, q3, k3, v3, z_ref, o_ref, on_ref, ls_ref, va_ref, bias_ref):
    s = o_ref.shape[1]
    scale = HEAD_DIM ** -0.5
    qi = lax.broadcasted_iota(jnp.int32, (ATT_BLK, 2 * ATT_BLK), 0)
    ki = lax.broadcasted_iota(jnp.int32, (ATT_BLK, 2 * ATT_BLK), 1)
    for t in range(3):
        bias_ref[t] = jnp.where(jnp.abs(qi - ki + t * N_SIDE) <= N_SIDE, 0.0, -jnp.inf)
    va_ref[:, :, HEAD_DIM:] = jnp.ones((len(DILATIONS), s, HEAD_DIM), BF16)

    def block(g, q_ref, k_ref, q0, k0, bias, nk):
        q = q_ref[0, 0, pl.ds(q0, ATT_BLK), :]
        kw = k_ref[0, 0, pl.ds(k0, nk), :]
        sc = lax.dot_general(q, kw, (((1,), (1,)), ((), ())), preferred_element_type=F32) * scale + bias
        m = jnp.max(sc, axis=-1, keepdims=True)
        p = jnp.exp(sc - m)
        pv = _dot(p.astype(BF16), va_ref[g, pl.ds(k0, nk), :])
        den = pv[:, HEAD_DIM:]
        return pv[:, :HEAD_DIM] / den, m + jnp.log(den)

    groups = ((q1, k1, v1), (q2, k2, v2), (q3, k3, v3))
    for g, (q_ref, k_ref, v_ref) in enumerate(groups):
        dil = DILATIONS[g]
        sub = s // dil
        nblk = sub // ATT_BLK
        va_ref[g, :, :HEAD_DIM] = v_ref[0, 0]
        for r in range(dil):
            if nblk == 1:
                o, lse = block(g, q_ref, k_ref, r * sub, r * sub, bias_ref[0, :, :ATT_BLK], ATT_BLK)
                on_ref[g, pl.ds(r, ATT_BLK, stride=dil), :] = o
                ls_ref[g, pl.ds(r, ATT_BLK, stride=dil), :] = lse
            else:
                def one(i, q_ref=q_ref, k_ref=k_ref, r=r, sub=sub, dil=dil, g=g):
                    loc = i * ATT_BLK
                    kloc = jnp.clip(loc - N_SIDE, 0, sub - 2 * ATT_BLK)
                    q0 = pl.multiple_of(r * sub + loc, ATT_BLK)
                    k0 = pl.multiple_of(r * sub + kloc, N_SIDE)
                    bias = bias_ref[lax.shift_right_logical(loc - kloc, N_SIDE.bit_length() - 1)]
                    o, lse = block(g, q_ref, k_ref, q0, k0, bias, 2 * ATT_BLK)
                    if dil == 1:
                        rows = pl.ds(q0, ATT_BLK)
                    else:
                        rows = pl.ds(loc * dil + r, ATT_BLK, stride=dil)
                    on_ref[g, rows, :] = o
                    ls_ref[g, rows, :] = lse

                un = min(ATT_UNROLL, nblk)

                def step(io, carry, one=one, un=un):
                    for j in range(un):
                        one(io * un + j)
                    return carry

                lax.fori_loop(0, nblk // un, step, 0)

    l0, l1, l2 = ls_ref[0], ls_ref[1], ls_ref[2]
    mx = jnp.maximum(jnp.maximum(l0, l1), l2)
    e0, e1, e2 = jnp.exp(l0 - mx), jnp.exp(l1 - mx), jnp.exp(l2 - mx)
    o = (e0 * on_ref[0] + e1 * on_ref[1] + e2 * on_ref[2]) / (e0 + e1 + e2)
    o_ref[0] = (o * z_ref[0].astype(F32)).astype(BF16)


def _attention(qkv, zs):
    bsz, _, s, _ = qkv[0].shape
    specs = []
    args = []
    for g in range(len(DILATIONS)):
        for sec in range(3):
            specs.append(pl.BlockSpec((1, 1, s, HEAD_DIM),
                                      lambda b, h, sec=sec: (b, sec * ATTN_HEADS + h, 0, 0)))
            args.append(qkv[g])
    specs.append(pl.BlockSpec((1, s, HEAD_DIM), lambda b, h: (b, 0, h)))
    args.append(zs)
    return pl.pallas_call(
        _attn_body,
        grid=(bsz, ATTN_HEADS),
        in_specs=specs,
        out_specs=pl.BlockSpec((1, s, HEAD_DIM), lambda b, h: (b, 0, h)),
        out_shape=jax.ShapeDtypeStruct((bsz, s, ATTN_WIDTH), BF16),
        scratch_shapes=[pltpu.VMEM((3, s, HEAD_DIM), F32), pltpu.VMEM((3, s, HEAD_DIM), F32),
                        pltpu.VMEM((len(DILATIONS), s, 2 * HEAD_DIM), BF16),
                        pltpu.VMEM((3, ATT_BLK, 2 * ATT_BLK), F32)],
        compiler_params=_params("arbitrary", "arbitrary"),
        name="dilated_attention",
    )(*args)


SSD_UNROLL = 16
SSD_BIG = 1e30


def _ssd_constants():
    q, gw = SSD_CHUNK, SSD_HPG * SSD_HEADDIM
    i = np.arange(q)
    low = i[None, :] <= i[:, None]
    upp = i[None, :] >= i[:, None]
    tri = np.stack([low, upp])
    tri3 = np.stack([np.tile(upp, (3, 1)), np.tile(low, (3, 1))])
    c = np.arange(LANES)[:, None]
    in_f = c < 3 * SSD_HPG
    in_b = (c >= SSD_HEADS) & (c < SSD_HEADS + 3 * SSD_HPG)
    head = np.arange(gw)[None, :] // SSD_HEADDIM
    efb = np.concatenate([in_f & (c % SSD_HPG == head), in_b & (c % SSD_HPG == head)], axis=1)
    k = np.arange(2 * SSD_HPG * LANES)[None, :] // LANES
    ecol = np.where(k < SSD_HPG, in_f & (c % SSD_HPG == k), in_b & (c % SSD_HPG == k - SSD_HPG))
    src = np.arange(2 * SSD_HPG * LANES)[None, :] % LANES
    dst = i[:, None]
    keep = np.where(k < SSD_HPG, dst >= src, dst <= src)
    rhs0 = np.concatenate([ecol.astype(np.float32), np.where(keep, 0.0, -SSD_BIG)], axis=0)
    return tuple(jnp.asarray(m, BF16) for m in (tri, tri3, efb, rhs0))


SSD_DYN_ROW0 = 2 * SSD_HPG
SSD_ONE_LANE0 = 3 * SSD_HPG


def _ssd_body(x_ref, b_ref, c_ref, dt_ref, dtt_ref, alog_ref, alogt_ref, dskip_ref, z_ref,
              tri_ref, tri3_ref, efb_ref, rhs0_ref, y_ref,
              pcum_ref, eexp_ref, cdx_ref, stb_ref, prev_ref, rdyn_ref, rhs_ref):
    q = SSD_CHUNK
    s = x_ref.shape[1]
    nc = s // q
    gw = SSD_HPG * SSD_HEADDIM
    fwd_lane = lax.broadcasted_iota(jnp.int32, (q, LANES), 1) < SSD_HEADS
    lane = lax.broadcasted_iota(jnp.int32, (q, LANES), 1)

    a_row = -jnp.exp(alog_ref[0])
    dt_all = dt_ref[0]
    a_all = dt_all * a_row
    a_wide = jnp.concatenate([a_all[c * q:(c + 1) * q] for c in range(nc)], axis=1)
    fwd_wide = (lax.broadcasted_iota(jnp.int32, a_wide.shape, 1) & (LANES - 1)) < SSD_HEADS
    both = jnp.concatenate([tri_ref[0], tri_ref[1]], axis=1)
    cum_wide = jnp.zeros(a_wide.shape, F32)
    for part in _split3(a_wide):
        zero = jnp.zeros_like(part)
        stacked = jnp.concatenate([jnp.where(fwd_wide, part, zero), jnp.where(fwd_wide, zero, part)], axis=0)
        cum_wide = cum_wide + _dot(both, stacked)

    dt_t = dtt_ref[0, 0]
    a_t = dt_t * -jnp.exp(alogt_ref[0])
    log_dt = jnp.where(dt_t > 0.0, jnp.log(dt_t), -SSD_BIG)
    neg_parts = []
    for d in range(2):
        parts = jnp.concatenate(_split3(a_t[d]), axis=1)
        src_term = _dot(parts, tri3_ref[d]) - log_dt[d]
        neg_parts.append([p.astype(F32) for p in _split3(-src_term)])
    sub = lax.broadcasted_iota(jnp.int32, (2 * SUBLANES, LANES), 0)
    for c in range(nc):
        blocks = []
        for k in range(2 * SSD_HPG):
            d, e = divmod(k, SSD_HPG)
            blk = jnp.where(sub == e, 1.0, 0.0) if d == 0 else jnp.zeros(sub.shape, F32)
            for p in range(3):
                r = c * SSD_HPG + e
                blk = jnp.where(sub == SUBLANES + p, neg_parts[d][p][r:r + 1, :], blk)
            blocks.append(blk)
        rdyn_ref[c] = jnp.concatenate(blocks, axis=1).astype(BF16)
    for j in range(SSD_UNROLL):
        rhs_ref[j] = rhs0_ref[...]

    bt = b_ref[0].T
    state_f = jnp.zeros((SSD_STATE, gw), F32)
    for c in range(nc):
        cum = cum_wide[:, c * LANES:(c + 1) * LANES]
        dt = dt_all[c * q:(c + 1) * q]
        ref = jnp.where(fwd_lane[0:1], cum[q - 1:q], cum[0:1])
        one_lane = (lane >= SSD_ONE_LANE0) & (lane < SSD_ONE_LANE0 + 3)
        pcum_ref[c] = jnp.where(one_lane, 1.0, _packed_split(cum).astype(F32)).astype(BF16)
        eexp_ref[c] = _dot(_packed_split(jnp.exp(cum)), efb_ref[...])
        wexp = _dot(_packed_split(dt * jnp.exp(ref - cum)), efb_ref[...])
        cdx = _dot(_packed_split(jnp.broadcast_to(jnp.exp(ref), (8, LANES))), efb_ref[...])[0:1]
        cdx_ref[c] = cdx
        xb = x_ref[0, c * q:(c + 1) * q, :].astype(F32)
        xw = (jnp.concatenate([xb, xb], axis=1) * wexp).astype(BF16)
        st = _dot(bt[:, c * q:(c + 1) * q], xw)
        stb_ref[c] = st[:, gw:]
        prev_ref[c, :, 0:gw] = state_f.astype(BF16)
        state_f = state_f * cdx[:, 0:gw] + st[:, 0:gw]
    state_b = jnp.zeros((SSD_STATE, gw), F32)
    for c in range(nc - 1, -1, -1):
        prev_ref[c, :, gw:] = state_b.astype(BF16)
        state_b = state_b * cdx_ref[c][:, gw:] + stb_ref[c]

    row = lax.broadcasted_iota(jnp.int32, (q, q), 0)
    col = lax.broadcasted_iota(jnp.int32, (q, q), 1)
    eye = jnp.where(row == col, 1.0, 0.0).astype(BF16)

    def chunk_out(c, slot):
        rows = pl.ds(pl.multiple_of(c * q, q), q)
        bc = b_ref[0, rows, :]
        cc = c_ref[0, rows, :]
        xb = x_ref[0, rows, :]
        cb = lax.dot_general(cc, bc, (((1,), (1,)), ((), ())), preferred_element_type=F32)
        rhs_ref[slot, SSD_DYN_ROW0:SSD_DYN_ROW0 + 2 * SUBLANES, :] = rdyn_ref[c]
        expo = _dot(jnp.concatenate([pcum_ref[c], eye], axis=1), rhs_ref[slot])
        ms = []
        for e in range(SSD_HPG):
            k = SSD_HPG + e
            w = jnp.exp(expo[:, e * LANES:(e + 1) * LANES]) + jnp.exp(expo[:, k * LANES:(k + 1) * LANES])
            ms.append((cb * w).astype(BF16))
        ys = []
        for p in range(SSD_HPG // 2):
            lhs = jnp.concatenate([ms[2 * p], ms[2 * p + 1]], axis=1)
            xp = xb[:, p * LANES:(p + 1) * LANES]
            zero = jnp.zeros_like(xp)
            rhs = jnp.concatenate([jnp.where(lane < SSD_HEADDIM, xp, zero),
                                   jnp.where(lane >= SSD_HEADDIM, xp, zero)], axis=0)
            ys.append(_dot(lhs, rhs))
        y = jnp.concatenate(ys, axis=1)
        yoff = _dot(cc, prev_ref[c]) * eexp_ref[c]
        y = y + yoff[:, 0:gw] + yoff[:, gw:]
        y = (y + dskip_ref[...] * xb.astype(F32)) * z_ref[0, rows, :].astype(F32)
        y_ref[0, rows, :] = y.astype(BF16)

    def chunk_step(io, carry):
        for j in range(SSD_UNROLL):
            chunk_out(io * SSD_UNROLL + j, j)
        return carry

    lax.fori_loop(0, nc // SSD_UNROLL, chunk_step, 0)


def _ssd(xbc, dtx, dtt, alog_g, alogt_g, dskip_x, zs):
    bsz, s, _ = xbc.shape
    q = SSD_CHUNK
    nc = s // q
    gw = SSD_HPG * SSD_HEADDIM
    nb = SSD_WIDTH // SSD_STATE
    consts = _ssd_constants()
    const_specs = [pl.BlockSpec(m.shape, lambda b, g, nd=m.ndim: (0,) * nd) for m in consts]
    return pl.pallas_call(
        _ssd_body,
        grid=(bsz, SSD_GROUPS),
        in_specs=[pl.BlockSpec((1, s, gw), lambda b, g: (b, 0, g)),
                  pl.BlockSpec((1, s, SSD_STATE), lambda b, g: (b, 0, nb + g)),
                  pl.BlockSpec((1, s, SSD_STATE), lambda b, g: (b, 0, nb + SSD_GROUPS + g)),
                  pl.BlockSpec((1, s, LANES), lambda b, g: (b, 0, g)),
                  pl.BlockSpec((1, 1, 2, nc * SSD_HPG, q), lambda b, g: (b, g, 0, 0, 0)),
                  pl.BlockSpec((1, 1, LANES), lambda b, g: (g, 0, 0)),
                  pl.BlockSpec((1, 2, nc * SSD_HPG, q), lambda b, g: (g, 0, 0, 0)),
                  pl.BlockSpec((1, gw), lambda b, g: (0, g)),
                  pl.BlockSpec((1, s, gw), lambda b, g: (b, 0, g)),
                  *const_specs],
        out_specs=pl.BlockSpec((1, s, gw), lambda b, g: (b, 0, g)),
        out_shape=jax.ShapeDtypeStruct((bsz, s, SSD_WIDTH), BF16),
        scratch_shapes=[pltpu.VMEM((nc, q, LANES), BF16),
                        pltpu.VMEM((nc, q, 2 * gw), F32),
                        pltpu.VMEM((nc, 1, 2 * gw), F32),
                        pltpu.VMEM((nc, SSD_STATE, gw), F32),
                        pltpu.VMEM((nc, SSD_STATE, 2 * gw), BF16),
                        pltpu.VMEM((nc, 2 * SUBLANES, 2 * SSD_HPG * LANES), BF16),
                        pltpu.VMEM((SSD_UNROLL, 2 * q, 2 * SSD_HPG * LANES), BF16)],
        compiler_params=_params("arbitrary", "arbitrary"),
        name="ssd",
    )(xbc, xbc, xbc, dtx, dtt, alog_g, alogt_g, dskip_x, zs, *consts)


TAIL_TM = 512
TAIL_A_TM = 256


def _tail_a_body(ya_ref, y_ref, g_ref, ng_ref, wa_ref, ws_ref, o_ref):
    d = o_ref.shape[1]
    ya = _dot(ya_ref[...], wa_ref[...])
    y = y_ref[...].astype(F32)
    yn = y * lax.rsqrt(jnp.mean(y * y, axis=-1, keepdims=True) + EPS) * ng_ref[...]
    ys = _dot(yn.astype(BF16), ws_ref[...])
    ga = g_ref[:, 0:d].astype(F32)
    gs = g_ref[:, d:2 * d].astype(F32)
    o_ref[...] = (ga * ya + gs * ys).astype(BF16)


def _tail_a(ya_in, y, gates, norm_g, wa, ws):
    n, d = ya_in.shape[0], wa.shape[1]
    tm = TAIL_A_TM
    resident = functools.partial(pl.BlockSpec, pipeline_mode=pl.Buffered(1))
    return pl.pallas_call(
        _tail_a_body,
        grid=(n // tm,),
        in_specs=[pl.BlockSpec((tm, ya_in.shape[1]), lambda i: (i, 0)),
                  pl.BlockSpec((tm, y.shape[1]), lambda i: (i, 0)),
                  pl.BlockSpec((tm, gates.shape[1]), lambda i: (i, 0)),
                  pl.BlockSpec((1, y.shape[1]), lambda i: (0, 0)),
                  resident(wa.shape, lambda i: (0, 0)),
                  resident(ws.shape, lambda i: (0, 0))],
        out_specs=pl.BlockSpec((tm, d), lambda i: (i, 0)),
        out_shape=jax.ShapeDtypeStruct((n, d), BF16),
        compiler_params=_params("arbitrary"),
        name="tail_a",
    )(ya_in, y, gates, norm_g, wa, ws)


def _tail_b_body(m_ref, x_ref, ada_ref, w_ref, fg_ref, o_ref):
    t = _dot(m_ref[...], w_ref[...])
    xn = x_ref[...] + ada_ref[0, 2:3, :] * t
    o_ref[...] = xn * lax.rsqrt(jnp.mean(xn * xn, axis=-1, keepdims=True) + EPS) * fg_ref[...]


def _tail_b(merged, x2, ada3, w_out, final_g, seq):
    n, d = x2.shape
    tm = TAIL_TM
    per = seq // tm
    resident = functools.partial(pl.BlockSpec, pipeline_mode=pl.Buffered(1))
    return pl.pallas_call(
        _tail_b_body,
        grid=(n // tm,),
        in_specs=[pl.BlockSpec((tm, d), lambda i: (i, 0)),
                  pl.BlockSpec((tm, d), lambda i: (i, 0)),
                  pl.BlockSpec((1, 3, d), lambda i: (i // per, 0, 0)),
                  resident(w_out.shape, lambda i: (0, 0)),
                  pl.BlockSpec((1, d), lambda i: (0, 0))],
        out_specs=pl.BlockSpec((tm, d), lambda i: (i, 0)),
        out_shape=jax.ShapeDtypeStruct((n, d), F32),
        compiler_params=_params("arbitrary"),
        name="tail_b",
    )(merged, x2, ada3, w_out, final_g)


def _layer(x, ada3, pos_tabs, norm_g, w_in, conv_w, conv_b, dt_bias, a_log, d_skip, ssd_norm_g,
           w_br_attn, w_br_ssd, w_out, out_g):
    bsz, s, d = x.shape
    cos, sin = pos_tabs
    hs = _modulated_norm(x, ada3, norm_g[None])
    qkv = [_proj_qkv(hs[g], w_in, cos, sin, g) for g in range(len(DILATIONS))]
    h = hs[0]
    tn = PROJ_TN
    zs_a = _proj_call(_proj_silu_body, h, w_in, COL_Z, ATTN_WIDTH // QKV_TN, QKV_TN, ATTN_WIDTH, BF16,
                      name="proj_za")
    zs_s = _proj_call(_proj_silu_body, h, w_in, COL_Z + ATTN_WIDTH, SSD_WIDTH // WIDE_TN, WIDE_TN, SSD_WIDTH,
                      BF16, name="proj_zs")
    xbc = _proj_call(
        _proj_conv_body, h, w_in, COL_XBC, CONV_CH // tn, tn, CONV_CH, BF16,
        extra=(conv_w, conv_b[None]),
        extra_specs=(pl.BlockSpec((CONV_WIDTH, tn), lambda j, b: (0, j)),
                     pl.BlockSpec((1, tn), lambda j, b: (0, j))),
        scratch=(*[pltpu.VMEM((tn // LANES, CONV_PAD + CONV_ROWS + CONV_TAIL, LANES), F32)] * CONV_CHUNKS,
                 *[pltpu.VMEM((tn // (2 * LANES), SUBLANES * CONV_STRIDE, LANES), jnp.uint32)] * CONV_CHUNKS),
        name="proj_xbc")
    n_dt = 2 * SSD_HEADS
    dtx, dtt = _proj_dt(h, w_in, dt_bias)
    gates = _proj_call(_proj_sigmoid_body, h, w_in[:, COL_GATE:], 0, 2 * d // WIDE_TN, WIDE_TN, 2 * d,
                       BF16, name="proj_gates")

    ya_in = _attention(qkv, zs_a)

    a_flat = a_log.reshape(1, n_dt)
    rolled = {k: jnp.roll(a_flat, -k * SSD_HPG, axis=1) for k in range(-2, SSD_GROUPS)}
    m0, m1 = _rep_masks(a_flat.shape)
    alog_g = jnp.stack([_replicate_heads(rolled, g, m0, m1) for g in range(SSD_GROUPS)])
    nc = s // SSD_CHUNK
    alogt_g = a_log.reshape(2, SSD_GROUPS, SSD_HPG).transpose(1, 0, 2)
    alogt_g = jnp.broadcast_to(alogt_g[:, :, None, :, None], (SSD_GROUPS, 2, nc, SSD_HPG, SSD_CHUNK))
    alogt_g = alogt_g.reshape(SSD_GROUPS, 2, nc * SSD_HPG, SSD_CHUNK)
    dskip_x = jnp.repeat(d_skip, SSD_HEADDIM)[None]
    y = _ssd(xbc, dtx, dtt, alog_g, alogt_g, dskip_x, zs_s)

    n = bsz * s
    merged = _tail_a(ya_in.reshape(n, ATTN_WIDTH), y.reshape(n, SSD_WIDTH),
                     gates.reshape(n, gates.shape[-1]), ssd_norm_g[None],
                     w_br_attn.astype(BF16), w_br_ssd.astype(BF16))
    out = _tail_b(merged, x.reshape(n, d), ada3, w_out.astype(BF16), out_g, s)
    return out.reshape(bsz, s, d)


def kernel(x, c, positions, norm_g, w_ada, b_ada, w_in, conv_w, conv_b, dt_bias, a_log, d_skip,
           ssd_norm_g, w_br_attn, w_br_ssd, w_out, final_g):
    bsz, s, d = x.shape
    depth = w_in.shape[0]
    inv = ROPE_THETA ** (-jnp.arange(0, HEAD_DIM, 2, dtype=F32) / HEAD_DIM)
    inv2 = jnp.concatenate([inv, inv])[None]
    cos, sin = _rope_tables(positions[..., None], inv2)
    pos_tabs = (cos.reshape(-1, s, HEAD_DIM), sin.reshape(-1, s, HEAD_DIM))
    for i in range(depth):
        ada3 = _ada(c, w_ada[i], b_ada[i][None]).reshape(bsz, 3, d)
        assert depth == 1
        x = _layer(x, ada3, pos_tabs, norm_g[i], w_in[i], conv_w[i], conv_b[i], dt_bias[i], a_log[i],
                   d_skip[i], ssd_norm_g[i], w_br_attn[i], w_br_ssd[i], w_out[i], final_g[None])
    return x
```

```python
import functools

import numpy as np
import jax
import jax.numpy as jnp
from jax import lax
from jax.experimental import pallas as pl
from jax.experimental.pallas import tpu as pltpu

F32 = jnp.float32
BF16 = jnp.bfloat16

D_MODEL = 2048
HEAD_DIM = 128
ATTN_HEADS = 12
DILATIONS = (1, 4, 16)
N_SIDE = 64
ATTN_WIDTH = ATTN_HEADS * HEAD_DIM
ROPE_THETA = 10000.0
SSD_WIDTH = 2 * D_MODEL
SSD_HEADDIM = 64
SSD_GROUPS = 8
SSD_HEADS = SSD_WIDTH // SSD_HEADDIM
SSD_HPG = SSD_HEADS // SSD_GROUPS
SSD_STATE = 128
SSD_CHUNK = 128
CONV_WIDTH = 5
CONV_CH = SSD_WIDTH + 2 * SSD_GROUPS * SSD_STATE
EPS = 1e-6
QKV_COLS = 3 * len(DILATIONS) * ATTN_WIDTH
COL_Z = QKV_COLS
COL_XBC = COL_Z + ATTN_WIDTH + SSD_WIDTH
COL_DT = COL_XBC + CONV_CH
COL_GATE = COL_DT + 2 * SSD_HEADS
IN_COLS = COL_GATE + 2 * D_MODEL

LANES = 128
PROJ_TN = 512
QKV_TN = 768
WIDE_TN = 1024
V7X_VMEM_LIMIT_BYTES = 56 * 1024 * 1024


def _params(*sem):
    return pltpu.CompilerParams(dimension_semantics=sem, vmem_limit_bytes=V7X_VMEM_LIMIT_BYTES)


def _dot(a, b):
    return jnp.dot(a, b, preferred_element_type=F32)


def _sigmoid(x):
    return 0.5 * jnp.tanh(0.5 * x) + 0.5


def _split3(v):
    hi = v.astype(BF16)
    r = v - hi.astype(F32)
    mid = r.astype(BF16)
    lo = (r - mid.astype(F32)).astype(BF16)
    return hi, mid, lo


def _ada_body(c_ref, w_ref, b_ref, o_ref):
    o_ref[...] = _dot(c_ref[...].astype(BF16), w_ref[...].astype(BF16)) + b_ref[...]


def _ada(c, w, bias):
    bsz, d = c.shape
    n = w.shape[1]
    tn = 768
    return pl.pallas_call(
        _ada_body,
        grid=(n // tn,),
        in_specs=[pl.BlockSpec((bsz, d), lambda j: (0, 0)),
                  pl.BlockSpec((d, tn), lambda j: (0, j)),
                  pl.BlockSpec((1, tn), lambda j: (0, j))],
        out_specs=pl.BlockSpec((bsz, tn), lambda j: (0, j)),
        out_shape=jax.ShapeDtypeStruct((bsz, n), F32),
        compiler_params=_params("arbitrary"),
        name="ada",
    )(c, w, bias)


def _rope_body(pos_ref, inv_ref, cos_ref, sin_ref):
    s = pos_ref.shape[1]
    ang = pos_ref[0].astype(F32) * inv_ref[...]
    lane = lax.broadcasted_iota(jnp.int32, ang.shape, 1)
    sin = jnp.sin(ang)
    cos_ref[0, 0] = jnp.cos(ang)
    sin_ref[0, 0] = jnp.where(lane < HEAD_DIM // 2, -sin, sin)
    assert DILATIONS == (1, 4, 16)
    for tab in (cos_ref, sin_ref):
        for g in (1, 2):
            sub = s // DILATIONS[g]
            for r in range(DILATIONS[g]):
                start = r if g == 1 else (r % 4) * (s // 4) + r // 4
                tab[g, 0, r * sub:(r + 1) * sub, :] = tab[g - 1, 0, pl.ds(start, sub, stride=4), :]


def _rope_tables(pos, inv2):
    bsz, s, _ = pos.shape
    ng = len(DILATIONS)
    spec = pl.BlockSpec((ng, 1, s, HEAD_DIM), lambda b: (0, b, 0, 0))
    return pl.pallas_call(
        _rope_body,
        grid=(bsz,),
        in_specs=[pl.BlockSpec((1, s, 1), lambda b: (b, 0, 0)),
                  pl.BlockSpec((1, HEAD_DIM), lambda b: (0, 0))],
        out_specs=[spec, spec],
        out_shape=[jax.ShapeDtypeStruct((ng, bsz, s, HEAD_DIM), F32)] * 2,
        compiler_params=_params("arbitrary"),
        name="rope_tables",
    )(pos, inv2)


H_TILE = 512


def _h_body(x_ref, ada_ref, g_ref, h1_ref, h4_ref, h16_ref, hs_ref, hs4_ref):
    x = x_ref[0]
    ms = jnp.mean(x * x, axis=-1, keepdims=True)
    xn = x * lax.rsqrt(ms + EPS) * g_ref[...]
    h = xn * (1.0 + ada_ref[0, 1:2, :]) + ada_ref[0, 0:1, :]
    h1_ref[0] = h.astype(BF16)
    quarter = H_TILE // 4
    for cb in range(h.shape[1] // LANES):
        cols = slice(cb * LANES, (cb + 1) * LANES)
        hs_ref[cb] = h[:, cols]
        for r in range(4):
            blk = hs_ref[cb, pl.ds(r, quarter, stride=4), :]
            h4_ref[0, r, :, cols] = blk.astype(BF16)
            hs4_ref[cb, r * quarter:(r + 1) * quarter, :] = blk
        for r in range(16):
            rows = pl.ds((r % 4) * quarter + r // 4, H_TILE // 16, stride=4)
            h16_ref[0, r, :, cols] = hs4_ref[cb, rows, :].astype(BF16)


def _modulated_norm(x, ada3, g):
    bsz, s, d = x.shape
    t = H_TILE
    outs = pl.pallas_call(
        _h_body,
        grid=(bsz, s // t),
        in_specs=[pl.BlockSpec((1, t, d), lambda b, i: (b, i, 0)),
                  pl.BlockSpec((1, 3, d), lambda b, i: (b, 0, 0)),
                  pl.BlockSpec((1, d), lambda b, i: (0, 0))],
        out_specs=[pl.BlockSpec((1, t, d), lambda b, i: (b, i, 0)),
                   pl.BlockSpec((1, 4, t // 4, d), lambda b, i: (b, 0, i, 0)),
                   pl.BlockSpec((1, 16, t // 16, d), lambda b, i: (b, 0, i, 0))],
        out_shape=[jax.ShapeDtypeStruct((bsz, s, d), BF16),
                   jax.ShapeDtypeStruct((bsz, 4, s // 4, d), BF16),
                   jax.ShapeDtypeStruct((bsz, 16, s // 16, d), BF16)],
        scratch_shapes=[pltpu.VMEM((d // LANES, t, LANES), F32)] * 2,
        compiler_params=_params("arbitrary", "arbitrary"),
        name="modulated_norm",
    )(x, ada3, g)
    h1, h4, h16 = outs
    return h1, h4.reshape(bsz, s, d), h16.reshape(bsz, s, d)


PROJ_ROWS = 512


def _cast_weight_tile(w_ref, wbf_ref):
    @pl.when(pl.program_id(1) == 0)
    def _():
        wbf_ref[...] = w_ref[...].astype(BF16)


PROJ_SPLIT = (PROJ_ROWS,) * 4


def _proj_chunks(h_ref, wbf_ref, epilogue, sizes=PROJ_SPLIT):
    assert sum(sizes) == h_ref.shape[1]
    start = 0
    for size in sizes:
        rows = slice(start, start + size)
        epilogue(rows, _dot(h_ref[0, rows, :], wbf_ref[...]))
        start += size


def _proj_rope_body(h_ref, w_ref, cos_ref, sin_ref, o_ref, wbf_ref):
    _cast_weight_tile(w_ref, wbf_ref)
    j = pl.program_id(0)
    heads = o_ref.shape[1]
    n_rot = 2 * ATTN_HEADS // heads

    def rotate(rows, acc):
        cos = cos_ref[0, rows, :]
        sin = sin_ref[0, rows, :]
        for hh in range(heads):
            t = acc[:, hh * HEAD_DIM:(hh + 1) * HEAD_DIM]
            rot = t * cos + pltpu.roll(t, HEAD_DIM // 2, 1) * sin
            o_ref[0, hh, rows, :] = rot.astype(BF16)

    def plain(rows, acc):
        for hh in range(heads):
            o_ref[0, hh, rows, :] = acc[:, hh * HEAD_DIM:(hh + 1) * HEAD_DIM].astype(BF16)

    @pl.when(j < n_rot)
    def _():
        _proj_chunks(h_ref, wbf_ref, rotate)

    @pl.when(j >= n_rot)
    def _():
        _proj_chunks(h_ref, wbf_ref, plain)


def _proj_silu_body(h_ref, w_ref, o_ref, wbf_ref):
    _cast_weight_tile(w_ref, wbf_ref)

    def epilogue(rows, acc):
        o_ref[0, rows, :] = (acc * _sigmoid(acc)).astype(BF16)

    _proj_chunks(h_ref, wbf_ref, epilogue)


def _proj_sigmoid_body(h_ref, w_ref, o_ref, wbf_ref):
    _cast_weight_tile(w_ref, wbf_ref)

    def epilogue(rows, acc):
        o_ref[0, rows, :] = _sigmoid(acc).astype(BF16)

    _proj_chunks(h_ref, wbf_ref, epilogue)


CONV_PAD = 8
SUBLANES = 8
CONV_ROWS = 512
CONV_STRIDE = CONV_ROWS // SUBLANES + 1
CONV_TAIL = 3 * SUBLANES
CONV_CHUNKS = 4


def _proj_conv_body(h_ref, w_ref, cw_ref, cb_ref, o_ref, wbf_ref, *scratch):
    pads, stages = scratch[:CONV_CHUNKS], scratch[CONV_CHUNKS:]
    n_cb = PROJ_TN // LANES
    half = (CONV_WIDTH - 1) // 2
    end = CONV_PAD + CONV_ROWS
    _cast_weight_tile(w_ref, wbf_ref)
    pads[0][:, 0:CONV_PAD, :] = jnp.zeros((n_cb, CONV_PAD, LANES), F32)
    pads[-1][:, end:end + CONV_TAIL, :] = jnp.zeros((n_cb, CONV_TAIL, LANES), F32)

    def conv_chunk(m):
        base = m * CONV_ROWS
        for pair in range(n_cb // 2):
            cbs = (2 * pair, 2 * pair + 1)
            cols = [slice(cb * LANES, (cb + 1) * LANES) for cb in cbs]
            taps = [[cw_ref[k:k + 1, c] for k in range(CONV_WIDTH)] for c in cols]
            bias = [cb_ref[:, c] for c in cols]

            def tile(cb, i):
                return pads[m][cb, pl.ds(CONV_PAD + i, SUBLANES, stride=CONV_STRIDE), :]

            wins = [[tile(cb, i) for i in range(-half, half)] for cb in cbs]
            for i in range(CONV_STRIDE):
                acts = []
                for n, cb in enumerate(cbs):
                    win = wins[n]
                    win.append(tile(cb, i + half))
                    out = bias[n] + taps[n][0] * win[0]
                    for k in range(1, CONV_WIDTH):
                        out = out + taps[n][k] * win[k]
                    acts.append(out * _sigmoid(out))
                    win.pop(0)
                stages[m][pair, pl.ds(i, SUBLANES, stride=CONV_STRIDE), :] = pltpu.pack_elementwise(
                    acts, packed_dtype=BF16)
            packed = stages[m][pair, 0:CONV_ROWS, :]
            for n, c in enumerate(cols):
                o_ref[0, base:base + CONV_ROWS, c] = pltpu.unpack_elementwise(
                    packed, index=n, packed_dtype=BF16, unpacked_dtype=F32).astype(BF16)

    def epilogue(rows, acc):
        m = rows.start // CONV_ROWS
        for cb in range(n_cb):
            a = acc[:, cb * LANES:(cb + 1) * LANES]
            pads[m][cb, CONV_PAD:end, :] = a
            if m > 0:
                pads[m - 1][cb, end:end + CONV_TAIL, :] = a[0:CONV_TAIL]
            if m + 1 < CONV_CHUNKS:
                pads[m + 1][cb, 0:CONV_PAD, :] = a[CONV_ROWS - CONV_PAD:CONV_ROWS]
        if m > 0:
            conv_chunk(m - 1)

    _proj_chunks(h_ref, wbf_ref, epilogue, sizes=(CONV_ROWS,) * CONV_CHUNKS)
    conv_chunk(CONV_CHUNKS - 1)


def _rep_masks(shape):
    lane = lax.broadcasted_iota(jnp.int32, shape, len(shape) - 1) & (SSD_HEADS - 1)
    return lane < SSD_HPG, lane < 2 * SSD_HPG


def _packed_split(v):
    hi, mid, lo = _split3(v)
    m0, m1 = _rep_masks(v.shape)
    return jnp.where(m0, hi, jnp.where(m1, mid, lo))


def _replicate_heads(rolled, g, m0, m1):
    return jnp.where(m0, rolled[g], jnp.where(m1, rolled[g - 1], rolled[g - 2]))


def _proj_dt_body(h_ref, w_ref, bias_ref, o_ref, ot_ref, wbf_ref):
    s = h_ref.shape[1]
    _cast_weight_tile(w_ref, wbf_ref)
    acc = _dot(h_ref[0], wbf_ref[...]) + bias_ref[...]
    sp = jnp.maximum(acc, 0.0) + jnp.log1p(jnp.exp(-jnp.abs(acc)))
    n = 2 * SSD_HEADS
    m0, m1 = _rep_masks(sp.shape)
    rolled = {k: (sp if k == 0 else pltpu.roll(sp, (-k * SSD_HPG) % n, 1))
              for k in range(-2, SSD_GROUPS)}
    for g in range(SSD_GROUPS):
        o_ref[0, :, g * n:(g + 1) * n] = _replicate_heads(rolled, g, m0, m1)
    for c in range(s // SSD_CHUNK):
        t = sp[c * SSD_CHUNK:(c + 1) * SSD_CHUNK, :].T
        for g in range(SSD_GROUPS):
            for d in range(2):
                r0 = d * SSD_HEADS + g * SSD_HPG
                ot_ref[0, g, d, c * SSD_HPG:(c + 1) * SSD_HPG, :] = t[r0:r0 + SSD_HPG, :]


def _proj_dt(h, w_in, dt_bias):
    bsz, s, d = h.shape
    n = 2 * SSD_HEADS
    nc = s // SSD_CHUNK
    blk0 = COL_DT // n
    return pl.pallas_call(
        _proj_dt_body,
        grid=(1, bsz),
        in_specs=[pl.BlockSpec((1, s, d), lambda j, b: (b, 0, 0)),
                  pl.BlockSpec((d, n), lambda j, b: (0, blk0)),
                  pl.BlockSpec((1, n), lambda j, b: (0, 0))],
        out_specs=[pl.BlockSpec((1, s, SSD_GROUPS * n), lambda j, b: (b, 0, 0)),
                   pl.BlockSpec((1, SSD_GROUPS, 2, nc * SSD_HPG, SSD_CHUNK), lambda j, b: (b, 0, 0, 0, 0))],
        out_shape=[jax.ShapeDtypeStruct((bsz, s, SSD_GROUPS * n), F32),
                   jax.ShapeDtypeStruct((bsz, SSD_GROUPS, 2, nc * SSD_HPG, SSD_CHUNK), F32)],
        scratch_shapes=[pltpu.VMEM((d, n), BF16)],
        compiler_params=_params("arbitrary", "arbitrary"),
        name="proj_dt",
    )(h, w_in, dt_bias.reshape(1, n))


def _proj_call(body, h, w_in, col0, n_tiles, tn, out_cols, out_dtype, extra=(), extra_specs=(),
               scratch=(), name="proj"):
    bsz, s, d = h.shape
    blk0, rem = divmod(col0, tn)
    assert rem == 0
    return pl.pallas_call(
        body,
        grid=(n_tiles, bsz),
        in_specs=[pl.BlockSpec((1, s, d), lambda j, b: (b, 0, 0)),
                  pl.BlockSpec((d, tn), lambda j, b: (0, blk0 + j)),
                  *extra_specs],
        out_specs=pl.BlockSpec((1, s, tn), lambda j, b: (b, 0, j)),
        out_shape=jax.ShapeDtypeStruct((bsz, s, out_cols), out_dtype),
        scratch_shapes=[pltpu.VMEM((d, tn), BF16), *scratch],
        compiler_params=_params("arbitrary", "arbitrary"),
        name=name,
    )(h, w_in, *extra)


def _proj_qkv(h, w_in, cos, sin, group):
    bsz, s, d = h.shape
    tn = QKV_TN
    tps = ATTN_WIDTH // tn
    n_groups = len(DILATIONS)

    def wcol(j, b):
        return (0, (j // tps) * (n_groups * tps) + group * tps + j % tps)

    tab = pl.BlockSpec((1, s, HEAD_DIM), lambda j, b: (group * bsz + b, 0, 0))
    return pl.pallas_call(
        _proj_rope_body,
        grid=(3 * tps, bsz),
        in_specs=[pl.BlockSpec((1, s, d), lambda j, b: (b, 0, 0)),
                  pl.BlockSpec((d, tn), wcol), tab, tab],
        out_specs=pl.BlockSpec((1, tn // HEAD_DIM, s, HEAD_DIM), lambda j, b: (b, j, 0, 0)),
        out_shape=jax.ShapeDtypeStruct((bsz, 3 * ATTN_HEADS, s, HEAD_DIM), BF16),
        scratch_shapes=[pltpu.VMEM((d, tn), BF16)],
        compiler_params=_params("arbitrary", "arbitrary"),
        name=f"proj_qkv{group}",
    )(h, w_in, cos, sin)


ATT_BLK = 128


ATT_UNROLL = 16


def _attn_body(q1, k1, v1, q2, k2, v2, q3, k3, v3, z_ref, o_ref, on_ref, ls_ref, va_ref, bias_ref):
    s = o_ref.shape[1]
    scale = HEAD_DIM ** -0.5
    qi = lax.broadcasted_iota(jnp.int32, (ATT_BLK, 2 * ATT_BLK), 0)
    ki = lax.broadcasted_iota(jnp.int32, (ATT_BLK, 2 * ATT_BLK), 1)
    for t in range(3):
        bias_ref[t] = jnp.where(jnp.abs(qi - ki + t * N_SIDE) <= N_SIDE, 0.0, -jnp.inf)
    va_ref[:, :, HEAD_DIM:] = jnp.ones((len(DILATIONS), s, HEAD_DIM), BF16)

    def block(g, q_ref, k_ref, q0, k0, bias, nk):
        q = q_ref[0, 0, pl.ds(q0, ATT_BLK), :]
        kw = k_ref[0, 0, pl.ds(k0, nk), :]
        sc = lax.dot_general(q, kw, (((1,), (1,)), ((), ())), preferred_element_type=F32) * scale + bias
        m = jnp.max(sc, axis=-1, keepdims=True)
        p = jnp.exp(sc - m)
        pv = _dot(p.astype(BF16), va_ref[g, pl.ds(k0, nk), :])
        den = pv[:, HEAD_DIM:]
        return pv[:, :HEAD_DIM] / den, m + jnp.log(den)

    groups = ((q1, k1, v1), (q2, k2, v2), (q3, k3, v3))
    for g, (q_ref, k_ref, v_ref) in enumerate(groups):
        dil = DILATIONS[g]
        sub = s // dil
        nblk = sub // ATT_BLK
        va_ref[g, :, :HEAD_DIM] = v_ref[0, 0]
        for r in range(dil):
            if nblk == 1:
                o, lse = block(g, q_ref, k_ref, r * sub, r * sub, bias_ref[0, :, :ATT_BLK], ATT_BLK)
                on_ref[g, pl.ds(r, ATT_BLK, stride=dil), :] = o
                ls_ref[g, pl.ds(r, ATT_BLK, stride=dil), :] = lse
            else:
                def one(i, q_ref=q_ref, k_ref=k_ref, r=r, sub=sub, dil=dil, g=g):
                    loc = i * ATT_BLK
                    kloc = jnp.clip(loc - N_SIDE, 0, sub - 2 * ATT_BLK)
                    q0 = pl.multiple_of(r * sub + loc, ATT_BLK)
                    k0 = pl.multiple_of(r * sub + kloc, N_SIDE)
                    bias = bias_ref[lax.shift_right_logical(loc - kloc, N_SIDE.bit_length() - 1)]
                    o, lse = block(g, q_ref, k_ref, q0, k0, bias, 2 * ATT_BLK)
                    if dil == 1:
                        rows = pl.ds(q0, ATT_BLK)
                    else:
                        rows = pl.ds(loc * dil + r, ATT_BLK, stride=dil)
                    on_ref[g, rows, :] = o
                    ls_ref[g, rows, :] = lse

                un = min(ATT_UNROLL, nblk)

                def step(io, carry, one=one, un=un):
                    for j in range(un):
                        one(io * un + j)
                    return carry

                lax.fori_loop(0, nblk // un, step, 0)

    l0, l1, l2 = ls_ref[0], ls_ref[1], ls_ref[2]
    mx = jnp.maximum(jnp.maximum(l0, l1), l2)
    e0, e1, e2 = jnp.exp(l0 - mx), jnp.exp(l1 - mx), jnp.exp(l2 - mx)
    o = (e0 * on_ref[0] + e1 * on_ref[1] + e2 * on_ref[2]) / (e0 + e1 + e2)
    o_ref[0] = (o * z_ref[0].astype(F32)).astype(BF16)


def _attention(qkv, zs):
    bsz, _, s, _ = qkv[0].shape
    specs = []
    args = []
    for g in range(len(DILATIONS)):
        for sec in range(3):
            specs.append(pl.BlockSpec((1, 1, s, HEAD_DIM),
                                      lambda b, h, sec=sec: (b, sec * ATTN_HEADS + h, 0, 0)))
            args.append(qkv[g])
    specs.append(pl.BlockSpec((1, s, HEAD_DIM), lambda b, h: (b, 0, h)))
    args.append(zs)
    return pl.pallas_call(
        _attn_body,
        grid=(bsz, ATTN_HEADS),
        in_specs=specs,
        out_specs=pl.BlockSpec((1, s, HEAD_DIM), lambda b, h: (b, 0, h)),
        out_shape=jax.ShapeDtypeStruct((bsz, s, ATTN_WIDTH), BF16),
        scratch_shapes=[pltpu.VMEM((3, s, HEAD_DIM), F32), pltpu.VMEM((3, s, HEAD_DIM), F32),
                        pltpu.VMEM((len(DILATIONS), s, 2 * HEAD_DIM), BF16),
                        pltpu.VMEM((3, ATT_BLK, 2 * ATT_BLK), F32)],
        compiler_params=_params("arbitrary", "arbitrary"),
        name="dilated_attention",
    )(*args)


SSD_UNROLL = 16
SSD_BIG = 1e30


def _ssd_constants():
    q, gw = SSD_CHUNK, SSD_HPG * SSD_HEADDIM
    i = np.arange(q)
    low = i[None, :] <= i[:, None]
    upp = i[None, :] >= i[:, None]
    tri = np.stack([low, upp])
    tri3 = np.stack([np.tile(upp, (3, 1)), np.tile(low, (3, 1))])
    c = np.arange(LANES)[:, None]
    in_f = c < 3 * SSD_HPG
    in_b = (c >= SSD_HEADS) & (c < SSD_HEADS + 3 * SSD_HPG)
    head = np.arange(gw)[None, :] // SSD_HEADDIM
    efb = np.concatenate([in_f & (c % SSD_HPG == head), in_b & (c % SSD_HPG == head)], axis=1)
    k = np.arange(2 * SSD_HPG * LANES)[None, :] // LANES
    ecol = np.where(k < SSD_HPG, in_f & (c % SSD_HPG == k), in_b & (c % SSD_HPG == k - SSD_HPG))
    src = np.arange(2 * SSD_HPG * LANES)[None, :] % LANES
    dst = i[:, None]
    keep = np.where(k < SSD_HPG, dst >= src, dst <= src)
    rhs0 = np.concatenate([ecol.astype(np.float32), np.where(keep, 0.0, -SSD_BIG)], axis=0)
    return tuple(jnp.asarray(m, BF16) for m in (tri, tri3, efb, rhs0))


SSD_DYN_ROW0 = 2 * SSD_HPG
SSD_ONE_LANE0 = 3 * SSD_HPG


def _ssd_body(x_ref, b_ref, c_ref, dt_ref, dtt_ref, alog_ref, alogt_ref, dskip_ref, z_ref,
              tri_ref, tri3_ref, efb_ref, rhs0_ref, y_ref,
              pcum_ref, eexp_ref, cdx_ref, stb_ref, prev_ref, rdyn_ref, rhs_ref):
    q = SSD_CHUNK
    s = x_ref.shape[1]
    nc = s // q
    gw = SSD_HPG * SSD_HEADDIM
    fwd_lane = lax.broadcasted_iota(jnp.int32, (q, LANES), 1) < SSD_HEADS
    lane = lax.broadcasted_iota(jnp.int32, (q, LANES), 1)

    @pl.when((pl.program_id(0) == 0) & (pl.program_id(1) == 0))
    def _():
        for j in range(SSD_UNROLL):
            rhs_ref[j] = rhs0_ref[...]

    a_row = -jnp.exp(alog_ref[0])
    dt_all = dt_ref[0]
    a_all = dt_all * a_row
    a_wide = jnp.concatenate([a_all[c * q:(c + 1) * q] for c in range(nc)], axis=1)
    fwd_wide = (lax.broadcasted_iota(jnp.int32, a_wide.shape, 1) & (LANES - 1)) < SSD_HEADS
    both = jnp.concatenate([tri_ref[0], tri_ref[1]], axis=1)
    cum_wide = jnp.zeros(a_wide.shape, F32)
    for part in _split3(a_wide):
        zero = jnp.zeros_like(part)
        stacked = jnp.concatenate([jnp.where(fwd_wide, part, zero), jnp.where(fwd_wide, zero, part)], axis=0)
        cum_wide = cum_wide + _dot(both, stacked)

    dt_t = dtt_ref[0, 0]
    a_t = dt_t * -jnp.exp(alogt_ref[0])
    log_dt = jnp.where(dt_t > 0.0, jnp.log(dt_t), -SSD_BIG)
    neg_parts = []
    for d in range(2):
        parts = jnp.concatenate(_split3(a_t[d]), axis=1)
        src_term = _dot(parts, tri3_ref[d]) - log_dt[d]
        neg_parts.append([p.astype(F32) for p in _split3(-src_term)])
    sub = lax.broadcasted_iota(jnp.int32, (2 * SUBLANES, LANES), 0)
    for c in range(nc):
        blocks = []
        for k in range(2 * SSD_HPG):
            d, e = divmod(k, SSD_HPG)
            blk = jnp.where(sub == e, 1.0, 0.0) if d == 0 else jnp.zeros(sub.shape, F32)
            for p in range(3):
                r = c * SSD_HPG + e
                blk = jnp.where(sub == SUBLANES + p, neg_parts[d][p][r:r + 1, :], blk)
            blocks.append(blk)
        rdyn_ref[c] = jnp.concatenate(blocks, axis=1).astype(BF16)

    bt = b_ref[0].T
    state_f = jnp.zeros((SSD_STATE, gw), F32)
    for c in range(nc):
        cum = cum_wide[:, c * LANES:(c + 1) * LANES]
        dt = dt_all[c * q:(c + 1) * q]
        ref = jnp.where(fwd_lane[0:1], cum[q - 1:q], cum[0:1])
        one_lane = (lane >= SSD_ONE_LANE0) & (lane < SSD_ONE_LANE0 + 3)
        pcum_ref[c] = jnp.where(one_lane, 1.0, _packed_split(cum).astype(F32)).astype(BF16)
        eexp_ref[c] = _dot(_packed_split(jnp.exp(cum)), efb_ref[...])
        wexp = _dot(_packed_split(dt * jnp.exp(ref - cum)), efb_ref[...])
        cdx = _dot(_packed_split(jnp.broadcast_to(jnp.exp(ref), (8, LANES))), efb_ref[...])[0:1]
        cdx_ref[c] = cdx
        xb = x_ref[0, c * q:(c + 1) * q, :].astype(F32)
        xw = (jnp.concatenate([xb, xb], axis=1) * wexp).astype(BF16)
        st = _dot(bt[:, c * q:(c + 1) * q], xw)
        stb_ref[c] = st[:, gw:]
        prev_ref[c, :, 0:gw] = state_f.astype(BF16)
        state_f = state_f * cdx[:, 0:gw] + st[:, 0:gw]
    state_b = jnp.zeros((SSD_STATE, gw), F32)
    for c in range(nc - 1, -1, -1):
        prev_ref[c, :, gw:] = state_b.astype(BF16)
        state_b = state_b * cdx_ref[c][:, gw:] + stb_ref[c]

    row = lax.broadcasted_iota(jnp.int32, (q, q), 0)
    col = lax.broadcasted_iota(jnp.int32, (q, q), 1)
    eye = jnp.where(row == col, 1.0, 0.0).astype(BF16)

    def chunk_out(c, slot):
        rows = pl.ds(pl.multiple_of(c * q, q), q)
        bc = b_ref[0, rows, :]
        cc = c_ref[0, rows, :]
        xb = x_ref[0, rows, :]
        cb = lax.dot_general(cc, bc, (((1,), (1,)), ((), ())), preferred_element_type=F32)
        rhs_ref[slot, SSD_DYN_ROW0:SSD_DYN_ROW0 + 2 * SUBLANES, :] = rdyn_ref[c]
        expo = _dot(jnp.concatenate([pcum_ref[c], eye], axis=1), rhs_ref[slot])
        ms = []
        for e in range(SSD_HPG):
            k = SSD_HPG + e
            w = jnp.exp(expo[:, e * LANES:(e + 1) * LANES]) + jnp.exp(expo[:, k * LANES:(k + 1) * LANES])
            ms.append((cb * w).astype(BF16))
        ys = []
        for p in range(SSD_HPG // 2):
            lhs = jnp.concatenate([ms[2 * p], ms[2 * p + 1]], axis=1)
            xp = xb[:, p * LANES:(p + 1) * LANES]
            zero = jnp.zeros_like(xp)
            rhs = jnp.concatenate([jnp.where(lane < SSD_HEADDIM, xp, zero),
                                   jnp.where(lane >= SSD_HEADDIM, xp, zero)], axis=0)
            ys.append(_dot(lhs, rhs))
        y = jnp.concatenate(ys, axis=1)
        yoff = _dot(cc, prev_ref[c]) * eexp_ref[c]
        y = y + yoff[:, 0:gw] + yoff[:, gw:]
        y = (y + dskip_ref[...] * xb.astype(F32)) * z_ref[0, rows, :].astype(F32)
        y_ref[0, rows, :] = y.astype(BF16)

    def chunk_step(io, carry):
        for j in range(SSD_UNROLL):
            chunk_out(io * SSD_UNROLL + j, j)
        return carry

    lax.fori_loop(0, nc // SSD_UNROLL, chunk_step, 0)


def _ssd(xbc, dtx, dtt, alog_g, alogt_g, dskip_x, zs):
    bsz, s, _ = xbc.shape
    q = SSD_CHUNK
    nc = s // q
    gw = SSD_HPG * SSD_HEADDIM
    nb = SSD_WIDTH // SSD_STATE
    consts = _ssd_constants()
    const_specs = [pl.BlockSpec(m.shape, lambda b, g, nd=m.ndim: (0,) * nd) for m in consts]
    return pl.pallas_call(
        _ssd_body,
        grid=(bsz, SSD_GROUPS),
        in_specs=[pl.BlockSpec((1, s, gw), lambda b, g: (b, 0, g)),
                  pl.BlockSpec((1, s, SSD_STATE), lambda b, g: (b, 0, nb + g)),
                  pl.BlockSpec((1, s, SSD_STATE), lambda b, g: (b, 0, nb + SSD_GROUPS + g)),
                  pl.BlockSpec((1, s, LANES), lambda b, g: (b, 0, g)),
                  pl.BlockSpec((1, 1, 2, nc * SSD_HPG, q), lambda b, g: (b, g, 0, 0, 0)),
                  pl.BlockSpec((1, 1, LANES), lambda b, g: (g, 0, 0)),
                  pl.BlockSpec((1, 2, nc * SSD_HPG, q), lambda b, g: (g, 0, 0, 0)),
                  pl.BlockSpec((1, gw), lambda b, g: (0, g)),
                  pl.BlockSpec((1, s, gw), lambda b, g: (b, 0, g)),
                  *const_specs],
        out_specs=pl.BlockSpec((1, s, gw), lambda b, g: (b, 0, g)),
        out_shape=jax.ShapeDtypeStruct((bsz, s, SSD_WIDTH), BF16),
        scratch_shapes=[pltpu.VMEM((nc, q, LANES), BF16),
                        pltpu.VMEM((nc, q, 2 * gw), F32),
                        pltpu.VMEM((nc, 1, 2 * gw), F32),
                        pltpu.VMEM((nc, SSD_STATE, gw), F32),
                        pltpu.VMEM((nc, SSD_STATE, 2 * gw), BF16),
                        pltpu.VMEM((nc, 2 * SUBLANES, 2 * SSD_HPG * LANES), BF16),
                        pltpu.VMEM((SSD_UNROLL, 2 * q, 2 * SSD_HPG * LANES), BF16)],
        compiler_params=_params("arbitrary", "arbitrary"),
        name="ssd",
    )(xbc, xbc, xbc, dtx, dtt, alog_g, alogt_g, dskip_x, zs, *consts)


TAIL_TM = 512
TAIL_A_TM = 256


def _tail_a_body(ya_ref, y_ref, g_ref, ng_ref, wa_ref, ws_ref, o_ref):
    d = o_ref.shape[1]
    ya = _dot(ya_ref[...], wa_ref[...])
    y = y_ref[...].astype(F32)
    yn = y * lax.rsqrt(jnp.mean(y * y, axis=-1, keepdims=True) + EPS) * ng_ref[...]
    ys = _dot(yn.astype(BF16), ws_ref[...])
    ga = g_ref[:, 0:d].astype(F32)
    gs = g_ref[:, d:2 * d].astype(F32)
    o_ref[...] = (ga * ya + gs * ys).astype(BF16)


def _tail_a(ya_in, y, gates, norm_g, wa, ws):
    n, d = ya_in.shape[0], wa.shape[1]
    tm = TAIL_A_TM
    resident = functools.partial(pl.BlockSpec, pipeline_mode=pl.Buffered(1))
    return pl.pallas_call(
        _tail_a_body,
        grid=(n // tm,),
        in_specs=[pl.BlockSpec((tm, ya_in.shape[1]), lambda i: (i, 0)),
                  pl.BlockSpec((tm, y.shape[1]), lambda i: (i, 0)),
                  pl.BlockSpec((tm, gates.shape[1]), lambda i: (i, 0)),
                  pl.BlockSpec((1, y.shape[1]), lambda i: (0, 0)),
                  resident(wa.shape, lambda i: (0, 0)),
                  resident(ws.shape, lambda i: (0, 0))],
        out_specs=pl.BlockSpec((tm, d), lambda i: (i, 0)),
        out_shape=jax.ShapeDtypeStruct((n, d), BF16),
        compiler_params=_params("arbitrary"),
        name="tail_a",
    )(ya_in, y, gates, norm_g, wa, ws)


def _tail_b_body(m_ref, x_ref, ada_ref, w_ref, fg_ref, o_ref):
    t = _dot(m_ref[...], w_ref[...])
    xn = x_ref[...] + ada_ref[0, 2:3, :] * t
    o_ref[...] = xn * lax.rsqrt(jnp.mean(xn * xn, axis=-1, keepdims=True) + EPS) * fg_ref[...]


def _tail_b(merged, x2, ada3, w_out, final_g, seq):
    n, d = x2.shape
    tm = TAIL_TM
    per = seq // tm
    resident = functools.partial(pl.BlockSpec, pipeline_mode=pl.Buffered(1))
    return pl.pallas_call(
        _tail_b_body,
        grid=(n // tm,),
        in_specs=[pl.BlockSpec((tm, d), lambda i: (i, 0)),
                  pl.BlockSpec((tm, d), lambda i: (i, 0)),
                  pl.BlockSpec((1, 3, d), lambda i: (i // per, 0, 0)),
                  resident(w_out.shape, lambda i: (0, 0)),
                  pl.BlockSpec((1, d), lambda i: (0, 0))],
        out_specs=pl.BlockSpec((tm, d), lambda i: (i, 0)),
        out_shape=jax.ShapeDtypeStruct((n, d), F32),
        compiler_params=_params("arbitrary"),
        name="tail_b",
    )(merged, x2, ada3, w_out, final_g)


def _layer(x, ada3, pos_tabs, norm_g, w_in, conv_w, conv_b, dt_bias, a_log, d_skip, ssd_norm_g,
           w_br_attn, w_br_ssd, w_out, out_g):
    bsz, s, d = x.shape
    cos, sin = pos_tabs
    hs = _modulated_norm(x, ada3, norm_g[None])
    qkv = [_proj_qkv(hs[g], w_in, cos, sin, g) for g in range(len(DILATIONS))]
    h = hs[0]
    tn = PROJ_TN
    zs_a = _proj_call(_proj_silu_body, h, w_in, COL_Z, ATTN_WIDTH // QKV_TN, QKV_TN, ATTN_WIDTH, BF16,
                      name="proj_za")
    zs_s = _proj_call(_proj_silu_body, h, w_in, COL_Z + ATTN_WIDTH, SSD_WIDTH // WIDE_TN, WIDE_TN, SSD_WIDTH,
                      BF16, name="proj_zs")
    xbc = _proj_call(
        _proj_conv_body, h, w_in, COL_XBC, CONV_CH // tn, tn, CONV_CH, BF16,
        extra=(conv_w, conv_b[None]),
        extra_specs=(pl.BlockSpec((CONV_WIDTH, tn), lambda j, b: (0, j)),
                     pl.BlockSpec((1, tn), lambda j, b: (0, j))),
        scratch=(*[pltpu.VMEM((tn // LANES, CONV_PAD + CONV_ROWS + CONV_TAIL, LANES), F32)] * CONV_CHUNKS,
                 *[pltpu.VMEM((tn // (2 * LANES), SUBLANES * CONV_STRIDE, LANES), jnp.uint32)] * CONV_CHUNKS),
        name="proj_xbc")
    n_dt = 2 * SSD_HEADS
    dtx, dtt = _proj_dt(h, w_in, dt_bias)
    gates = _proj_call(_proj_sigmoid_body, h, w_in[:, COL_GATE:], 0, 2 * d // WIDE_TN, WIDE_TN, 2 * d,
                       BF16, name="proj_gates")

    ya_in = _attention(qkv, zs_a)

    a_flat = a_log.reshape(1, n_dt)
    rolled = {k: jnp.roll(a_flat, -k * SSD_HPG, axis=1) for k in range(-2, SSD_GROUPS)}
    m0, m1 = _rep_masks(a_flat.shape)
    alog_g = jnp.stack([_replicate_heads(rolled, g, m0, m1) for g in range(SSD_GROUPS)])
    nc = s // SSD_CHUNK
    alogt_g = a_log.reshape(2, SSD_GROUPS, SSD_HPG).transpose(1, 0, 2)
    alogt_g = jnp.broadcast_to(alogt_g[:, :, None, :, None], (SSD_GROUPS, 2, nc, SSD_HPG, SSD_CHUNK))
    alogt_g = alogt_g.reshape(SSD_GROUPS, 2, nc * SSD_HPG, SSD_CHUNK)
    dskip_x = jnp.repeat(d_skip, SSD_HEADDIM)[None]
    y = _ssd(xbc, dtx, dtt, alog_g, alogt_g, dskip_x, zs_s)

    n = bsz * s
    merged = _tail_a(ya_in.reshape(n, ATTN_WIDTH), y.reshape(n, SSD_WIDTH),
                     gates.reshape(n, gates.shape[-1]), ssd_norm_g[None],
                     w_br_attn.astype(BF16), w_br_ssd.astype(BF16))
    out = _tail_b(merged, x.reshape(n, d), ada3, w_out.astype(BF16), out_g, s)
    return out.reshape(bsz, s, d)


def kernel(x, c, positions, norm_g, w_ada, b_ada, w_in, conv_w, conv_b, dt_bias, a_log, d_skip,
           ssd_norm_g, w_br_attn, w_br_ssd, w_out, final_g):
    bsz, s, d = x.shape
    depth = w_in.shape[0]
    inv = ROPE_THETA ** (-jnp.arange(0, HEAD_DIM, 2, dtype=F32) / HEAD_DIM)
    inv2 = jnp.concatenate([inv, inv])[None]
    cos, sin = _rope_tables(positions[..., None], inv2)
    pos_tabs = (cos.reshape(-1, s, HEAD_DIM), sin.reshape(-1, s, HEAD_DIM))
    for i in range(depth):
        ada3 = _ada(c, w_ada[i], b_ada[i][None]).reshape(bsz, 3, d)
        assert depth == 1
        x = _layer(x, ada3, pos_tabs, norm_g[i], w_in[i], conv_w[i], conv_b[i], dt_bias[i], a_log[i],
                   d_skip[i], ssd_norm_g[i], w_br_attn[i], w_br_ssd[i], w_out[i], final_g[None])
    return x
```
